```python
import math, functools
import jax, jax.numpy as jnp
from jax import lax
import numpy as np

D_MODEL = 2048
BATCH = 4
SEQ = 8192
DEPTH = 1
DEC_BATCH = 8
DEC_SEQ = 32
PAST_LEN = 1024

CHUNK = 64
HEAD_DIM = 64
A_HEADS = 16
A_KV_HEADS = 4
A_WINDOW = 128
A_PAST_CHUNKS = A_WINDOW // CHUNK
B_HEADS = 16
B_PAST_CHUNKS = 8
B_REL_CLIP = 256
T5_BUCKETS = 32
T5_MAX_DIST = (A_PAST_CHUNKS + 1) * CHUNK
N_EXPERTS = 32
TOP_K = 4
D_FF = D_MODEL
SWIGLU_ALPHA = 1.702
SWIGLU_LIMIT = 7.0
NORM_EPS = 1e-6
NEG_INF = -1e30
A_QW = A_HEADS * HEAD_DIM
A_KVW = A_KV_HEADS * HEAD_DIM
B_W = B_HEADS * HEAD_DIM
IN_SPLITS = (A_QW, A_KVW, A_KVW, B_W, B_W, B_W, D_MODEL, D_MODEL)
IN_WIDTH = sum(IN_SPLITS)

kernel_name = "hybrid_chunk_stream_encoder_step"


def _keep(n_past_chunks):
    return min(n_past_chunks * CHUNK, PAST_LEN)


def rms_norm(x, g):
    xf = x.astype(jnp.float32)
    y = xf * lax.rsqrt(jnp.mean(xf * xf, axis=-1, keepdims=True) + NORM_EPS)
    return (y * g.astype(jnp.float32)).astype(x.dtype)


def t5_bucket(rel):
    nb = T5_BUCKETS // 2
    max_exact = nb // 2
    ret = (rel > 0).astype(jnp.int32) * nb
    n = jnp.abs(rel)
    large = max_exact + (jnp.log(jnp.maximum(n, 1).astype(jnp.float32) / max_exact)
                         / math.log(T5_MAX_DIST / max_exact) * (nb - max_exact)).astype(jnp.int32)
    large = jnp.minimum(large, nb - 1)
    return ret + jnp.where(n < max_exact, n, large)


def t5_bias(rel, table):
    return jnp.transpose(table[t5_bucket(rel)], (2, 0, 1))


def clip_bias(rel, table):
    idx = jnp.clip(rel, -B_REL_CLIP, CHUNK - 1) + B_REL_CLIP
    return jnp.transpose(table[idx], (2, 0, 1))


def project(xn, w_in, gq_a, gk_a, gq_b, gk_b):
    h = xn @ w_in
    cuts = [int(c) for c in np.cumsum(IN_SPLITS)[:-1]]
    qa, ka, va, qb, kb, vb, ga, gb = jnp.split(h, cuts, axis=-1)
    lead = h.shape[:-1]
    heads = lambda t, n: t.reshape(*lead, n, HEAD_DIM)
    qa = rms_norm(heads(qa, A_HEADS), gq_a)
    ka = rms_norm(heads(ka, A_KV_HEADS), gk_a)
    qb = rms_norm(heads(qb, B_HEADS), gq_b)
    kb = rms_norm(heads(kb, B_HEADS), gk_b)
    return (qa, ka, heads(va, A_KV_HEADS), qb, kb, heads(vb, B_HEADS),
            jax.nn.sigmoid(ga), jax.nn.sigmoid(gb))


def band_attend(q, k, v, bias, valid, sink):
    n, nq, h, d = q.shape
    l, kvh = k.shape[1], k.shape[2]
    g = h // kvh
    qg = q.reshape(n, nq, kvh, g, d).astype(jnp.float32)
    s = jnp.einsum('nqkgd,nlkd->nkgql', qg, k.astype(jnp.float32)) * (d ** -0.5)
    s = s + bias.reshape(kvh, g, nq, l).astype(jnp.float32)
    if valid is not None:
        s = jnp.where(valid, s, NEG_INF)
    if sink is None:
        p = jax.nn.softmax(s, axis=-1)
    else:
        sk = jnp.broadcast_to(sink.astype(jnp.float32).reshape(kvh, g, 1, 1), s.shape[:-1] + (1,))
        p = jax.nn.softmax(jnp.concatenate([s, sk], axis=-1), axis=-1)[..., :-1]
    o = jnp.einsum('nkgql,nlkd->nqkgd', p, v.astype(jnp.float32))
    return o.reshape(n, nq, h, d).astype(v.dtype)


def prompt_band(q, k, v, n_past, bias_fn, sink):
    b, s, h, d = q.shape
    pad = n_past * CHUNK
    l = pad + CHUNK
    nc = s // CHUNK
    kp = jnp.pad(k, ((0, 0), (pad, 0), (0, 0), (0, 0)))
    vp = jnp.pad(v, ((0, 0), (pad, 0), (0, 0), (0, 0)))
    j = jnp.arange(l)
    i = jnp.arange(CHUNK)
    bias = bias_fn(j[None, :] - pad - i[:, None])

    def one_chunk(c):
        start = c * CHUNK
        qc = lax.dynamic_slice_in_dim(q, start, CHUNK, axis=1)
        kb = lax.dynamic_slice_in_dim(kp, start, l, axis=1)
        vb = lax.dynamic_slice_in_dim(vp, start, l, axis=1)
        valid = (start + j - pad) >= 0
        return band_attend(qc, kb, vb, bias, valid, sink)

    o = lax.map(one_chunk, jnp.arange(nc))
    o = jnp.moveaxis(o, 0, 1).reshape(b, s, h, d)
    keep = _keep(n_past)
    return o, kp[:, -keep:], vp[:, -keep:]


def sample_band(q, k, v, cache_k, cache_v, bias_fn, sink):
    t = q.shape[1]
    keep = cache_k.shape[1]
    kb = jnp.concatenate([cache_k.astype(k.dtype), k], axis=1)
    vb = jnp.concatenate([cache_v.astype(v.dtype), v], axis=1)
    qpos = PAST_LEN + jnp.arange(t)
    kpos = jnp.concatenate([PAST_LEN - keep + jnp.arange(keep), qpos])
    bias = bias_fn(kpos[None, :] - qpos[:, None])
    return band_attend(q, kb, vb, bias, None, sink)


def merge(oa, ob, ga, gb, w_oa, w_ob, w_out):
    lead = oa.shape[:2]
    ya = oa.reshape(*lead, A_QW) @ w_oa
    yb = ob.reshape(*lead, B_W) @ w_ob
    return (ga * ya + gb * yb) @ w_out


def moe(x, w_r, b_r, w1, b1, w2, b2):
    shape = x.shape
    x = x.reshape(-1, shape[-1])
    n, d = x.shape
    logits = x.astype(jnp.float32) @ w_r.astype(jnp.float32) + b_r.astype(jnp.float32)
    top_v, top_i = lax.top_k(logits, TOP_K)
    gates = jax.nn.softmax(top_v, axis=-1)
    nk = n * TOP_K
    blk = max(8, min(256, nk // N_EXPERTS))
    nblk = -(-nk // blk) + N_EXPERTS
    rows = nblk * blk
    flat_e = top_i.reshape(-1)
    flat_tok = jnp.arange(nk, dtype=jnp.int32) // TOP_K
    flat_g = gates.reshape(-1)
    order = jnp.argsort(flat_e)
    se = flat_e[order]
    counts = jnp.zeros((N_EXPERTS,), jnp.int32).at[flat_e].add(1)
    padded = (counts + blk - 1) // blk * blk
    pad_end = jnp.cumsum(padded)
    pad_start = pad_end - padded
    start = jnp.cumsum(counts) - counts
    dest = pad_start[se] + jnp.arange(nk, dtype=jnp.int32) - start[se]
    row_tok = jnp.zeros((rows,), jnp.int32).at[dest].set(flat_tok[order])
    row_gate = jnp.zeros((rows,), jnp.float32).at[dest].set(flat_g[order])
    blk_exp = jnp.minimum(jnp.searchsorted(pad_end, jnp.arange(nblk) * blk, side='right'),
                          N_EXPERTS - 1)

    def expert_block(args):
        tok, gate_w, e = args
        h = x[tok] @ w1[e] + b1[e]
        hg, hu = jnp.split(h, 2, axis=-1)
        hg = jnp.minimum(hg, SWIGLU_LIMIT)
        hu = jnp.clip(hu, -SWIGLU_LIMIT, SWIGLU_LIMIT)
        act = hg * jax.nn.sigmoid(SWIGLU_ALPHA * hg) * (hu + 1)
        out = act @ w2[e] + b2[e]
        return out.astype(jnp.float32) * gate_w[:, None]

    ys = lax.map(expert_block, (row_tok.reshape(nblk, blk), row_gate.reshape(nblk, blk), blk_exp))
    y = jax.ops.segment_sum(ys.reshape(rows, d), row_tok, num_segments=n)
    return y.astype(x.dtype).reshape(shape)


def setup_inputs(seed: int = 0) -> dict:
    key = jax.random.key(seed)
    ks = jax.random.split(key, 24)
    f32 = jnp.float32
    nrm = lambda k, s, sc: jax.random.normal(k, s, f32) * sc
    a_keep = _keep(A_PAST_CHUNKS)
    b_keep = _keep(B_PAST_CHUNKS)
    return {
        "x_prompt": nrm(ks[0], (BATCH, SEQ, D_MODEL), 1.0),
        "x_sample": nrm(ks[1], (DEC_BATCH, DEC_SEQ, D_MODEL), 1.0),
        "cache_a_k": nrm(ks[2], (DEPTH, DEC_BATCH, a_keep, A_KV_HEADS, HEAD_DIM), 1.0),
        "cache_a_v": nrm(ks[3], (DEPTH, DEC_BATCH, a_keep, A_KV_HEADS, HEAD_DIM), 1.0),
        "cache_b_k": nrm(ks[4], (DEPTH, DEC_BATCH, b_keep, B_HEADS, HEAD_DIM), 1.0),
        "cache_b_v": nrm(ks[5], (DEPTH, DEC_BATCH, b_keep, B_HEADS, HEAD_DIM), 1.0),
        "attn_norm": 1.0 + nrm(ks[6], (DEPTH, D_MODEL), 0.02),
        "w_in": nrm(ks[7], (DEPTH, D_MODEL, IN_WIDTH), D_MODEL ** -0.5),
        "a_q_norm": 1.0 + nrm(ks[8], (DEPTH, HEAD_DIM), 0.02),
        "a_k_norm": 1.0 + nrm(ks[9], (DEPTH, HEAD_DIM), 0.02),
        "b_q_norm": 1.0 + nrm(ks[10], (DEPTH, HEAD_DIM), 0.02),
        "b_k_norm": 1.0 + nrm(ks[11], (DEPTH, HEAD_DIM), 0.02),
        "a_sinks": nrm(ks[12], (DEPTH, A_HEADS), 0.5),
        "t5_table": nrm(ks[13], (T5_BUCKETS, A_HEADS), 0.5),
        "b_rel_table": nrm(ks[14], (DEPTH, B_REL_CLIP + CHUNK, B_HEADS), 0.5),
        "w_oa": nrm(ks[15], (DEPTH, A_QW, D_MODEL), A_QW ** -0.5),
        "w_ob": nrm(ks[16], (DEPTH, B_W, D_MODEL), B_W ** -0.5),
        "w_out": nrm(ks[17], (DEPTH, D_MODEL, D_MODEL), D_MODEL ** -0.5),
        "ffn_norm": 1.0 + nrm(ks[18], (DEPTH, D_MODEL), 0.02),
        "router_w": nrm(ks[19], (DEPTH, D_MODEL, N_EXPERTS), D_MODEL ** -0.5),
        "router_b": nrm(ks[20], (DEPTH, N_EXPERTS), 0.01),
        "w1": nrm(ks[21], (DEPTH, N_EXPERTS, D_MODEL, 2 * D_FF), D_MODEL ** -0.5),
        "b1": nrm(ks[22], (DEPTH, N_EXPERTS, 2 * D_FF), 0.01),
        "w2": nrm(ks[23], (DEPTH, N_EXPERTS, D_FF, D_MODEL), D_FF ** -0.5),
        "b2": nrm(jax.random.fold_in(ks[23], 1), (DEPTH, N_EXPERTS, D_MODEL), 0.01),
    }


def reference(x_prompt, x_sample, cache_a_k, cache_a_v, cache_b_k, cache_b_v, attn_norm, w_in,
              a_q_norm, a_k_norm, b_q_norm, b_k_norm, a_sinks, t5_table, b_rel_table, w_oa, w_ob,
              w_out, ffn_norm, router_w, router_b, w1, b1, w2, b2):
    xp, xs = x_prompt, x_sample
    p_ak, p_av, p_bk, p_bv, s_ak, s_av, s_bk, s_bv = [], [], [], [], [], [], [], []
    bias_a = functools.partial(t5_bias, table=t5_table)
    for l in range(DEPTH):
        bias_b = functools.partial(clip_bias, table=b_rel_table[l])
        norms = (a_q_norm[l], a_k_norm[l], b_q_norm[l], b_k_norm[l])
        qa, ka, va, qb, kb, vb, ga, gb = project(rms_norm(xp, attn_norm[l]), w_in[l], *norms)
        oa, pak, pav = prompt_band(qa, ka, va, A_PAST_CHUNKS, bias_a, a_sinks[l])
        ob, pbk, pbv = prompt_band(qb, kb, vb, B_PAST_CHUNKS, bias_b, None)
        xp = xp + merge(oa, ob, ga, gb, w_oa[l], w_ob[l], w_out[l])
        xp = xp + moe(rms_norm(xp, ffn_norm[l]), router_w[l], router_b[l], w1[l], b1[l], w2[l], b2[l])
        qa, ka, va, qb, kb, vb, ga, gb = project(rms_norm(xs, attn_norm[l]), w_in[l], *norms)
        oa = sample_band(qa, ka, va, cache_a_k[l], cache_a_v[l], bias_a, a_sinks[l])
        ob = sample_band(qb, kb, vb, cache_b_k[l], cache_b_v[l], bias_b, None)
        xs = xs + merge(oa, ob, ga, gb, w_oa[l], w_ob[l], w_out[l])
        xs = xs + moe(rms_norm(xs, ffn_norm[l]), router_w[l], router_b[l], w1[l], b1[l], w2[l], b2[l])
        p_ak.append(pak); p_av.append(pav); p_bk.append(pbk); p_bv.append(pbv)
        s_ak.append(ka); s_av.append(va); s_bk.append(kb); s_bv.append(vb)
    return (xp, xs, jnp.stack(p_ak), jnp.stack(p_av), jnp.stack(p_bk), jnp.stack(p_bv),
            jnp.stack(s_ak), jnp.stack(s_av), jnp.stack(s_bk), jnp.stack(s_bv))
```

```python
import functools
import math

import jax
import jax.numpy as jnp
from jax import lax
from jax.experimental import pallas as pl
from jax.experimental.pallas import tpu as pltpu

F32 = jnp.float32
BF16 = jnp.bfloat16

D_MODEL = 2048
CHUNK = 64
HEAD_DIM = 64
A_HEADS = 16
A_KV_HEADS = 4
A_PAST_CHUNKS = 2
B_HEADS = 16
B_PAST_CHUNKS = 8
B_REL_CLIP = 256
T5_BUCKETS = 32
T5_MAX_DIST = (A_PAST_CHUNKS + 1) * CHUNK
N_EXPERTS = 32
TOP_K = 4
D_FF = D_MODEL
SWIGLU_ALPHA = 1.702
SWIGLU_LIMIT = 7.0
NORM_EPS = 1e-6
NEG_INF = -1e30
PAST_LEN = 1024

A_QW = A_HEADS * HEAD_DIM
A_KVW = A_KV_HEADS * HEAD_DIM
B_W = B_HEADS * HEAD_DIM
PA = A_PAST_CHUNKS * CHUNK
PB = B_PAST_CHUNKS * CHUNK
LA = PA + CHUNK
LB = PB + CHUNK

LANES = 128
MXU_COLS = 256

A_KVD = 2 * A_KVW
PROJ_W = A_QW + 3 * B_W + 2 * D_MODEL + 2 * A_KVD
PROJ_TN = 512
KV32_W = 2 * B_W + 2 * A_KVD
_NORM_TILES = (0, 1, 2, 3, 4, 5, 16)
_SIGMOID_LO, _SIGMOID_HI = 8, 16

MOE_TM = 512
MOE_TF = 512
COMB_TT = 256
MERGE_TM = 256


def _cparams(sem, vmem_mb):
    return pltpu.CompilerParams(dimension_semantics=sem, vmem_limit_bytes=vmem_mb * 1024 * 1024)


def _proj_body(x_ref, g_ref, w_ref, cg_ref, gm_ref, h_ref, kv_ref, xn_s):
    j = pl.program_id(1)

    @pl.when(j == 0)
    def _():
        x = x_ref[...]
        ms = jnp.mean(x * x, axis=-1, keepdims=True)
        xn_s[...] = (x * lax.rsqrt(ms + NORM_EPS) * g_ref[...]).astype(BF16)

    acc = jnp.dot(xn_s[...], w_ref[...], preferred_element_type=F32)
    is_norm = functools.reduce(jnp.logical_or, [j == t for t in _NORM_TILES])
    is_sig = jnp.logical_and(j >= _SIGMOID_LO, j < _SIGMOID_HI)
    is_kv = jnp.logical_or(jnp.logical_and(j >= 4, j < 8), j >= 16)

    @pl.when(is_norm)
    def _():
        parts = []
        for c in range(PROJ_TN // MXU_COLS):
            a = acc[:, c * MXU_COLS:(c + 1) * MXU_COLS]
            ss = jnp.dot((a * a).astype(BF16), gm_ref[...], preferred_element_type=F32)
            parts.append(a * lax.rsqrt(ss * (1.0 / HEAD_DIM) + NORM_EPS))
        y = jnp.concatenate(parts, axis=-1) * cg_ref[...]
        h_ref[...] = y.astype(BF16)

        @pl.when(is_kv)
        def _():
            kv_ref[...] = y

    @pl.when(is_sig)
    def _():
        h_ref[...] = jax.nn.sigmoid(acc).astype(BF16)

    @pl.when(jnp.logical_not(jnp.logical_or(is_norm, is_sig)))
    def _():
        h_ref[...] = acc.astype(BF16)

        @pl.when(is_kv)
        def _():
            kv_ref[...] = acc


def _kv_tile(j):
    return jnp.clip(j - 4, 0, 3) + (j >= 16).astype(jnp.int32) + (j >= 17).astype(jnp.int32)


def _proj(x2d, g, w_perm, cg, gm, tm):
    n = x2d.shape[0]
    grid = (n // tm, PROJ_W // PROJ_TN)
    return pl.pallas_call(
        _proj_body,
        grid=grid,
        in_specs=[
            pl.BlockSpec((tm, D_MODEL), lambda i, j: (i, 0)),
            pl.BlockSpec((1, D_MODEL), lambda i, j: (0, 0)),
            pl.BlockSpec((D_MODEL, PROJ_TN), lambda i, j: (0, j)),
            pl.BlockSpec((1, PROJ_TN), lambda i, j: (0, j)),
            pl.BlockSpec((MXU_COLS, MXU_COLS), lambda i, j: (0, 0)),
        ],
        out_specs=[
            pl.BlockSpec((tm, PROJ_TN), lambda i, j: (i, j)),
            pl.BlockSpec((tm, PROJ_TN), lambda i, j: (i, _kv_tile(j))),
        ],
        out_shape=[
            jax.ShapeDtypeStruct((n, PROJ_W), BF16),
            jax.ShapeDtypeStruct((n, KV32_W), F32),
        ],
        scratch_shapes=[pltpu.VMEM((tm, D_MODEL), BF16)],
        compiler_params=_cparams(("parallel", "arbitrary"), 48),
        name="proj",
    )(x2d, g, w_perm, cg, gm)


def _attn_body(sink_ref, qa_ref, qb_ref, kbp_ref, kbc_ref, vbp_ref, vbc_ref,
               kap_ref, kac_ref, vap_ref, vac_ref, ba_ref, bb_ref,
               oa_ref, ob_ref, kb_s, vb_s, ka_s, va_s, *, cpb, first_pos, hi_a, hi_b):
    i = pl.program_id(1)
    qb_rows = cpb * CHUNK
    kb_s[0:PB, :] = kbp_ref[0]
    kb_s[PB:PB + qb_rows, :] = kbc_ref[0]
    vb_s[0:PB, :] = vbp_ref[0]
    vb_s[PB:PB + qb_rows, :] = vbc_ref[0]
    ka_s[0:PA, :] = kap_ref[0]
    ka_s[PA:PA + qb_rows, :] = kac_ref[0]
    va_s[0:PA, :] = vap_ref[0]
    va_s[PA:PA + qb_rows, :] = vac_ref[0]

    lane_a = lax.broadcasted_iota(jnp.int32, (1, LA), 1)
    lane_b = lax.broadcasted_iota(jnp.int32, (1, LB), 1)
    low_half = lax.broadcasted_iota(jnp.int32, (1, LANES), 1) < HEAD_DIM
    nt = (((1,), (1,)), ((), ()))

    def one_head(q_p, k_p, v_p, bias, valid, half, sink):
        qm = jnp.where(low_half if half == 0 else jnp.logical_not(low_half), q_p, jnp.zeros_like(q_p))
        s = lax.dot_general(qm, k_p, nt, preferred_element_type=F32) + bias
        s = jnp.where(valid, s, NEG_INF)
        m = jnp.max(s, axis=-1, keepdims=True)
        if sink is not None:
            m = jnp.maximum(m, sink)
        e = jnp.exp(s - m)
        l = jnp.sum(e, axis=-1, keepdims=True)
        if sink is not None:
            l = l + jnp.exp(sink - m)
        o = jnp.dot(e.astype(BF16), v_p, preferred_element_type=F32)
        return o / l

    def chunk(jc, carry):
        r0 = pl.multiple_of(jc * CHUNK, CHUNK)
        start = first_pos + (i * cpb + jc) * CHUNK
        valid_a = jnp.logical_and(lane_a >= jnp.maximum(PA - start, 0), lane_a < hi_a)
        valid_b = jnp.logical_and(lane_b >= jnp.maximum(PB - start, 0), lane_b < hi_b)
        for p in range(A_HEADS // 2):
            g = p // 2
            cols = slice(p * LANES, (p + 1) * LANES)
            kcols = slice(g * LANES, (g + 1) * LANES)
            q_p = qa_ref[0, pl.ds(r0, CHUNK), cols]
            k_p = ka_s[pl.ds(r0, LA), kcols]
            v_p = va_s[pl.ds(r0, LA), kcols]
            o0 = one_head(q_p, k_p, v_p, ba_ref[2 * p], valid_a, 0, sink_ref[2 * p])
            o1 = one_head(q_p, k_p, v_p, ba_ref[2 * p + 1], valid_a, 1, sink_ref[2 * p + 1])
            oa_ref[0, pl.ds(r0, CHUNK), cols] = jnp.where(low_half, o0, o1).astype(BF16)
        for p in range(B_HEADS // 2):
            cols = slice(p * LANES, (p + 1) * LANES)
            q_p = qb_ref[0, pl.ds(r0, CHUNK), cols]
            k_p = kb_s[pl.ds(r0, LB), cols]
            v_p = vb_s[pl.ds(r0, LB), cols]
            o0 = one_head(q_p, k_p, v_p, bb_ref[2 * p], valid_b, 0, None)
            o1 = one_head(q_p, k_p, v_p, bb_ref[2 * p + 1], valid_b, 1, None)
            ob_ref[0, pl.ds(r0, CHUNK), cols] = jnp.where(low_half, o0, o1).astype(BF16)
        return carry

    lax.fori_loop(0, cpb, chunk, 0)


def _attn(sinks, hq, kbp, kbc, vbp, vbc, kap, kac, vap, vac, bias_a, bias_b, *,
          cpb, first_pos, hi_a, hi_b, cols):
    nb, s, _ = hq.shape
    qb_rows = cpb * CHUNK
    nblk = s // qb_rows
    rb = qb_rows // PB if qb_rows >= PB else None
    ra = qb_rows // PA if qb_rows >= PA else None

    def prev_map(ratio, col):
        if ratio is None:
            return lambda b, i, *_: (b, 0, col)
        return lambda b, i, *_: (b, jnp.maximum(i * ratio - 1, 0), col)

    def cur_map(col):
        return lambda b, i, *_: (b, i, col)

    in_specs = [
        pl.BlockSpec((1, qb_rows, A_QW), cur_map(cols["qa"])),
        pl.BlockSpec((1, qb_rows, B_W), cur_map(cols["qb"])),
        pl.BlockSpec((1, PB, B_W), prev_map(rb, cols["kbp"])),
        pl.BlockSpec((1, qb_rows, B_W), cur_map(cols["kbc"])),
        pl.BlockSpec((1, PB, B_W), prev_map(rb, cols["vbp"])),
        pl.BlockSpec((1, qb_rows, B_W), cur_map(cols["vbc"])),
        pl.BlockSpec((1, PA, A_KVD), prev_map(ra, cols["kap"])),
        pl.BlockSpec((1, qb_rows, A_KVD), cur_map(cols["kac"])),
        pl.BlockSpec((1, PA, A_KVD), prev_map(ra, cols["vap"])),
        pl.BlockSpec((1, qb_rows, A_KVD), cur_map(cols["vac"])),
        pl.BlockSpec((A_HEADS, CHUNK, LA), lambda b, i, *_: (0, 0, 0)),
        pl.BlockSpec((B_HEADS, CHUNK, LB), lambda b, i, *_: (0, 0, 0)),
    ]
    out_specs = [
        pl.BlockSpec((1, qb_rows, A_QW), lambda b, i, *_: (b, i, 0)),
        pl.BlockSpec((1, qb_rows, B_W), lambda b, i, *_: (b, i, 0)),
    ]
    body = functools.partial(_attn_body, cpb=cpb, first_pos=first_pos, hi_a=hi_a, hi_b=hi_b)
    return pl.pallas_call(
        body,
        grid_spec=pltpu.PrefetchScalarGridSpec(
            num_scalar_prefetch=1,
            grid=(nb, nblk),
            in_specs=in_specs,
            out_specs=out_specs,
            scratch_shapes=[
                pltpu.VMEM((PB + qb_rows, B_W), BF16),
                pltpu.VMEM((PB + qb_rows, B_W), BF16),
                pltpu.VMEM((PA + qb_rows, A_KVD), BF16),
                pltpu.VMEM((PA + qb_rows, A_KVD), BF16),
            ],
        ),
        out_shape=[
            jax.ShapeDtypeStruct((nb, s, A_QW), BF16),
            jax.ShapeDtypeStruct((nb, s, B_W), BF16),
        ],
        compiler_params=_cparams(("parallel", "arbitrary"), 48),
        name="attn",
    )(sinks, hq, hq, kbp, kbc, vbp, vbc, kap, kac, vap, vac, bias_a, bias_b)


def _merge_body(oa_ref, ob_ref, ga_ref, gb_ref, x_ref, woa_ref, wob_ref, wout_ref,
                fg_ref, wr_ref, br_ref, x2_ref, xn_ref, lg_ref):
    ya = jnp.dot(oa_ref[...], woa_ref[...], preferred_element_type=F32)
    yb = jnp.dot(ob_ref[...], wob_ref[...], preferred_element_type=F32)
    z = ga_ref[...].astype(F32) * ya + gb_ref[...].astype(F32) * yb
    y = jnp.dot(z.astype(BF16), wout_ref[...], preferred_element_type=F32)
    x2 = x_ref[...] + y
    x2_ref[...] = x2
    ms = jnp.mean(x2 * x2, axis=-1, keepdims=True)
    xn = x2 * lax.rsqrt(ms + NORM_EPS) * fg_ref[...]
    xn_ref[...] = xn
    lg_ref[...] = jnp.dot(xn.astype(BF16), wr_ref[...], preferred_element_type=F32) + br_ref[...]


def _merge(oa, ob, h2d, x2d, w_oa, w_ob, w_out, fg, wr, br):
    n = x2d.shape[0]
    tm = MERGE_TM
    const = lambda i: (0, 0)
    resident = functools.partial(pl.BlockSpec, index_map=const, pipeline_mode=pl.Buffered(1))
    return pl.pallas_call(
        _merge_body,
        grid=(n // tm,),
        in_specs=[
            pl.BlockSpec((tm, A_QW), lambda i: (i, 0)),
            pl.BlockSpec((tm, B_W), lambda i: (i, 0)),
            pl.BlockSpec((tm, D_MODEL), lambda i: (i, 2)),
            pl.BlockSpec((tm, D_MODEL), lambda i: (i, 3)),
            pl.BlockSpec((tm, D_MODEL), lambda i: (i, 0)),
            resident((A_QW, D_MODEL)),
            resident((B_W, D_MODEL)),
            resident((D_MODEL, D_MODEL)),
            resident((1, D_MODEL)),
            resident((D_MODEL, LANES)),
            resident((1, LANES)),
        ],
        out_specs=[
            pl.BlockSpec((tm, D_MODEL), lambda i: (i, 0)),
            pl.BlockSpec((tm, D_MODEL), lambda i: (i, 0)),
            pl.BlockSpec((tm, LANES), lambda i: (i, 0)),
        ],
        out_shape=[
            jax.ShapeDtypeStruct((n, D_MODEL), F32),
            jax.ShapeDtypeStruct((n, D_MODEL), F32),
            jax.ShapeDtypeStruct((n, LANES), F32),
        ],
        compiler_params=_cparams(("parallel",), 48),
        name="merge",
    )(oa, ob, h2d, h2d, x2d, w_oa, w_ob, w_out, fg, wr, br)


def _row_gather(idx_ref, n, src_hbm, dst, sem):
    def body(r, c):
        t = idx_ref[0, 0, r]
        pltpu.make_async_copy(src_hbm.at[pl.ds(t, 1), :], dst.at[pl.ds(r, 1), :], sem).start()
        return c
    lax.fori_loop(0, n, body, 0)


def _expert_body(bexp_ref, nv_ref, tokc_ref, tokn_ref, gate_ref, xn_hbm,
                 w1g_ref, w1u_ref, b1g_ref, b1u_ref, w2_ref, b2_ref,
                 out_ref, xg, xb, sem):
    i = pl.program_id(0)
    f = pl.program_id(1)
    nf = pl.num_programs(1)
    nv = nv_ref[0]
    slot = i % 2

    @pl.when(i >= nv)
    def _():
        @pl.when(f == 0)
        def _():
            out_ref[...] = jnp.zeros_like(out_ref)

    @pl.when(i < nv)
    def _():
        @pl.when(f == 0)
        def _():
            @pl.when(i == 0)
            def _():
                _row_gather(tokc_ref, MOE_TM, xn_hbm, xg.at[0], sem.at[0])

            pltpu.make_async_copy(xn_hbm.at[pl.ds(0, MOE_TM), :], xg.at[slot], sem.at[slot]).wait()

            @pl.when(i + 1 < nv)
            def _():
                _row_gather(tokn_ref, MOE_TM, xn_hbm, xg.at[1 - slot], sem.at[1 - slot])

            xb[...] = xg[slot].astype(BF16)

        x = xb[...]
        hg = jnp.dot(x, w1g_ref[0], preferred_element_type=F32) + b1g_ref[0]
        hu = jnp.dot(x, w1u_ref[0], preferred_element_type=F32) + b1u_ref[0]
        hg = jnp.minimum(hg, SWIGLU_LIMIT)
        hu = jnp.clip(hu, -SWIGLU_LIMIT, SWIGLU_LIMIT)
        act = hg * jax.nn.sigmoid(SWIGLU_ALPHA * hg) * (hu + 1.0)
        part = jnp.dot(act.astype(BF16), w2_ref[0], preferred_element_type=F32)

        @pl.when(f == 0)
        def _():
            out_ref[...] = part

        @pl.when(f > 0)
        def _():
            out_ref[...] += part

        @pl.when(f == nf - 1)
        def _():
            out_ref[...] = (out_ref[...] + b2_ref[0]) * gate_ref[...]


def _experts(blk_exp, nvalid, row_tok3, row_gate, xn, w1, b1, w2, b2):
    nblk = row_tok3.shape[0]
    nf = D_FF // MOE_TF
    rows = nblk * MOE_TM

    def fz(i, f, nv):
        return jnp.where(i < nv[0], f, nf - 1)

    in_specs = [
        pl.BlockSpec((1, 1, MOE_TM), lambda i, f, be, nv: (i, 0, 0), memory_space=pltpu.SMEM),
        pl.BlockSpec((1, 1, MOE_TM), lambda i, f, be, nv: (jnp.minimum(i + 1, nblk - 1), 0, 0),
                     memory_space=pltpu.SMEM),
        pl.BlockSpec((MOE_TM, 1), lambda i, f, be, nv: (i, 0)),
        pl.BlockSpec(memory_space=pl.ANY),
        pl.BlockSpec((1, D_MODEL, MOE_TF), lambda i, f, be, nv: (be[i], 0, fz(i, f, nv))),
        pl.BlockSpec((1, D_MODEL, MOE_TF), lambda i, f, be, nv: (be[i], 0, nf + fz(i, f, nv))),
        pl.BlockSpec((1, 1, MOE_TF), lambda i, f, be, nv: (be[i], 0, fz(i, f, nv))),
        pl.BlockSpec((1, 1, MOE_TF), lambda i, f, be, nv: (be[i], 0, nf + fz(i, f, nv))),
        pl.BlockSpec((1, MOE_TF, D_MODEL), lambda i, f, be, nv: (be[i], fz(i, f, nv), 0)),
        pl.BlockSpec((1, 1, D_MODEL), lambda i, f, be, nv: (be[i], 0, 0)),
    ]
    return pl.pallas_call(
        _expert_body,
        grid_spec=pltpu.PrefetchScalarGridSpec(
            num_scalar_prefetch=2,
            grid=(nblk, nf),
            in_specs=in_specs,
            out_specs=pl.BlockSpec((MOE_TM, D_MODEL), lambda i, f, be, nv: (i, 0)),
            scratch_shapes=[
                pltpu.VMEM((2, MOE_TM, D_MODEL), F32),
                pltpu.VMEM((MOE_TM, D_MODEL), BF16),
                pltpu.SemaphoreType.DMA((2,)),
            ],
        ),
        out_shape=jax.ShapeDtypeStruct((rows, D_MODEL), F32),
        compiler_params=_cparams(("arbitrary", "arbitrary"), 52),
        name="experts",
    )(blk_exp, nvalid, row_tok3, row_tok3, row_gate, xn, w1, w1, b1, b1, w2, b2)


def _combine_body(posc_ref, posn_ref, x2_ref, ys_hbm, outp_ref, outs_ref, buf, sem, *, n_prompt_tiles):
    i = pl.program_id(0)
    n = pl.num_programs(0)
    slot = i % 2
    nrow = TOP_K * COMB_TT

    def issue(pref, s):
        def body(r, c):
            for k in range(TOP_K):
                p = pref[0, 0, TOP_K * r + k]
                pltpu.make_async_copy(ys_hbm.at[pl.ds(p, 1), :],
                                      buf.at[s, pl.ds(k * COMB_TT + r, 1), :], sem.at[s]).start()
            return c
        lax.fori_loop(0, COMB_TT, body, 0)

    @pl.when(i == 0)
    def _():
        issue(posc_ref, 0)

    pltpu.make_async_copy(ys_hbm.at[pl.ds(0, nrow), :], buf.at[slot], sem.at[slot]).wait()

    @pl.when(i + 1 < n)
    def _():
        issue(posn_ref, 1 - slot)

    y = x2_ref[...]
    for k in range(TOP_K):
        y = y + buf[slot, k * COMB_TT:(k + 1) * COMB_TT, :]

    @pl.when(i < n_prompt_tiles)
    def _():
        outp_ref[...] = y

    @pl.when(i >= n_prompt_tiles)
    def _():
        outs_ref[...] = y


def _combine(pos3, x2, ys, n_prompt):
    n_tok = x2.shape[0]
    nt = n_tok // COMB_TT
    npt = n_prompt // COMB_TT
    assert n_tok - n_prompt == COMB_TT
    body = functools.partial(_combine_body, n_prompt_tiles=npt)
    return pl.pallas_call(
        body,
        grid=(nt,),
        in_specs=[
            pl.BlockSpec((1, 1, TOP_K * COMB_TT), lambda i: (i, 0, 0), memory_space=pltpu.SMEM),
            pl.BlockSpec((1, 1, TOP_K * COMB_TT), lambda i: (jnp.minimum(i + 1, nt - 1), 0, 0),
                         memory_space=pltpu.SMEM),
            pl.BlockSpec((COMB_TT, D_MODEL), lambda i: (i, 0)),
            pl.BlockSpec(memory_space=pl.ANY),
        ],
        out_specs=[
            pl.BlockSpec((COMB_TT, D_MODEL), lambda i: (jnp.minimum(i, npt - 1), 0)),
            pl.BlockSpec((COMB_TT, D_MODEL), lambda i: (0, 0)),
        ],
        out_shape=[
            jax.ShapeDtypeStruct((n_prompt, D_MODEL), F32),
            jax.ShapeDtypeStruct((COMB_TT, D_MODEL), F32),
        ],
        scratch_shapes=[
            pltpu.VMEM((2, TOP_K * COMB_TT, D_MODEL), F32),
            pltpu.SemaphoreType.DMA((2,)),
        ],
        compiler_params=_cparams(("arbitrary",), 40),
        name="combine",
    )(pos3, pos3, x2, ys)


def _t5_bucket(rel):
    nb = T5_BUCKETS // 2
    max_exact = nb // 2
    ret = (rel > 0).astype(jnp.int32) * nb
    n = jnp.abs(rel)
    large = max_exact + (jnp.log(jnp.maximum(n, 1).astype(F32) / max_exact)
                         / math.log(T5_MAX_DIST / max_exact) * (nb - max_exact)).astype(jnp.int32)
    large = jnp.minimum(large, nb - 1)
    return ret + jnp.where(n < max_exact, n, large)


def _band_rel(pad):
    return jnp.arange(pad + CHUNK)[None, :] - pad - jnp.arange(CHUNK)[:, None]


def _dup_heads(t):
    lead = t.shape[:-1]
    t = t.reshape(*lead, A_KV_HEADS, 1, HEAD_DIM)
    return jnp.broadcast_to(t, (*lead, A_KV_HEADS, 2, HEAD_DIM)).reshape(*lead, A_KVD)


def _route(logits, n_tok):
    nk = n_tok * TOP_K
    nblk = nk // MOE_TM + N_EXPERTS
    rows = nblk * MOE_TM
    top_v, top_i = lax.top_k(logits, TOP_K)
    gates = jax.nn.softmax(top_v, axis=-1).reshape(-1)
    flat_e = top_i.reshape(-1).astype(jnp.int32)
    order = jnp.argsort(flat_e, stable=True).astype(jnp.int32)
    se = flat_e[order]
    counts = jnp.sum((flat_e[:, None] == jnp.arange(N_EXPERTS, dtype=jnp.int32)[None, :]).astype(jnp.int32),
                     axis=0)
    padded = (counts + MOE_TM - 1) // MOE_TM * MOE_TM
    pad_end = jnp.cumsum(padded)
    pad_start = pad_end - padded
    start = jnp.cumsum(counts) - counts
    nvalid = (pad_end[-1] // MOE_TM).astype(jnp.int32)
    blk = jnp.arange(nblk, dtype=jnp.int32)
    blk_exp = jnp.minimum(jnp.searchsorted(pad_end, blk * MOE_TM, side='right'), N_EXPERTS - 1).astype(jnp.int32)
    blk_exp = jnp.where(blk < nvalid, blk_exp, blk_exp[jnp.maximum(nvalid - 1, 0)])
    dest = pad_start[se] + jnp.arange(nk, dtype=jnp.int32) - start[se]
    row_tok = jnp.zeros((rows,), jnp.int32).at[dest].set(order // TOP_K, unique_indices=True)
    row_gate = jnp.zeros((rows,), F32).at[dest].set(gates[order], unique_indices=True)
    pos = jnp.zeros((nk,), jnp.int32).at[order].set(dest, unique_indices=True)
    return (row_tok.reshape(nblk, 1, MOE_TM), row_gate.reshape(rows, 1), blk_exp,
            nvalid.reshape(1), pos.reshape(n_tok // COMB_TT, 1, TOP_K * COMB_TT))


def kernel(x_prompt, x_sample, cache_a_k, cache_a_v, cache_b_k, cache_b_v, attn_norm, w_in,
           a_q_norm, a_k_norm, b_q_norm, b_k_norm, a_sinks, t5_table, b_rel_table, w_oa, w_ob,
           w_out, ffn_norm, router_w, router_b, w1, b1, w2, b2):
    batch, seq, _ = x_prompt.shape
    dec_b, dec_s, _ = x_sample.shape
    assert attn_norm.shape[0] == 1, "single layer"
    assert dec_s <= CHUNK and PAST_LEN % CHUNK == 0 and PAST_LEN >= PB
    n_prompt = batch * seq
    n_sample = dec_b * dec_s

    wi = w_in[0]
    c = [0]
    for wdt in (A_QW, A_KVW, A_KVW, B_W, B_W, B_W, D_MODEL, D_MODEL):
        c.append(c[-1] + wdt)
    w_qa, w_ka, w_va, w_qb, w_kb, w_vb, w_ga, w_gb = [wi[:, c[k]:c[k + 1]] for k in range(8)]
    w_perm = jnp.concatenate([w_qa, w_qb, w_kb, w_vb, w_ga, w_gb, _dup_heads(w_ka), _dup_heads(w_va)],
                             axis=1).astype(BF16)
    scale = HEAD_DIM ** -0.5
    ones = lambda n: jnp.ones((n,), F32)
    cg = jnp.concatenate([
        jnp.tile(a_q_norm[0] * scale, A_HEADS), jnp.tile(b_q_norm[0] * scale, B_HEADS),
        jnp.tile(b_k_norm[0], B_HEADS), ones(B_W), ones(2 * D_MODEL),
        jnp.tile(a_k_norm[0], 2 * A_KV_HEADS), ones(A_KVD)]).reshape(1, PROJ_W).astype(F32)
    hd = jnp.arange(MXU_COLS) // HEAD_DIM
    gm = (hd[:, None] == hd[None, :]).astype(BF16)
    g_attn = attn_norm[0].reshape(1, D_MODEL)
    bias_a = jnp.transpose(t5_table[_t5_bucket(_band_rel(PA))], (2, 0, 1)).astype(F32)
    idx_b = jnp.clip(_band_rel(PB), -B_REL_CLIP, CHUNK - 1) + B_REL_CLIP
    bias_b = jnp.transpose(b_rel_table[0][idx_b], (2, 0, 1)).astype(F32)
    sinks = a_sinks[0].astype(F32)
    woa, wob, wout = w_oa[0].astype(BF16), w_ob[0].astype(BF16), w_out[0].astype(BF16)
    fg = ffn_norm[0].reshape(1, D_MODEL)
    wr = jnp.pad(router_w[0], ((0, 0), (0, LANES - N_EXPERTS))).astype(BF16)
    br = jnp.pad(router_b[0], (0, LANES - N_EXPERTS)).reshape(1, LANES)
    w1b, w2b = w1[0].astype(BF16), w2[0].astype(BF16)
    b1r = b1[0].reshape(N_EXPERTS, 1, 2 * D_FF)
    b2r = b2[0].reshape(N_EXPERTS, 1, D_MODEL)

    xp2 = x_prompt.reshape(n_prompt, D_MODEL)
    h_p, kv_p = _proj(xp2, g_attn, w_perm, cg, gm, tm=1024)
    h_p3 = h_p.reshape(batch, seq, PROJ_W)
    cols_p = dict(qa=0, qb=1, kbp=2, kbc=2, vbp=3, vbc=3, kap=16, kac=16, vap=17, vac=17)
    oa_p, ob_p = _attn(sinks, h_p3, h_p3, h_p3, h_p3, h_p3, h_p3, h_p3, h_p3, h_p3, bias_a, bias_b,
                       cpb=8, first_pos=0, hi_a=LA, hi_b=LB, cols=cols_p)
    x2_p, xn_p, lg_p = _merge(oa_p.reshape(n_prompt, A_QW), ob_p.reshape(n_prompt, B_W), h_p, xp2,
                              woa, wob, wout, fg, wr, br)

    xs_pad = jnp.pad(x_sample, ((0, 0), (0, CHUNK - dec_s), (0, 0))).reshape(dec_b * CHUNK, D_MODEL)
    h_s, kv_s = _proj(xs_pad, g_attn, w_perm, cg, gm, tm=dec_b * CHUNK)
    h_s3 = h_s.reshape(dec_b, CHUNK, PROJ_W)
    ckb = cache_b_k[0].reshape(dec_b, PB, B_W).astype(BF16)
    cvb = cache_b_v[0].reshape(dec_b, PB, B_W).astype(BF16)
    cka = _dup_heads(cache_a_k[0].reshape(dec_b, PA, A_KVW)).astype(BF16)
    cva = _dup_heads(cache_a_v[0].reshape(dec_b, PA, A_KVW)).astype(BF16)
    cols_s = dict(qa=0, qb=1, kbp=0, kbc=2, vbp=0, vbc=3, kap=0, kac=16, vap=0, vac=17)
    oa_s, ob_s = _attn(sinks, h_s3, ckb, h_s3, cvb, h_s3, cka, h_s3, cva, h_s3, bias_a, bias_b,
                       cpb=1, first_pos=PAST_LEN, hi_a=PA + dec_s, hi_b=PB + dec_s, cols=cols_s)
    keep_rows = lambda t: t[:, :dec_s].reshape(n_sample, t.shape[-1])
    x2_s, xn_s, lg_s = _merge(keep_rows(oa_s), keep_rows(ob_s), keep_rows(h_s3),
                              x_sample.reshape(n_sample, D_MODEL), woa, wob, wout, fg, wr, br)

    n_tok = n_prompt + n_sample
    x2 = jnp.concatenate([x2_p, x2_s], axis=0)
    xn = jnp.concatenate([xn_p, xn_s], axis=0)
    logits = jnp.concatenate([lg_p, lg_s], axis=0)[:, :N_EXPERTS]
    row_tok3, row_gate, blk_exp, nvalid, pos3 = _route(logits, n_tok)
    ys = _experts(blk_exp, nvalid, row_tok3, row_gate, xn, w1b, b1r, w2b, b2r)
    y_p, y_s = _combine(pos3, x2, ys, n_prompt)

    kv_p3 = kv_p.reshape(batch, seq, KV32_W)
    kv_s3 = kv_s.reshape(dec_b, CHUNK, KV32_W)
    undup = lambda t: t.reshape(*t.shape[:-1], A_KV_HEADS, 2, HEAD_DIM)[..., 0, :]
    heads_b = lambda t: t.reshape(*t.shape[:-1], B_HEADS, HEAD_DIM)
    o_kb, o_vb, o_ka, o_va = 0, B_W, 2 * B_W, 2 * B_W + A_KVD
    p_bk = heads_b(kv_p3[:, seq - PB:, o_kb:o_kb + B_W])[None]
    p_bv = heads_b(kv_p3[:, seq - PB:, o_vb:o_vb + B_W])[None]
    p_ak = undup(kv_p3[:, seq - PA:, o_ka:o_ka + A_KVD])[None]
    p_av = undup(kv_p3[:, seq - PA:, o_va:o_va + A_KVD])[None]
    s_bk = heads_b(kv_s3[:, :dec_s, o_kb:o_kb + B_W])[None]
    s_bv = heads_b(kv_s3[:, :dec_s, o_vb:o_vb + B_W])[None]
    s_ak = undup(kv_s3[:, :dec_s, o_ka:o_ka + A_KVD])[None]
    s_av = undup(kv_s3[:, :dec_s, o_va:o_va + A_KVD])[None]
    return (y_p.reshape(batch, seq, D_MODEL), y_s.reshape(dec_b, dec_s, D_MODEL),
            p_ak, p_av, p_bk, p_bv, s_ak, s_av, s_bk, s_bv)
```

```python
import functools
import math

import jax
import jax.numpy as jnp
from jax import lax
from jax.experimental import pallas as pl
from jax.experimental.pallas import tpu as pltpu

F32 = jnp.float32
BF16 = jnp.bfloat16

D_MODEL = 2048
CHUNK = 64
HEAD_DIM = 64
A_HEADS = 16
A_KV_HEADS = 4
A_PAST_CHUNKS = 2
B_HEADS = 16
B_PAST_CHUNKS = 8
B_REL_CLIP = 256
T5_BUCKETS = 32
T5_MAX_DIST = (A_PAST_CHUNKS + 1) * CHUNK
N_EXPERTS = 32
TOP_K = 4
D_FF = D_MODEL
SWIGLU_ALPHA = 1.702
SWIGLU_LIMIT = 7.0
NORM_EPS = 1e-6
NEG_INF = -1e30
PAST_LEN = 1024

A_QW = A_HEADS * HEAD_DIM
A_KVW = A_KV_HEADS * HEAD_DIM
B_W = B_HEADS * HEAD_DIM
PA = A_PAST_CHUNKS * CHUNK
PB = B_PAST_CHUNKS * CHUNK
LA = PA + CHUNK
LB = PB + CHUNK

LANES = 128
MXU_COLS = 256

A_KVD = 2 * A_KVW
PROJ_W = A_QW + 3 * B_W + 2 * D_MODEL + 2 * A_KVD
PROJ_TN = 512
KV32_W = 2 * B_W + 2 * A_KVD
_NORM_TILES = (0, 1, 2, 3, 4, 5, 16)
_SIGMOID_LO, _SIGMOID_HI = 8, 16

MOE_TM = 512
MOE_TF = 512
MOE_NF = D_FF // MOE_TF
MOE_SHARE = MOE_TM // MOE_NF
COMB_TT = 256
ROUTER_TM = 256
MERGE_TM = 256
ATTN_AHEAD = 4


def _cparams(sem, vmem_mb):
    return pltpu.CompilerParams(dimension_semantics=sem, vmem_limit_bytes=vmem_mb * 1024 * 1024)


def _proj_body(x_ref, g_ref, w_ref, cg_ref, gm_ref, h_ref, kv_ref, xn_s):
    j = pl.program_id(1)

    @pl.when(j == 0)
    def _():
        x = x_ref[...]
        ms = jnp.mean(x * x, axis=-1, keepdims=True)
        xn_s[...] = (x * lax.rsqrt(ms + NORM_EPS) * g_ref[...]).astype(BF16)

    acc = jnp.dot(xn_s[...], w_ref[...], preferred_element_type=F32)
    is_norm = functools.reduce(jnp.logical_or, [j == t for t in _NORM_TILES])
    is_sig = jnp.logical_and(j >= _SIGMOID_LO, j < _SIGMOID_HI)
    is_kv = jnp.logical_or(jnp.logical_and(j >= 4, j < 8), j >= 16)

    @pl.when(is_norm)
    def _():
        parts = []
        for c in range(PROJ_TN // MXU_COLS):
            a = acc[:, c * MXU_COLS:(c + 1) * MXU_COLS]
            ss = jnp.dot((a * a).astype(BF16), gm_ref[...], preferred_element_type=F32)
            parts.append(a * lax.rsqrt(ss * (1.0 / HEAD_DIM) + NORM_EPS))
        y = jnp.concatenate(parts, axis=-1) * cg_ref[...]
        h_ref[...] = y.astype(BF16)

        @pl.when(is_kv)
        def _():
            kv_ref[...] = y

    @pl.when(is_sig)
    def _():
        h_ref[...] = jax.nn.sigmoid(acc).astype(BF16)

    @pl.when(jnp.logical_not(jnp.logical_or(is_norm, is_sig)))
    def _():
        h_ref[...] = acc.astype(BF16)

        @pl.when(is_kv)
        def _():
            kv_ref[...] = acc


def _kv_tile(j):
    return jnp.clip(j - 4, 0, 3) + (j >= 16).astype(jnp.int32) + (j >= 17).astype(jnp.int32)


def _proj(x2d, g, w_perm, cg, gm, tm):
    n = x2d.shape[0]
    grid = (n // tm, PROJ_W // PROJ_TN)
    return pl.pallas_call(
        _proj_body,
        grid=grid,
        in_specs=[
            pl.BlockSpec((tm, D_MODEL), lambda i, j: (i, 0)),
            pl.BlockSpec((1, D_MODEL), lambda i, j: (0, 0)),
            pl.BlockSpec((D_MODEL, PROJ_TN), lambda i, j: (0, j)),
            pl.BlockSpec((1, PROJ_TN), lambda i, j: (0, j)),
            pl.BlockSpec((MXU_COLS, MXU_COLS), lambda i, j: (0, 0)),
        ],
        out_specs=[
            pl.BlockSpec((tm, PROJ_TN), lambda i, j: (i, j)),
            pl.BlockSpec((tm, PROJ_TN), lambda i, j: (i, _kv_tile(j))),
        ],
        out_shape=[
            jax.ShapeDtypeStruct((n, PROJ_W), BF16),
            jax.ShapeDtypeStruct((n, KV32_W), F32),
        ],
        scratch_shapes=[pltpu.VMEM((tm, D_MODEL), BF16)],
        compiler_params=_cparams(("parallel", "arbitrary"), 48),
        name="proj",
    )(x2d, g, w_perm, cg, gm)


def _attn_body(sink_ref, qa_ref, qb_ref, kbp_ref, kbc_ref, vbp_ref, vbc_ref,
               kap_ref, kac_ref, vap_ref, vac_ref, ba_ref, bb_ref,
               oa_ref, ob_ref, kb_s, vb_s, ka_s, va_s, *, cpb, first_pos, hi_a, hi_b):
    i = pl.program_id(1)
    qb_rows = cpb * CHUNK
    kb_s[0:PB, :] = kbp_ref[0]
    kb_s[PB:PB + qb_rows, :] = kbc_ref[0]
    vb_s[0:PB, :] = vbp_ref[0]
    vb_s[PB:PB + qb_rows, :] = vbc_ref[0]
    ka_s[0:PA, :] = kap_ref[0]
    ka_s[PA:PA + qb_rows, :] = kac_ref[0]
    va_s[0:PA, :] = vap_ref[0]
    va_s[PA:PA + qb_rows, :] = vac_ref[0]

    lane_a = lax.broadcasted_iota(jnp.int32, (1, LA), 1)
    lane_b = lax.broadcasted_iota(jnp.int32, (1, LB), 1)
    low_half = lax.broadcasted_iota(jnp.int32, (1, LANES), 1) < HEAD_DIM
    nt = (((1,), (1,)), ((), ()))

    tasks = [("a", h) for h in range(A_HEADS)] + [("b", h) for h in range(B_HEADS)]

    def scores(task, r0, valid_a, valid_b):
        mixer, h = task
        p, half = h // 2, h % 2
        cols = slice(p * LANES, (p + 1) * LANES)
        if mixer == "a":
            kcols = slice((p // 2) * LANES, (p // 2 + 1) * LANES)
            q_p, k_p, bias, valid = qa_ref[0, pl.ds(r0, CHUNK), cols], ka_s[pl.ds(r0, LA), kcols], ba_ref[h], valid_a
        else:
            q_p, k_p, bias, valid = qb_ref[0, pl.ds(r0, CHUNK), cols], kb_s[pl.ds(r0, LB), cols], bb_ref[h], valid_b
        qm = jnp.where(low_half if half == 0 else jnp.logical_not(low_half), q_p, jnp.zeros_like(q_p))
        s = lax.dot_general(qm, k_p, nt, preferred_element_type=F32) + bias
        if valid is not None:
            s = jnp.where(valid, s, NEG_INF)
        return s

    def attend(task, s, r0):
        mixer, h = task
        p = h // 2
        if mixer == "a":
            kcols = slice((p // 2) * LANES, (p // 2 + 1) * LANES)
            v_p, sink = va_s[pl.ds(r0, LA), kcols], sink_ref[h]
        else:
            v_p, sink = vb_s[pl.ds(r0, LB), slice(p * LANES, (p + 1) * LANES)], None
        m = jnp.max(s, axis=-1, keepdims=True)
        if sink is not None:
            m = jnp.maximum(m, sink)
        e = jnp.exp(s - m)
        l = jnp.sum(e, axis=-1, keepdims=True)
        if sink is not None:
            l = l + jnp.exp(sink - m)
        o = jnp.dot(e.astype(BF16), v_p, preferred_element_type=F32)
        return o / l

    def make_chunk(masked):
        def chunk(jc, carry):
            r0 = pl.multiple_of(jc * CHUNK, CHUNK)
            valid_a = valid_b = None
            if masked:
                start = first_pos + (i * cpb + jc) * CHUNK
                valid_a = jnp.logical_and(lane_a >= jnp.maximum(PA - start, 0), lane_a < hi_a)
                valid_b = jnp.logical_and(lane_b >= jnp.maximum(PB - start, 0), lane_b < hi_b)
            pending, outs = {}, {}
            for t in range(len(tasks) + ATTN_AHEAD):
                if t < len(tasks):
                    pending[t] = scores(tasks[t], r0, valid_a, valid_b)
                d = t - ATTN_AHEAD
                if d >= 0:
                    mixer, h = tasks[d]
                    outs[h % 2] = attend(tasks[d], pending.pop(d), r0)
                    if h % 2 == 1:
                        cols = slice((h // 2) * LANES, (h // 2 + 1) * LANES)
                        o_ref = oa_ref if mixer == "a" else ob_ref
                        o_ref[0, pl.ds(r0, CHUNK), cols] = jnp.where(low_half, outs[0], outs[1]).astype(BF16)
            return carry
        return chunk

    n_masked_blocks = pl.cdiv(max(PB - first_pos, 0), cpb * CHUNK)
    if hi_a < LA or hi_b < LB:
        lax.fori_loop(0, cpb, make_chunk(True), 0)
    else:
        @pl.when(i < n_masked_blocks)
        def _():
            lax.fori_loop(0, cpb, make_chunk(True), 0)

        @pl.when(i >= n_masked_blocks)
        def _():
            lax.fori_loop(0, cpb, make_chunk(False), 0)


def _attn(sinks, hq, kbp, kbc, vbp, vbc, kap, kac, vap, vac, bias_a, bias_b, *,
          cpb, first_pos, hi_a, hi_b, cols):
    nb, s, _ = hq.shape
    qb_rows = cpb * CHUNK
    nblk = s // qb_rows
    rb = qb_rows // PB if qb_rows >= PB else None
    ra = qb_rows // PA if qb_rows >= PA else None

    def prev_map(ratio, col):
        if ratio is None:
            return lambda b, i, *_: (b, 0, col)
        return lambda b, i, *_: (b, jnp.maximum(i * ratio - 1, 0), col)

    def cur_map(col):
        return lambda b, i, *_: (b, i, col)

    in_specs = [
        pl.BlockSpec((1, qb_rows, A_QW), cur_map(cols["qa"])),
        pl.BlockSpec((1, qb_rows, B_W), cur_map(cols["qb"])),
        pl.BlockSpec((1, PB, B_W), prev_map(rb, cols["kbp"])),
        pl.BlockSpec((1, qb_rows, B_W), cur_map(cols["kbc"])),
        pl.BlockSpec((1, PB, B_W), prev_map(rb, cols["vbp"])),
        pl.BlockSpec((1, qb_rows, B_W), cur_map(cols["vbc"])),
        pl.BlockSpec((1, PA, A_KVD), prev_map(ra, cols["kap"])),
        pl.BlockSpec((1, qb_rows, A_KVD), cur_map(cols["kac"])),
        pl.BlockSpec((1, PA, A_KVD), prev_map(ra, cols["vap"])),
        pl.BlockSpec((1, qb_rows, A_KVD), cur_map(cols["vac"])),
        pl.BlockSpec((A_HEADS, CHUNK, LA), lambda b, i, *_: (0, 0, 0)),
        pl.BlockSpec((B_HEADS, CHUNK, LB), lambda b, i, *_: (0, 0, 0)),
    ]
    out_specs = [
        pl.BlockSpec((1, qb_rows, A_QW), lambda b, i, *_: (b, i, 0)),
        pl.BlockSpec((1, qb_rows, B_W), lambda b, i, *_: (b, i, 0)),
    ]
    body = functools.partial(_attn_body, cpb=cpb, first_pos=first_pos, hi_a=hi_a, hi_b=hi_b)
    return pl.pallas_call(
        body,
        grid_spec=pltpu.PrefetchScalarGridSpec(
            num_scalar_prefetch=1,
            grid=(nb, nblk),
            in_specs=in_specs,
            out_specs=out_specs,
            scratch_shapes=[
                pltpu.VMEM((PB + qb_rows, B_W), BF16),
                pltpu.VMEM((PB + qb_rows, B_W), BF16),
                pltpu.VMEM((PA + qb_rows, A_KVD), BF16),
                pltpu.VMEM((PA + qb_rows, A_KVD), BF16),
            ],
        ),
        out_shape=[
            jax.ShapeDtypeStruct((nb, s, A_QW), BF16),
            jax.ShapeDtypeStruct((nb, s, B_W), BF16),
        ],
        compiler_params=_cparams(("parallel", "arbitrary"), 48),
        name="attn",
    )(sinks, hq, hq, kbp, kbc, vbp, vbc, kap, kac, vap, vac, bias_a, bias_b)


def _merge_body(oa_p, ob_p, ga_p, gb_p, x_p, oa_s, ob_s, ga_s, gb_s, x_s,
                woa_ref, wob_ref, wout_ref, fg_ref, wr_ref, br_ref, x2_ref, xn_ref, lg_ref, *,
                n_prompt_tiles):
    def tile(oa_ref, ob_ref, ga_ref, gb_ref, x_ref):
        ya = jnp.dot(oa_ref[...], woa_ref[...], preferred_element_type=F32)
        yb = jnp.dot(ob_ref[...], wob_ref[...], preferred_element_type=F32)
        z = ga_ref[...].astype(F32) * ya + gb_ref[...].astype(F32) * yb
        y = jnp.dot(z.astype(BF16), wout_ref[...], preferred_element_type=F32)
        x2 = x_ref[...] + y
        x2_ref[...] = x2
        ms = jnp.mean(x2 * x2, axis=-1, keepdims=True)
        xn = x2 * lax.rsqrt(ms + NORM_EPS) * fg_ref[...]
        xn_ref[...] = xn
        lg_ref[...] = jnp.dot(xn.astype(BF16), wr_ref[...], preferred_element_type=F32) + br_ref[...]

    i = pl.program_id(0)

    @pl.when(i < n_prompt_tiles)
    def _():
        tile(oa_p, ob_p, ga_p, gb_p, x_p)

    @pl.when(i >= n_prompt_tiles)
    def _():
        tile(oa_s, ob_s, ga_s, gb_s, x_s)


def _merge(prompt, sample, w_oa, w_ob, w_out, fg, wr, br):
    tm = MERGE_TM
    n_p, n_s = prompt[3].shape[0], sample[3].shape[0]
    npt, nst = n_p // tm, n_s // tm
    n_out = n_p + n_s
    const = lambda i: (0, 0)
    resident = functools.partial(pl.BlockSpec, index_map=const, pipeline_mode=pl.Buffered(1))

    def token_specs(row):
        return [
            pl.BlockSpec((tm, A_QW), lambda i: (row(i), 0)),
            pl.BlockSpec((tm, B_W), lambda i: (row(i), 0)),
            pl.BlockSpec((tm, D_MODEL), lambda i: (row(i), 2)),
            pl.BlockSpec((tm, D_MODEL), lambda i: (row(i), 3)),
            pl.BlockSpec((tm, D_MODEL), lambda i: (row(i), 0)),
        ]

    in_specs = (token_specs(lambda i: jnp.minimum(i, npt - 1))
                + token_specs(lambda i: jnp.clip(i - npt, 0, nst - 1))
                + [resident((A_QW, D_MODEL)), resident((B_W, D_MODEL)), resident((D_MODEL, D_MODEL)),
                   resident((1, D_MODEL)), resident((D_MODEL, LANES)), resident((1, LANES))])
    oa_p, ob_p, h_p, x_p = prompt
    oa_s, ob_s, h_s, x_s = sample
    return pl.pallas_call(
        functools.partial(_merge_body, n_prompt_tiles=npt),
        grid=(npt + nst,),
        in_specs=in_specs,
        out_specs=[
            pl.BlockSpec((tm, D_MODEL), lambda i: (i, 0)),
            pl.BlockSpec((tm, D_MODEL), lambda i: (i, 0)),
            pl.BlockSpec((tm, LANES), lambda i: (i, 0)),
        ],
        out_shape=[
            jax.ShapeDtypeStruct((n_out, D_MODEL), F32),
            jax.ShapeDtypeStruct((n_out, D_MODEL), F32),
            jax.ShapeDtypeStruct((n_out, LANES), F32),
        ],
        compiler_params=_cparams(("arbitrary",), 56),
        name="merge",
    )(oa_p, ob_p, h_p, h_p, x_p, oa_s, ob_s, h_s, h_s, x_s, w_oa, w_ob, w_out, fg, wr, br)


def _router_body(lg_ref, tri_ref, idx_ref, gate_ref, rank_ref, cnt_ref, carry):
    i = pl.program_id(0)

    @pl.when(i == 0)
    def _():
        carry[...] = jnp.zeros_like(carry)

    lane = lax.broadcasted_iota(jnp.int32, lg_ref.shape, 1)
    x = jnp.where(lane < N_EXPERTS, lg_ref[...], -jnp.inf)
    vals, hots = [], []
    idx_out = jnp.zeros(lg_ref.shape, jnp.int32)
    for k in range(TOP_K):
        m = jnp.max(x, axis=-1, keepdims=True)
        am = jnp.min(jnp.where(x == m, lane, LANES), axis=-1, keepdims=True)
        hot = lane == am
        vals.append(m)
        hots.append(hot)
        idx_out = jnp.where(lane == k, am, idx_out)
        x = jnp.where(hot, -jnp.inf, x)
    es = [jnp.exp(v - vals[0]) for v in vals]
    denom = functools.reduce(lambda a, b: a + b, es)
    sel = functools.reduce(jnp.logical_or, hots)
    sel_f = jnp.where(sel, 1.0, 0.0)
    before = jnp.dot(tri_ref[...], sel_f.astype(BF16), preferred_element_type=F32) + carry[...]
    gate_out = jnp.zeros(lg_ref.shape, F32)
    rank_out = jnp.zeros(lg_ref.shape, F32)
    for k in range(TOP_K):
        gate_out = jnp.where(lane == k, es[k] / denom, gate_out)
        rk = jnp.sum(jnp.where(hots[k], before, 0.0), axis=-1, keepdims=True)
        rank_out = jnp.where(lane == k, rk, rank_out)
    idx_ref[...] = idx_out
    gate_ref[...] = gate_out
    rank_ref[...] = rank_out.astype(jnp.int32)
    total = carry[...] + jnp.sum(sel_f, axis=0, keepdims=True)
    carry[...] = total
    cnt_ref[...] = total.astype(jnp.int32)


def _router(logits):
    n = logits.shape[0]
    tm = ROUTER_TM
    r = jnp.arange(tm)
    tri = (r[None, :] < r[:, None]).astype(BF16)
    tile = pl.BlockSpec((tm, LANES), lambda i: (i, 0))
    return pl.pallas_call(
        _router_body,
        grid=(n // tm,),
        in_specs=[tile, pl.BlockSpec((tm, tm), lambda i: (0, 0))],
        out_specs=[tile, tile, tile, pl.BlockSpec((1, LANES), lambda i: (0, 0))],
        out_shape=[
            jax.ShapeDtypeStruct((n, LANES), jnp.int32),
            jax.ShapeDtypeStruct((n, LANES), F32),
            jax.ShapeDtypeStruct((n, LANES), jnp.int32),
            jax.ShapeDtypeStruct((1, LANES), jnp.int32),
        ],
        scratch_shapes=[pltpu.VMEM((1, LANES), F32)],
        compiler_params=_cparams(("arbitrary",), 16),
        name="router",
    )(logits, tri)


def _row_gather(idx_ref, n, src_hbm, dst, sem):
    def body(r, c):
        t = idx_ref[0, 0, r]
        pltpu.make_async_copy(src_hbm.at[pl.ds(t, 1), :], dst.at[pl.ds(r, 1), :], sem).start()
        return c
    lax.fori_loop(0, n, body, 0)


def _expert_body(bexp_ref, nv_ref, tokc_ref, tokn_ref, xn_hbm,
                 w1g_ref, w1u_ref, b1g_ref, b1u_ref, w2_ref, b2_ref,
                 out_ref, xg, xb, act_s, sem):
    i = pl.program_id(0)
    f = pl.program_id(1)
    nf = MOE_NF
    nv = nv_ref[0]
    slot = i % 2

    def rows_landed(s):
        return pltpu.make_async_copy(xn_hbm.at[pl.ds(0, MOE_TM), :], xg.at[s], sem.at[s])

    @pl.when(jnp.logical_and(i >= nv, f == 0))
    def _():
        out_ref[...] = jnp.zeros_like(out_ref)

        @pl.when(i == nv)
        def _():
            rows_landed(slot).wait()

    @pl.when(jnp.logical_and(i < nv, f == 0))
    def _():
        @pl.when(i == 0)
        def _():
            _row_gather(tokc_ref, MOE_TM, xn_hbm, xg.at[0], sem.at[0])

        rows_landed(slot).wait()
        xb[...] = xg[slot].astype(BF16)

    @pl.when(i < nv)
    def _():
        base = f * MOE_SHARE
        for r in range(MOE_SHARE):
            t = tokn_ref[0, 0, base + r]
            pltpu.make_async_copy(xn_hbm.at[pl.ds(t, 1), :], xg.at[1 - slot, pl.ds(base + r, 1), :],
                                  sem.at[1 - slot]).start()
        x = xb[...]
        hg = jnp.dot(x, w1g_ref[0], preferred_element_type=F32) + b1g_ref[0]
        hu = jnp.dot(x, w1u_ref[0], preferred_element_type=F32) + b1u_ref[0]
        hg = jnp.minimum(hg, SWIGLU_LIMIT)
        hu = jnp.clip(hu, -SWIGLU_LIMIT, SWIGLU_LIMIT)
        act_s[f] = (hg * jax.nn.sigmoid(SWIGLU_ALPHA * hg) * (hu + 1.0)).astype(BF16)

    @pl.when(jnp.logical_and(i < nv, f == nf - 1))
    def _():
        act = jnp.concatenate([act_s[k] for k in range(nf)], axis=-1)
        out_ref[...] = jnp.dot(act, w2_ref[0], preferred_element_type=F32) + b2_ref[0]


def _experts(blk_exp, nvalid, row_tok3, xn, w1, b1, w2, b2):
    nblk = row_tok3.shape[0]
    nf = MOE_NF
    rows = nblk * MOE_TM

    def fz(i, f, nv):
        return jnp.where(i < nv[0], f, nf - 1)

    in_specs = [
        pl.BlockSpec((1, 1, MOE_TM), lambda i, f, be, nv: (i, 0, 0), memory_space=pltpu.SMEM),
        pl.BlockSpec((1, 1, MOE_TM), lambda i, f, be, nv: (jnp.minimum(i + 1, nblk - 1), 0, 0),
                     memory_space=pltpu.SMEM),
        pl.BlockSpec(memory_space=pl.ANY),
        pl.BlockSpec((1, D_MODEL, MOE_TF), lambda i, f, be, nv: (be[i], 0, fz(i, f, nv))),
        pl.BlockSpec((1, D_MODEL, MOE_TF), lambda i, f, be, nv: (be[i], 0, nf + fz(i, f, nv))),
        pl.BlockSpec((1, 1, MOE_TF), lambda i, f, be, nv: (be[i], 0, fz(i, f, nv))),
        pl.BlockSpec((1, 1, MOE_TF), lambda i, f, be, nv: (be[i], 0, nf + fz(i, f, nv))),
        pl.BlockSpec((1, D_FF, D_MODEL), lambda i, f, be, nv: (be[i], 0, 0)),
        pl.BlockSpec((1, 1, D_MODEL), lambda i, f, be, nv: (be[i], 0, 0)),
    ]
    return pl.pallas_call(
        _expert_body,
        grid_spec=pltpu.PrefetchScalarGridSpec(
            num_scalar_prefetch=2,
            grid=(nblk, nf),
            in_specs=in_specs,
            out_specs=pl.BlockSpec((MOE_TM, D_MODEL), lambda i, f, be, nv: (i, 0)),
            scratch_shapes=[
                pltpu.VMEM((2, MOE_TM, D_MODEL), F32),
                pltpu.VMEM((MOE_TM, D_MODEL), BF16),
                pltpu.VMEM((nf, MOE_TM, MOE_TF), BF16),
                pltpu.SemaphoreType.DMA((2,)),
            ],
        ),
        out_shape=jax.ShapeDtypeStruct((rows, D_MODEL), F32),
        compiler_params=_cparams(("arbitrary", "arbitrary"), 56),
        name="experts",
    )(blk_exp, nvalid, row_tok3, row_tok3, xn, w1, w1, b1, b1, w2, b2)


def _combine_body(posc_ref, posn_ref, x2_ref, gate_ref, ys_hbm, outp_ref, outs_ref, buf, sem, *,
                  n_prompt_tiles):
    i = pl.program_id(0)
    n = pl.num_programs(0)
    slot = i % 2
    nrow = TOP_K * COMB_TT

    def issue(pref, s):
        def body(r, c):
            for k in range(TOP_K):
                p = pref[0, 0, TOP_K * r + k]
                pltpu.make_async_copy(ys_hbm.at[pl.ds(p, 1), :],
                                      buf.at[s, pl.ds(k * COMB_TT + r, 1), :], sem.at[s]).start()
            return c
        lax.fori_loop(0, COMB_TT, body, 0)

    @pl.when(i == 0)
    def _():
        issue(posc_ref, 0)

    pltpu.make_async_copy(ys_hbm.at[pl.ds(0, nrow), :], buf.at[slot], sem.at[slot]).wait()

    @pl.when(i + 1 < n)
    def _():
        issue(posn_ref, 1 - slot)

    y = x2_ref[...]
    g = gate_ref[...]
    for k in range(TOP_K):
        y = y + g[:, k:k + 1] * buf[slot, k * COMB_TT:(k + 1) * COMB_TT, :]

    @pl.when(i < n_prompt_tiles)
    def _():
        outp_ref[...] = y

    @pl.when(i >= n_prompt_tiles)
    def _():
        outs_ref[...] = y


def _combine(pos3, x2, gates, ys, n_prompt):
    n_tok = x2.shape[0]
    nt = n_tok // COMB_TT
    npt = n_prompt // COMB_TT
    assert n_tok - n_prompt == COMB_TT
    body = functools.partial(_combine_body, n_prompt_tiles=npt)
    return pl.pallas_call(
        body,
        grid=(nt,),
        in_specs=[
            pl.BlockSpec((1, 1, TOP_K * COMB_TT), lambda i: (i, 0, 0), memory_space=pltpu.SMEM),
            pl.BlockSpec((1, 1, TOP_K * COMB_TT), lambda i: (jnp.minimum(i + 1, nt - 1), 0, 0),
                         memory_space=pltpu.SMEM),
            pl.BlockSpec((COMB_TT, D_MODEL), lambda i: (i, 0)),
            pl.BlockSpec((COMB_TT, LANES), lambda i: (i, 0)),
            pl.BlockSpec(memory_space=pl.ANY),
        ],
        out_specs=[
            pl.BlockSpec((COMB_TT, D_MODEL), lambda i: (jnp.minimum(i, npt - 1), 0)),
            pl.BlockSpec((COMB_TT, D_MODEL), lambda i: (0, 0)),
        ],
        out_shape=[
            jax.ShapeDtypeStruct((n_prompt, D_MODEL), F32),
            jax.ShapeDtypeStruct((COMB_TT, D_MODEL), F32),
        ],
        scratch_shapes=[
            pltpu.VMEM((2, TOP_K * COMB_TT, D_MODEL), F32),
            pltpu.SemaphoreType.DMA((2,)),
        ],
        compiler_params=_cparams(("arbitrary",), 40),
        name="combine",
    )(pos3, pos3, x2, gates, ys)


def _t5_bucket(rel):
    nb = T5_BUCKETS // 2
    max_exact = nb // 2
    ret = (rel > 0).astype(jnp.int32) * nb
    n = jnp.abs(rel)
    large = max_exact + (jnp.log(jnp.maximum(n, 1).astype(F32) / max_exact)
                         / math.log(T5_MAX_DIST / max_exact) * (nb - max_exact)).astype(jnp.int32)
    large = jnp.minimum(large, nb - 1)
    return ret + jnp.where(n < max_exact, n, large)


def _band_rel(pad):
    return jnp.arange(pad + CHUNK)[None, :] - pad - jnp.arange(CHUNK)[:, None]


def _dup_heads(t):
    lead = t.shape[:-1]
    t = t.reshape(*lead, A_KV_HEADS, 1, HEAD_DIM)
    return jnp.broadcast_to(t, (*lead, A_KV_HEADS, 2, HEAD_DIM)).reshape(*lead, A_KVD)


def _layout(top_i, rank, counts, n_tok):
    nk = n_tok * TOP_K
    nblk = nk // MOE_TM + N_EXPERTS
    rows = nblk * MOE_TM
    padded = (counts + MOE_TM - 1) // MOE_TM * MOE_TM
    pad_end = jnp.cumsum(padded)
    pad_start = pad_end - padded
    start = jnp.cumsum(counts) - counts
    nvalid = (pad_end[-1] // MOE_TM).astype(jnp.int32)
    blk = jnp.arange(nblk, dtype=jnp.int32)
    blk_exp = jnp.sum((pad_end[None, :] <= (blk * MOE_TM)[:, None]).astype(jnp.int32), axis=1)
    last_used = jnp.sum((pad_end <= (nvalid - 1) * MOE_TM).astype(jnp.int32))
    blk_exp = jnp.minimum(jnp.where(blk < nvalid, blk_exp, last_used), N_EXPERTS - 1)
    pos = pad_start[top_i] + rank
    order = jnp.argsort(top_i.reshape(-1), stable=True).astype(jnp.int32)
    r = jnp.arange(rows, dtype=jnp.int32)
    e_r = blk_exp[r // MOE_TM]
    j = r - pad_start[e_r]
    src = order[jnp.clip(start[e_r] + j, 0, nk - 1)]
    row_tok = jnp.where(j < counts[e_r], src // TOP_K, 0)
    return (row_tok.reshape(nblk, 1, MOE_TM), blk_exp.astype(jnp.int32), nvalid.reshape(1),
            pos.reshape(n_tok // COMB_TT, 1, TOP_K * COMB_TT))


def kernel(x_prompt, x_sample, cache_a_k, cache_a_v, cache_b_k, cache_b_v, attn_norm, w_in,
           a_q_norm, a_k_norm, b_q_norm, b_k_norm, a_sinks, t5_table, b_rel_table, w_oa, w_ob,
           w_out, ffn_norm, router_w, router_b, w1, b1, w2, b2):
    batch, seq, _ = x_prompt.shape
    dec_b, dec_s, _ = x_sample.shape
    assert attn_norm.shape[0] == 1, "single layer"
    assert dec_s <= CHUNK and PAST_LEN % CHUNK == 0 and PAST_LEN >= PB
    n_prompt = batch * seq
    n_sample = dec_b * dec_s

    wi = w_in[0]
    c = [0]
    for wdt in (A_QW, A_KVW, A_KVW, B_W, B_W, B_W, D_MODEL, D_MODEL):
        c.append(c[-1] + wdt)
    w_qa, w_ka, w_va, w_qb, w_kb, w_vb, w_ga, w_gb = [wi[:, c[k]:c[k + 1]] for k in range(8)]
    w_perm = jnp.concatenate([w_qa, w_qb, w_kb, w_vb, w_ga, w_gb, _dup_heads(w_ka), _dup_heads(w_va)],
                             axis=1).astype(BF16)
    scale = HEAD_DIM ** -0.5
    ones = lambda n: jnp.ones((n,), F32)
    cg = jnp.concatenate([
        jnp.tile(a_q_norm[0] * scale, A_HEADS), jnp.tile(b_q_norm[0] * scale, B_HEADS),
        jnp.tile(b_k_norm[0], B_HEADS), ones(B_W), ones(2 * D_MODEL),
        jnp.tile(a_k_norm[0], 2 * A_KV_HEADS), ones(A_KVD)]).reshape(1, PROJ_W).astype(F32)
    hd = jnp.arange(MXU_COLS) // HEAD_DIM
    gm = (hd[:, None] == hd[None, :]).astype(BF16)
    g_attn = attn_norm[0].reshape(1, D_MODEL)
    bias_a = jnp.transpose(t5_table[_t5_bucket(_band_rel(PA))], (2, 0, 1)).astype(F32)
    idx_b = jnp.clip(_band_rel(PB), -B_REL_CLIP, CHUNK - 1) + B_REL_CLIP
    bias_b = jnp.transpose(b_rel_table[0][idx_b], (2, 0, 1)).astype(F32)
    sinks = a_sinks[0].astype(F32)
    woa, wob, wout = w_oa[0].astype(BF16), w_ob[0].astype(BF16), w_out[0].astype(BF16)
    fg = ffn_norm[0].reshape(1, D_MODEL)
    wr = jnp.pad(router_w[0], ((0, 0), (0, LANES - N_EXPERTS))).astype(BF16)
    br = jnp.pad(router_b[0], (0, LANES - N_EXPERTS)).reshape(1, LANES)
    w1b, w2b = w1[0].astype(BF16), w2[0].astype(BF16)
    b1r = b1[0].reshape(N_EXPERTS, 1, 2 * D_FF)
    b2r = b2[0].reshape(N_EXPERTS, 1, D_MODEL)

    xp2 = x_prompt.reshape(n_prompt, D_MODEL)
    h_p, kv_p = _proj(xp2, g_attn, w_perm, cg, gm, tm=1024)
    h_p3 = h_p.reshape(batch, seq, PROJ_W)
    cols_p = dict(qa=0, qb=1, kbp=2, kbc=2, vbp=3, vbc=3, kap=16, kac=16, vap=17, vac=17)
    oa_p, ob_p = _attn(sinks, h_p3, h_p3, h_p3, h_p3, h_p3, h_p3, h_p3, h_p3, h_p3, bias_a, bias_b,
                       cpb=8, first_pos=0, hi_a=LA, hi_b=LB, cols=cols_p)
    n_tok = n_prompt + n_sample

    xs_pad = jnp.pad(x_sample, ((0, 0), (0, CHUNK - dec_s), (0, 0))).reshape(dec_b * CHUNK, D_MODEL)
    h_s, kv_s = _proj(xs_pad, g_attn, w_perm, cg, gm, tm=dec_b * CHUNK)
    h_s3 = h_s.reshape(dec_b, CHUNK, PROJ_W)
    ckb = cache_b_k[0].reshape(dec_b, PB, B_W).astype(BF16)
    cvb = cache_b_v[0].reshape(dec_b, PB, B_W).astype(BF16)
    cka = _dup_heads(cache_a_k[0].reshape(dec_b, PA, A_KVW)).astype(BF16)
    cva = _dup_heads(cache_a_v[0].reshape(dec_b, PA, A_KVW)).astype(BF16)
    cols_s = dict(qa=0, qb=1, kbp=0, kbc=2, vbp=0, vbc=3, kap=0, kac=16, vap=0, vac=17)
    oa_s, ob_s = _attn(sinks, h_s3, ckb, h_s3, cvb, h_s3, cka, h_s3, cva, h_s3, bias_a, bias_b,
                       cpb=1, first_pos=PAST_LEN, hi_a=PA + dec_s, hi_b=PB + dec_s, cols=cols_s)
    keep_rows = lambda t: t[:, :dec_s].reshape(n_sample, t.shape[-1])
    x2, xn, logits = _merge(
        (oa_p.reshape(n_prompt, A_QW), ob_p.reshape(n_prompt, B_W), h_p, xp2),
        (keep_rows(oa_s), keep_rows(ob_s), keep_rows(h_s3), x_sample.reshape(n_sample, D_MODEL)),
        woa, wob, wout, fg, wr, br)

    top_i, gates, rank, counts = _router(logits)
    row_tok3, blk_exp, nvalid, pos3 = _layout(top_i[:, :TOP_K], rank[:, :TOP_K], counts[0, :N_EXPERTS], n_tok)
    ys = _experts(blk_exp, nvalid, row_tok3, xn, w1b, b1r, w2b, b2r)
    y_p, y_s = _combine(pos3, x2, gates, ys, n_prompt)

    kv_p3 = kv_p.reshape(batch, seq, KV32_W)
    kv_s3 = kv_s.reshape(dec_b, CHUNK, KV32_W)
    undup = lambda t: t.reshape(*t.shape[:-1], A_KV_HEADS, 2, HEAD_DIM)[..., 0, :]
    heads_b = lambda t: t.reshape(*t.shape[:-1], B_HEADS, HEAD_DIM)
    o_kb, o_vb, o_ka, o_va = 0, B_W, 2 * B_W, 2 * B_W + A_KVD
    p_bk = heads_b(kv_p3[:, seq - PB:, o_kb:o_kb + B_W])[None]
    p_bv = heads_b(kv_p3[:, seq - PB:, o_vb:o_vb + B_W])[None]
    p_ak = undup(kv_p3[:, seq - PA:, o_ka:o_ka + A_KVD])[None]
    p_av = undup(kv_p3[:, seq - PA:, o_va:o_va + A_KVD])[None]
    s_bk = heads_b(kv_s3[:, :dec_s, o_kb:o_kb + B_W])[None]
    s_bv = heads_b(kv_s3[:, :dec_s, o_vb:o_vb + B_W])[None]
    s_ak = undup(kv_s3[:, :dec_s, o_ka:o_ka + A_KVD])[None]
    s_av = undup(kv_s3[:, :dec_s, o_va:o_va + A_KVD])[None]
    return (y_p.reshape(batch, seq, D_MODEL), y_s.reshape(dec_b, dec_s, D_MODEL),
            p_ak, p_av, p_bk, p_bv, s_ak, s_av, s_bk, s_bv)
```

```python
import functools
import math

import jax
import jax.numpy as jnp
from jax import lax
from jax.experimental import pallas as pl
from jax.experimental.pallas import tpu as pltpu

F32 = jnp.float32
BF16 = jnp.bfloat16

D_MODEL = 2048
CHUNK = 64
HEAD_DIM = 64
A_HEADS = 16
A_KV_HEADS = 4
A_PAST_CHUNKS = 2
B_HEADS = 16
B_PAST_CHUNKS = 8
B_REL_CLIP = 256
T5_BUCKETS = 32
T5_MAX_DIST = (A_PAST_CHUNKS + 1) * CHUNK
N_EXPERTS = 32
TOP_K = 4
D_FF = D_MODEL
SWIGLU_ALPHA = 1.702
SWIGLU_LIMIT = 7.0
NORM_EPS = 1e-6
NEG_INF = -1e30
PAST_LEN = 1024

A_QW = A_HEADS * HEAD_DIM
A_KVW = A_KV_HEADS * HEAD_DIM
B_W = B_HEADS * HEAD_DIM
PA = A_PAST_CHUNKS * CHUNK
PB = B_PAST_CHUNKS * CHUNK
LA = PA + CHUNK
LB = PB + CHUNK

LANES = 128
MXU_COLS = 256

A_KVD = 2 * A_KVW
PROJ_W = A_QW + 3 * B_W + 2 * D_MODEL + 2 * A_KVD
PROJ_TN = 512
KV32_W = 2 * B_W + 2 * A_KVD
_NORM_TILES = (0, 1, 2, 3, 4, 5, 16)
_SIGMOID_LO, _SIGMOID_HI = 8, 16

MOE_TM = 512
MOE_TF = 512
MOE_NF = D_FF // MOE_TF
MOE_SHARE = MOE_TM // MOE_NF
COMB_TT = 256
ROUTER_TM = 256
MERGE_TM = 256
ATTN_AHEAD = 6


def _cparams(sem, vmem_mb):
    return pltpu.CompilerParams(dimension_semantics=sem, vmem_limit_bytes=vmem_mb * 1024 * 1024)


def _proj_body(x_ref, g_ref, w_ref, cg_ref, gm_ref, h_ref, kv_ref, xn_s):
    j = pl.program_id(1)

    @pl.when(j == 0)
    def _():
        x = x_ref[...]
        ms = jnp.mean(x * x, axis=-1, keepdims=True)
        xn_s[...] = (x * lax.rsqrt(ms + NORM_EPS) * g_ref[...]).astype(BF16)

    is_norm = functools.reduce(jnp.logical_or, [j == t for t in _NORM_TILES])
    is_sig = jnp.logical_and(j >= _SIGMOID_LO, j < _SIGMOID_HI)

    def strips(epilogue, write_kv, dots_first=False):
        col = [slice(c * MXU_COLS, (c + 1) * MXU_COLS) for c in range(PROJ_TN // MXU_COLS)]
        mm = lambda cs: jnp.dot(xn_s[...], w_ref[:, cs], preferred_element_type=F32)
        accs = [mm(cs) for cs in col] if dots_first else None
        for c, cs in enumerate(col):
            y = epilogue(accs[c] if dots_first else mm(cs), cs)
            h_ref[:, cs] = y.astype(BF16)
            if write_kv:
                kv_ref[:, cs] = y

    def norm(a, cs):
        ss = jnp.dot((a * a).astype(BF16), gm_ref[...], preferred_element_type=F32)
        return a * lax.rsqrt(ss * (1.0 / HEAD_DIM) + NORM_EPS) * cg_ref[:, cs]

    @pl.when(is_norm)
    def _():
        strips(norm, True, dots_first=True)

    @pl.when(is_sig)
    def _():
        strips(lambda a, cs: 0.5 * jnp.tanh(0.5 * a) + 0.5, False)

    @pl.when(jnp.logical_not(jnp.logical_or(is_norm, is_sig)))
    def _():
        strips(lambda a, cs: a, True)


def _kv_tile(j):
    return jnp.clip(j - 4, 0, 3) + (j >= 16).astype(jnp.int32) + (j >= 17).astype(jnp.int32)


def _proj(x2d, g, w_perm, cg, gm, tm):
    n = x2d.shape[0]
    grid = (n // tm, PROJ_W // PROJ_TN)
    return pl.pallas_call(
        _proj_body,
        grid=grid,
        in_specs=[
            pl.BlockSpec((tm, D_MODEL), lambda i, j: (i, 0)),
            pl.BlockSpec((1, D_MODEL), lambda i, j: (0, 0)),
            pl.BlockSpec((D_MODEL, PROJ_TN), lambda i, j: (0, j)),
            pl.BlockSpec((1, PROJ_TN), lambda i, j: (0, j)),
            pl.BlockSpec((MXU_COLS, MXU_COLS), lambda i, j: (0, 0)),
        ],
        out_specs=[
            pl.BlockSpec((tm, PROJ_TN), lambda i, j: (i, j)),
            pl.BlockSpec((tm, PROJ_TN), lambda i, j: (i, _kv_tile(j))),
        ],
        out_shape=[
            jax.ShapeDtypeStruct((n, PROJ_W), BF16),
            jax.ShapeDtypeStruct((n, KV32_W), F32),
        ],
        scratch_shapes=[pltpu.VMEM((tm, D_MODEL), BF16)],
        compiler_params=_cparams(("parallel", "arbitrary"), 48),
        name="proj",
    )(x2d, g, w_perm, cg, gm)


def _attn_body(sink_ref, qa_ref, qb_ref, kbp_ref, kbc_ref, vbp_ref, vbc_ref,
               kap_ref, kac_ref, vap_ref, vac_ref, ba_ref, bb_ref,
               oa_ref, ob_ref, kb_s, vb_s, ka_s, va_s, *, cpb, first_pos, hi_a, hi_b):
    i = pl.program_id(1)
    qb_rows = cpb * CHUNK
    kb_s[0:PB, :] = kbp_ref[0]
    kb_s[PB:PB + qb_rows, :] = kbc_ref[0]
    vb_s[0:PB, :] = vbp_ref[0]
    vb_s[PB:PB + qb_rows, :] = vbc_ref[0]
    ka_s[0:PA, :] = kap_ref[0]
    ka_s[PA:PA + qb_rows, :] = kac_ref[0]
    va_s[0:PA, :] = vap_ref[0]
    va_s[PA:PA + qb_rows, :] = vac_ref[0]

    lane_a = lax.broadcasted_iota(jnp.int32, (1, LA), 1)
    lane_b = lax.broadcasted_iota(jnp.int32, (1, LB), 1)
    low_half = lax.broadcasted_iota(jnp.int32, (1, LANES), 1) < HEAD_DIM
    nt = (((1,), (1,)), ((), ()))

    tasks = [("a", h) for h in range(A_HEADS)] + [("b", h) for h in range(B_HEADS)]

    def scores(task, r0, valid_a, valid_b):
        mixer, h = task
        p, half = h // 2, h % 2
        cols = slice(p * LANES, (p + 1) * LANES)
        if mixer == "a":
            kcols = slice((p // 2) * LANES, (p // 2 + 1) * LANES)
            q_p, k_p, bias, valid = qa_ref[0, pl.ds(r0, CHUNK), cols], ka_s[pl.ds(r0, LA), kcols], ba_ref[h], valid_a
        else:
            q_p, k_p, bias, valid = qb_ref[0, pl.ds(r0, CHUNK), cols], kb_s[pl.ds(r0, LB), cols], bb_ref[h], valid_b
        qm = jnp.where(low_half if half == 0 else jnp.logical_not(low_half), q_p, jnp.zeros_like(q_p))
        s = lax.dot_general(qm, k_p, nt, preferred_element_type=F32) + bias
        if valid is not None:
            s = jnp.where(valid, s, NEG_INF)
        return s

    def attend(task, s, r0):
        mixer, h = task
        p = h // 2
        if mixer == "a":
            kcols = slice((p // 2) * LANES, (p // 2 + 1) * LANES)
            v_p, sink = va_s[pl.ds(r0, LA), kcols], sink_ref[h]
        else:
            v_p, sink = vb_s[pl.ds(r0, LB), slice(p * LANES, (p + 1) * LANES)], None
        m = jnp.max(s, axis=-1, keepdims=True)
        if sink is not None:
            m = jnp.maximum(m, sink)
        e = jnp.exp(s - m)
        l = jnp.sum(e, axis=-1, keepdims=True)
        if sink is not None:
            l = l + jnp.exp(sink - m)
        o = jnp.dot(e.astype(BF16), v_p, preferred_element_type=F32)
        return o / l

    def make_chunk(masked):
        def chunk(jc, carry):
            r0 = pl.multiple_of(jc * CHUNK, CHUNK)
            valid_a = valid_b = None
            if masked:
                start = first_pos + (i * cpb + jc) * CHUNK
                valid_a = jnp.logical_and(lane_a >= jnp.maximum(PA - start, 0), lane_a < hi_a)
                valid_b = jnp.logical_and(lane_b >= jnp.maximum(PB - start, 0), lane_b < hi_b)
            pending, outs = {}, {}
            for t in range(len(tasks) + ATTN_AHEAD):
                if t < len(tasks):
                    pending[t] = scores(tasks[t], r0, valid_a, valid_b)
                d = t - ATTN_AHEAD
                if d >= 0:
                    mixer, h = tasks[d]
                    outs[h % 2] = attend(tasks[d], pending.pop(d), r0)
                    if h % 2 == 1:
                        cols = slice((h // 2) * LANES, (h // 2 + 1) * LANES)
                        o_ref = oa_ref if mixer == "a" else ob_ref
                        o_ref[0, pl.ds(r0, CHUNK), cols] = jnp.where(low_half, outs[0], outs[1]).astype(BF16)
            return carry
        return chunk

    n_masked_blocks = pl.cdiv(max(PB - first_pos, 0), cpb * CHUNK)
    if hi_a < LA or hi_b < LB:
        lax.fori_loop(0, cpb, make_chunk(True), 0)
    else:
        @pl.when(i < n_masked_blocks)
        def _():
            lax.fori_loop(0, cpb, make_chunk(True), 0)

        @pl.when(i >= n_masked_blocks)
        def _():
            lax.fori_loop(0, cpb, make_chunk(False), 0)


def _attn(sinks, hq, kbp, kbc, vbp, vbc, kap, kac, vap, vac, bias_a, bias_b, *,
          cpb, first_pos, hi_a, hi_b, cols):
    nb, s, _ = hq.shape
    qb_rows = cpb * CHUNK
    nblk = s // qb_rows
    rb = qb_rows // PB if qb_rows >= PB else None
    ra = qb_rows // PA if qb_rows >= PA else None

    def prev_map(ratio, col):
        if ratio is None:
            return lambda b, i, *_: (b, 0, col)
        return lambda b, i, *_: (b, jnp.maximum(i * ratio - 1, 0), col)

    def cur_map(col):
        return lambda b, i, *_: (b, i, col)

    in_specs = [
        pl.BlockSpec((1, qb_rows, A_QW), cur_map(cols["qa"])),
        pl.BlockSpec((1, qb_rows, B_W), cur_map(cols["qb"])),
        pl.BlockSpec((1, PB, B_W), prev_map(rb, cols["kbp"])),
        pl.BlockSpec((1, qb_rows, B_W), cur_map(cols["kbc"])),
        pl.BlockSpec((1, PB, B_W), prev_map(rb, cols["vbp"])),
        pl.BlockSpec((1, qb_rows, B_W), cur_map(cols["vbc"])),
        pl.BlockSpec((1, PA, A_KVD), prev_map(ra, cols["kap"])),
        pl.BlockSpec((1, qb_rows, A_KVD), cur_map(cols["kac"])),
        pl.BlockSpec((1, PA, A_KVD), prev_map(ra, cols["vap"])),
        pl.BlockSpec((1, qb_rows, A_KVD), cur_map(cols["vac"])),
        pl.BlockSpec((A_HEADS, CHUNK, LA), lambda b, i, *_: (0, 0, 0)),
        pl.BlockSpec((B_HEADS, CHUNK, LB), lambda b, i, *_: (0, 0, 0)),
    ]
    out_specs = [
        pl.BlockSpec((1, qb_rows, A_QW), lambda b, i, *_: (b, i, 0)),
        pl.BlockSpec((1, qb_rows, B_W), lambda b, i, *_: (b, i, 0)),
    ]
    body = functools.partial(_attn_body, cpb=cpb, first_pos=first_pos, hi_a=hi_a, hi_b=hi_b)
    return pl.pallas_call(
        body,
        grid_spec=pltpu.PrefetchScalarGridSpec(
            num_scalar_prefetch=1,
            grid=(nb, nblk),
            in_specs=in_specs,
            out_specs=out_specs,
            scratch_shapes=[
                pltpu.VMEM((PB + qb_rows, B_W), BF16),
                pltpu.VMEM((PB + qb_rows, B_W), BF16),
                pltpu.VMEM((PA + qb_rows, A_KVD), BF16),
                pltpu.VMEM((PA + qb_rows, A_KVD), BF16),
            ],
        ),
        out_shape=[
            jax.ShapeDtypeStruct((nb, s, A_QW), BF16),
            jax.ShapeDtypeStruct((nb, s, B_W), BF16),
        ],
        compiler_params=_cparams(("parallel", "arbitrary"), 48),
        name="attn",
    )(sinks, hq, hq, kbp, kbc, vbp, vbc, kap, kac, vap, vac, bias_a, bias_b)


def _merge_body(oa_p, ob_p, ga_p, gb_p, x_p, oa_s, ob_s, ga_s, gb_s, x_s,
                woa_ref, wob_ref, wout_ref, fg_ref, wr_ref, br_ref, x2_ref, xn_ref, lg_ref, *,
                n_prompt_tiles):
    def tile(oa_ref, ob_ref, ga_ref, gb_ref, x_ref):
        ya = jnp.dot(oa_ref[...], woa_ref[...], preferred_element_type=F32)
        yb = jnp.dot(ob_ref[...], wob_ref[...], preferred_element_type=F32)
        z = ga_ref[...].astype(F32) * ya + gb_ref[...].astype(F32) * yb
        y = jnp.dot(z.astype(BF16), wout_ref[...], preferred_element_type=F32)
        x2 = x_ref[...] + y
        x2_ref[...] = x2
        ms = jnp.mean(x2 * x2, axis=-1, keepdims=True)
        xn = x2 * lax.rsqrt(ms + NORM_EPS) * fg_ref[...]
        xn_ref[...] = xn
        lg_ref[...] = jnp.dot(xn.astype(BF16), wr_ref[...], preferred_element_type=F32) + br_ref[...]

    i = pl.program_id(0)

    @pl.when(i < n_prompt_tiles)
    def _():
        tile(oa_p, ob_p, ga_p, gb_p, x_p)

    @pl.when(i >= n_prompt_tiles)
    def _():
        tile(oa_s, ob_s, ga_s, gb_s, x_s)


def _merge(prompt, sample, w_oa, w_ob, w_out, fg, wr, br):
    tm = MERGE_TM
    n_p, n_s = prompt[3].shape[0], sample[3].shape[0]
    npt, nst = n_p // tm, n_s // tm
    n_out = n_p + n_s
    const = lambda i: (0, 0)
    resident = functools.partial(pl.BlockSpec, index_map=const, pipeline_mode=pl.Buffered(1))

    def token_specs(row):
        return [
            pl.BlockSpec((tm, A_QW), lambda i: (row(i), 0)),
            pl.BlockSpec((tm, B_W), lambda i: (row(i), 0)),
            pl.BlockSpec((tm, D_MODEL), lambda i: (row(i), 2)),
            pl.BlockSpec((tm, D_MODEL), lambda i: (row(i), 3)),
            pl.BlockSpec((tm, D_MODEL), lambda i: (row(i), 0)),
        ]

    in_specs = (token_specs(lambda i: jnp.minimum(i, npt - 1))
                + token_specs(lambda i: jnp.clip(i - npt, 0, nst - 1))
                + [resident((A_QW, D_MODEL)), resident((B_W, D_MODEL)), resident((D_MODEL, D_MODEL)),
                   resident((1, D_MODEL)), resident((D_MODEL, LANES)), resident((1, LANES))])
    oa_p, ob_p, h_p, x_p = prompt
    oa_s, ob_s, h_s, x_s = sample
    return pl.pallas_call(
        functools.partial(_merge_body, n_prompt_tiles=npt),
        grid=(npt + nst,),
        in_specs=in_specs,
        out_specs=[
            pl.BlockSpec((tm, D_MODEL), lambda i: (i, 0)),
            pl.BlockSpec((tm, D_MODEL), lambda i: (i, 0)),
            pl.BlockSpec((tm, LANES), lambda i: (i, 0)),
        ],
        out_shape=[
            jax.ShapeDtypeStruct((n_out, D_MODEL), F32),
            jax.ShapeDtypeStruct((n_out, D_MODEL), F32),
            jax.ShapeDtypeStruct((n_out, LANES), F32),
        ],
        compiler_params=_cparams(("arbitrary",), 56),
        name="merge",
    )(oa_p, ob_p, h_p, h_p, x_p, oa_s, ob_s, h_s, h_s, x_s, w_oa, w_ob, w_out, fg, wr, br)


def _router_body(lg_ref, tri_ref, idx_ref, gate_ref, rank_ref, cnt_ref, carry):
    i = pl.program_id(0)

    @pl.when(i == 0)
    def _():
        carry[...] = jnp.zeros_like(carry)

    lane = lax.broadcasted_iota(jnp.int32, lg_ref.shape, 1)
    x = jnp.where(lane < N_EXPERTS, lg_ref[...], -jnp.inf)
    vals, hots = [], []
    idx_out = jnp.zeros(lg_ref.shape, jnp.int32)
    for k in range(TOP_K):
        m = jnp.max(x, axis=-1, keepdims=True)
        am = jnp.min(jnp.where(x == m, lane, LANES), axis=-1, keepdims=True)
        hot = lane == am
        vals.append(m)
        hots.append(hot)
        idx_out = jnp.where(lane == k, am, idx_out)
        x = jnp.where(hot, -jnp.inf, x)
    es = [jnp.exp(v - vals[0]) for v in vals]
    denom = functools.reduce(lambda a, b: a + b, es)
    sel = functools.reduce(jnp.logical_or, hots)
    sel_f = jnp.where(sel, 1.0, 0.0)
    before = jnp.dot(tri_ref[...], sel_f.astype(BF16), preferred_element_type=F32) + carry[...]
    gate_out = jnp.zeros(lg_ref.shape, F32)
    rank_out = jnp.zeros(lg_ref.shape, F32)
    for k in range(TOP_K):
        gate_out = jnp.where(lane == k, es[k] / denom, gate_out)
        rk = jnp.sum(jnp.where(hots[k], before, 0.0), axis=-1, keepdims=True)
        rank_out = jnp.where(lane == k, rk, rank_out)
    idx_ref[...] = idx_out
    gate_ref[...] = gate_out
    rank_ref[...] = rank_out.astype(jnp.int32)
    total = carry[...] + jnp.sum(sel_f, axis=0, keepdims=True)
    carry[...] = total
    cnt_ref[...] = total.astype(jnp.int32)


def _router(logits):
    n = logits.shape[0]
    tm = ROUTER_TM
    r = jnp.arange(tm)
    tri = (r[None, :] < r[:, None]).astype(BF16)
    tile = pl.BlockSpec((tm, LANES), lambda i: (i, 0))
    return pl.pallas_call(
        _router_body,
        grid=(n // tm,),
        in_specs=[tile, pl.BlockSpec((tm, tm), lambda i: (0, 0))],
        out_specs=[tile, tile, tile, pl.BlockSpec((1, LANES), lambda i: (0, 0))],
        out_shape=[
            jax.ShapeDtypeStruct((n, LANES), jnp.int32),
            jax.ShapeDtypeStruct((n, LANES), F32),
            jax.ShapeDtypeStruct((n, LANES), jnp.int32),
            jax.ShapeDtypeStruct((1, LANES), jnp.int32),
        ],
        scratch_shapes=[pltpu.VMEM((1, LANES), F32)],
        compiler_params=_cparams(("arbitrary",), 16),
        name="router",
    )(logits, tri)


def _row_gather(idx_ref, n, src_hbm, dst, sem):
    def body(r, c):
        t = idx_ref[0, 0, r]
        pltpu.make_async_copy(src_hbm.at[pl.ds(t, 1), :], dst.at[pl.ds(r, 1), :], sem).start()
        return c
    lax.fori_loop(0, n, body, 0)


def _expert_body(bexp_ref, nv_ref, tokc_ref, tokn_ref, xn_hbm,
                 w1g_ref, w1u_ref, b1g_ref, b1u_ref, w2_ref, b2_ref,
                 out_ref, xg, xb, act_s, sem):
    i = pl.program_id(0)
    f = pl.program_id(1)
    nf = MOE_NF
    nv = nv_ref[0]
    slot = i % 2

    def rows_landed(s):
        return pltpu.make_async_copy(xn_hbm.at[pl.ds(0, MOE_TM), :], xg.at[s], sem.at[s])

    @pl.when(jnp.logical_and(i >= nv, f == 0))
    def _():
        out_ref[...] = jnp.zeros_like(out_ref)

        @pl.when(i == nv)
        def _():
            rows_landed(slot).wait()

    @pl.when(jnp.logical_and(i < nv, f == 0))
    def _():
        @pl.when(i == 0)
        def _():
            _row_gather(tokc_ref, MOE_TM, xn_hbm, xg.at[0], sem.at[0])

        rows_landed(slot).wait()
        xb[...] = xg[slot].astype(BF16)

    @pl.when(i < nv)
    def _():
        base = f * MOE_SHARE
        for r in range(MOE_SHARE):
            t = tokn_ref[0, 0, base + r]
            pltpu.make_async_copy(xn_hbm.at[pl.ds(t, 1), :], xg.at[1 - slot, pl.ds(base + r, 1), :],
                                  sem.at[1 - slot]).start()
        x = xb[...]
        for c in range(MOE_TF // MXU_COLS):
            cs = slice(c * MXU_COLS, (c + 1) * MXU_COLS)
            hg = jnp.dot(x, w1g_ref[0, :, cs], preferred_element_type=F32) + b1g_ref[0, :, cs]
            hu = jnp.dot(x, w1u_ref[0, :, cs], preferred_element_type=F32) + b1u_ref[0, :, cs]
            hg = jnp.minimum(hg, SWIGLU_LIMIT)
            hu = jnp.clip(hu, -SWIGLU_LIMIT, SWIGLU_LIMIT)
            sig = 0.5 * jnp.tanh((0.5 * SWIGLU_ALPHA) * hg) + 0.5
            act_s[f, :, cs] = (hg * sig * (hu + 1.0)).astype(BF16)

    @pl.when(jnp.logical_and(i < nv, f == nf - 1))
    def _():
        act = jnp.concatenate([act_s[k] for k in range(nf)], axis=-1)
        out_ref[...] = jnp.dot(act, w2_ref[0], preferred_element_type=F32) + b2_ref[0]


def _experts(blk_exp, nvalid, row_tok3, xn, w1, b1, w2, b2):
    nblk = row_tok3.shape[0]
    nf = MOE_NF
    rows = nblk * MOE_TM

    def fz(i, f, nv):
        return jnp.where(i < nv[0], f, nf - 1)

    in_specs = [
        pl.BlockSpec((1, 1, MOE_TM), lambda i, f, be, nv: (i, 0, 0), memory_space=pltpu.SMEM),
        pl.BlockSpec((1, 1, MOE_TM), lambda i, f, be, nv: (jnp.minimum(i + 1, nblk - 1), 0, 0),
                     memory_space=pltpu.SMEM),
        pl.BlockSpec(memory_space=pl.ANY),
        pl.BlockSpec((1, D_MODEL, MOE_TF), lambda i, f, be, nv: (be[i], 0, fz(i, f, nv))),
        pl.BlockSpec((1, D_MODEL, MOE_TF), lambda i, f, be, nv: (be[i], 0, nf + fz(i, f, nv))),
        pl.BlockSpec((1, 1, MOE_TF), lambda i, f, be, nv: (be[i], 0, fz(i, f, nv))),
        pl.BlockSpec((1, 1, MOE_TF), lambda i, f, be, nv: (be[i], 0, nf + fz(i, f, nv))),
        pl.BlockSpec((1, D_FF, D_MODEL), lambda i, f, be, nv: (be[i], 0, 0)),
        pl.BlockSpec((1, 1, D_MODEL), lambda i, f, be, nv: (be[i], 0, 0)),
    ]
    return pl.pallas_call(
        _expert_body,
        grid_spec=pltpu.PrefetchScalarGridSpec(
            num_scalar_prefetch=2,
            grid=(nblk, nf),
            in_specs=in_specs,
            out_specs=pl.BlockSpec((MOE_TM, D_MODEL), lambda i, f, be, nv: (i, 0)),
            scratch_shapes=[
                pltpu.VMEM((2, MOE_TM, D_MODEL), F32),
                pltpu.VMEM((MOE_TM, D_MODEL), BF16),
                pltpu.VMEM((nf, MOE_TM, MOE_TF), BF16),
                pltpu.SemaphoreType.DMA((2,)),
            ],
        ),
        out_shape=jax.ShapeDtypeStruct((rows, D_MODEL), F32),
        compiler_params=_cparams(("arbitrary", "arbitrary"), 56),
        name="experts",
    )(blk_exp, nvalid, row_tok3, row_tok3, xn, w1, w1, b1, b1, w2, b2)


def _combine_body(posc_ref, posn_ref, x2_ref, gate_ref, ys_hbm, outp_ref, outs_ref, buf, sem, *,
                  n_prompt_tiles):
    i = pl.program_id(0)
    n = pl.num_programs(0)
    slot = i % 2
    nrow = TOP_K * COMB_TT

    def issue(pref, s):
        def body(r, c):
            for k in range(TOP_K):
                p = pref[0, 0, TOP_K * r + k]
                pltpu.make_async_copy(ys_hbm.at[pl.ds(p, 1), :],
                                      buf.at[s, pl.ds(k * COMB_TT + r, 1), :], sem.at[s]).start()
            return c
        lax.fori_loop(0, COMB_TT, body, 0)

    @pl.when(i == 0)
    def _():
        issue(posc_ref, 0)

    pltpu.make_async_copy(ys_hbm.at[pl.ds(0, nrow), :], buf.at[slot], sem.at[slot]).wait()

    @pl.when(i + 1 < n)
    def _():
        issue(posn_ref, 1 - slot)

    y = x2_ref[...]
    g = gate_ref[...]
    for k in range(TOP_K):
        y = y + g[:, k:k + 1] * buf[slot, k * COMB_TT:(k + 1) * COMB_TT, :]

    @pl.when(i < n_prompt_tiles)
    def _():
        outp_ref[...] = y

    @pl.when(i >= n_prompt_tiles)
    def _():
        outs_ref[...] = y


def _combine(pos3, x2, gates, ys, n_prompt):
    n_tok = x2.shape[0]
    nt = n_tok // COMB_TT
    npt = n_prompt // COMB_TT
    assert n_tok - n_prompt == COMB_TT
    body = functools.partial(_combine_body, n_prompt_tiles=npt)
    return pl.pallas_call(
        body,
        grid=(nt,),
        in_specs=[
            pl.BlockSpec((1, 1, TOP_K * COMB_TT), lambda i: (i, 0, 0), memory_space=pltpu.SMEM),
            pl.BlockSpec((1, 1, TOP_K * COMB_TT), lambda i: (jnp.minimum(i + 1, nt - 1), 0, 0),
                         memory_space=pltpu.SMEM),
            pl.BlockSpec((COMB_TT, D_MODEL), lambda i: (i, 0)),
            pl.BlockSpec((COMB_TT, LANES), lambda i: (i, 0)),
            pl.BlockSpec(memory_space=pl.ANY),
        ],
        out_specs=[
            pl.BlockSpec((COMB_TT, D_MODEL), lambda i: (jnp.minimum(i, npt - 1), 0)),
            pl.BlockSpec((COMB_TT, D_MODEL), lambda i: (0, 0)),
        ],
        out_shape=[
            jax.ShapeDtypeStruct((n_prompt, D_MODEL), F32),
            jax.ShapeDtypeStruct((COMB_TT, D_MODEL), F32),
        ],
        scratch_shapes=[
            pltpu.VMEM((2, TOP_K * COMB_TT, D_MODEL), F32),
            pltpu.SemaphoreType.DMA((2,)),
        ],
        compiler_params=_cparams(("arbitrary",), 40),
        name="combine",
    )(pos3, pos3, x2, gates, ys)


def _t5_bucket(rel):
    nb = T5_BUCKETS // 2
    max_exact = nb // 2
    ret = (rel > 0).astype(jnp.int32) * nb
    n = jnp.abs(rel)
    large = max_exact + (jnp.log(jnp.maximum(n, 1).astype(F32) / max_exact)
                         / math.log(T5_MAX_DIST / max_exact) * (nb - max_exact)).astype(jnp.int32)
    large = jnp.minimum(large, nb - 1)
    return ret + jnp.where(n < max_exact, n, large)


def _band_rel(pad):
    return jnp.arange(pad + CHUNK)[None, :] - pad - jnp.arange(CHUNK)[:, None]


def _dup_heads(t):
    lead = t.shape[:-1]
    t = t.reshape(*lead, A_KV_HEADS, 1, HEAD_DIM)
    return jnp.broadcast_to(t, (*lead, A_KV_HEADS, 2, HEAD_DIM)).reshape(*lead, A_KVD)


def _layout(top_i, rank, counts, n_tok):
    nk = n_tok * TOP_K
    nblk = nk // MOE_TM + N_EXPERTS
    rows = nblk * MOE_TM
    padded = (counts + MOE_TM - 1) // MOE_TM * MOE_TM
    pad_end = jnp.cumsum(padded)
    pad_start = pad_end - padded
    start = jnp.cumsum(counts) - counts
    nvalid = (pad_end[-1] // MOE_TM).astype(jnp.int32)
    blk = jnp.arange(nblk, dtype=jnp.int32)
    blk_exp = jnp.sum((pad_end[None, :] <= (blk * MOE_TM)[:, None]).astype(jnp.int32), axis=1)
    last_used = jnp.sum((pad_end <= (nvalid - 1) * MOE_TM).astype(jnp.int32))
    blk_exp = jnp.minimum(jnp.where(blk < nvalid, blk_exp, last_used), N_EXPERTS - 1)
    experts = jnp.arange(N_EXPERTS, dtype=jnp.int32)
    table = lambda tab, idx: jnp.sum(jnp.where(idx[..., None] == experts, tab, 0), axis=-1)
    pos = table(pad_start, top_i) + rank
    flat_tok = jnp.arange(nk, dtype=jnp.int32) // TOP_K
    _, sorted_tok = lax.sort_key_val(top_i.reshape(-1), flat_tok, is_stable=True)
    sorted_tok = jnp.concatenate([sorted_tok, jnp.zeros((MOE_TM,), jnp.int32)])
    win = jnp.clip(table(start, blk_exp) + blk * MOE_TM - table(pad_start, blk_exp), 0, nk)
    row_tok = jax.vmap(lambda o: lax.dynamic_slice(sorted_tok, (o,), (MOE_TM,)))(win)
    return (row_tok.reshape(nblk, 1, MOE_TM), blk_exp.astype(jnp.int32), nvalid.reshape(1),
            pos.reshape(n_tok // COMB_TT, 1, TOP_K * COMB_TT))


def kernel(x_prompt, x_sample, cache_a_k, cache_a_v, cache_b_k, cache_b_v, attn_norm, w_in,
           a_q_norm, a_k_norm, b_q_norm, b_k_norm, a_sinks, t5_table, b_rel_table, w_oa, w_ob,
           w_out, ffn_norm, router_w, router_b, w1, b1, w2, b2):
    batch, seq, _ = x_prompt.shape
    dec_b, dec_s, _ = x_sample.shape
    assert attn_norm.shape[0] == 1, "single layer"
    assert dec_s <= CHUNK and PAST_LEN % CHUNK == 0 and PAST_LEN >= PB
    n_prompt = batch * seq
    n_sample = dec_b * dec_s

    wi = w_in[0]
    c = [0]
    for wdt in (A_QW, A_KVW, A_KVW, B_W, B_W, B_W, D_MODEL, D_MODEL):
        c.append(c[-1] + wdt)
    w_qa, w_ka, w_va, w_qb, w_kb, w_vb, w_ga, w_gb = [wi[:, c[k]:c[k + 1]] for k in range(8)]
    w_perm = jnp.concatenate([w_qa, w_qb, w_kb, w_vb, w_ga, w_gb, _dup_heads(w_ka), _dup_heads(w_va)],
                             axis=1).astype(BF16)
    scale = HEAD_DIM ** -0.5
    ones = lambda n: jnp.ones((n,), F32)
    cg = jnp.concatenate([
        jnp.tile(a_q_norm[0] * scale, A_HEADS), jnp.tile(b_q_norm[0] * scale, B_HEADS),
        jnp.tile(b_k_norm[0], B_HEADS), ones(B_W), ones(2 * D_MODEL),
        jnp.tile(a_k_norm[0], 2 * A_KV_HEADS), ones(A_KVD)]).reshape(1, PROJ_W).astype(F32)
    hd = jnp.arange(MXU_COLS) // HEAD_DIM
    gm = (hd[:, None] == hd[None, :]).astype(BF16)
    g_attn = attn_norm[0].reshape(1, D_MODEL)
    bias_a = jnp.transpose(t5_table[_t5_bucket(_band_rel(PA))], (2, 0, 1)).astype(F32)
    idx_b = jnp.clip(_band_rel(PB), -B_REL_CLIP, CHUNK - 1) + B_REL_CLIP
    bias_b = jnp.transpose(b_rel_table[0][idx_b], (2, 0, 1)).astype(F32)
    sinks = a_sinks[0].astype(F32)
    woa, wob, wout = w_oa[0].astype(BF16), w_ob[0].astype(BF16), w_out[0].astype(BF16)
    fg = ffn_norm[0].reshape(1, D_MODEL)
    wr = jnp.pad(router_w[0], ((0, 0), (0, LANES - N_EXPERTS))).astype(BF16)
    br = jnp.pad(router_b[0], (0, LANES - N_EXPERTS)).reshape(1, LANES)
    w1b, w2b = w1[0].astype(BF16), w2[0].astype(BF16)
    b1r = b1[0].reshape(N_EXPERTS, 1, 2 * D_FF)
    b2r = b2[0].reshape(N_EXPERTS, 1, D_MODEL)

    xp2 = x_prompt.reshape(n_prompt, D_MODEL)
    h_p, kv_p = _proj(xp2, g_attn, w_perm, cg, gm, tm=1024)
    h_p3 = h_p.reshape(batch, seq, PROJ_W)
    cols_p = dict(qa=0, qb=1, kbp=2, kbc=2, vbp=3, vbc=3, kap=16, kac=16, vap=17, vac=17)
    oa_p, ob_p = _attn(sinks, h_p3, h_p3, h_p3, h_p3, h_p3, h_p3, h_p3, h_p3, h_p3, bias_a, bias_b,
                       cpb=8, first_pos=0, hi_a=LA, hi_b=LB, cols=cols_p)
    n_tok = n_prompt + n_sample

    xs_pad = jnp.pad(x_sample, ((0, 0), (0, CHUNK - dec_s), (0, 0))).reshape(dec_b * CHUNK, D_MODEL)
    h_s, kv_s = _proj(xs_pad, g_attn, w_perm, cg, gm, tm=dec_b * CHUNK)
    h_s3 = h_s.reshape(dec_b, CHUNK, PROJ_W)
    ckb = cache_b_k[0].reshape(dec_b, PB, B_W).astype(BF16)
    cvb = cache_b_v[0].reshape(dec_b, PB, B_W).astype(BF16)
    cka = _dup_heads(cache_a_k[0].reshape(dec_b, PA, A_KVW)).astype(BF16)
    cva = _dup_heads(cache_a_v[0].reshape(dec_b, PA, A_KVW)).astype(BF16)
    cols_s = dict(qa=0, qb=1, kbp=0, kbc=2, vbp=0, vbc=3, kap=0, kac=16, vap=0, vac=17)
    oa_s, ob_s = _attn(sinks, h_s3, ckb, h_s3, cvb, h_s3, cka, h_s3, cva, h_s3, bias_a, bias_b,
                       cpb=1, first_pos=PAST_LEN, hi_a=PA + dec_s, hi_b=PB + dec_s, cols=cols_s)
    keep_rows = lambda t: t[:, :dec_s].reshape(n_sample, t.shape[-1])
    x2, xn, logits = _merge(
        (oa_p.reshape(n_prompt, A_QW), ob_p.reshape(n_prompt, B_W), h_p, xp2),
        (keep_rows(oa_s), keep_rows(ob_s), keep_rows(h_s3), x_sample.reshape(n_sample, D_MODEL)),
        woa, wob, wout, fg, wr, br)

    top_i, gates, rank, counts = _router(logits)
    row_tok3, blk_exp, nvalid, pos3 = _layout(top_i[:, :TOP_K], rank[:, :TOP_K], counts[0, :N_EXPERTS], n_tok)
    ys = _experts(blk_exp, nvalid, row_tok3, xn, w1b, b1r, w2b, b2r)
    y_p, y_s = _combine(pos3, x2, gates, ys, n_prompt)

    kv_p3 = kv_p.reshape(batch, seq, KV32_W)
    kv_s3 = kv_s.reshape(dec_b, CHUNK, KV32_W)
    undup = lambda t: t.reshape(*t.shape[:-1], A_KV_HEADS, 2, HEAD_DIM)[..., 0, :]
    heads_b = lambda t: t.reshape(*t.shape[:-1], B_HEADS, HEAD_DIM)
    o_kb, o_vb, o_ka, o_va = 0, B_W, 2 * B_W, 2 * B_W + A_KVD
    p_bk = heads_b(kv_p3[:, seq - PB:, o_kb:o_kb + B_W])[None]
    p_bv = heads_b(kv_p3[:, seq - PB:, o_vb:o_vb + B_W])[None]
    p_ak = undup(kv_p3[:, seq - PA:, o_ka:o_ka + A_KVD])[None]
    p_av = undup(kv_p3[:, seq - PA:, o_va:o_va + A_KVD])[None]
    s_bk = heads_b(kv_s3[:, :dec_s, o_kb:o_kb + B_W])[None]
    s_bv = heads_b(kv_s3[:, :dec_s, o_vb:o_vb + B_W])[None]
    s_ak = undup(kv_s3[:, :dec_s, o_ka:o_ka + A_KVD])[None]
    s_av = undup(kv_s3[:, :dec_s, o_va:o_va + A_KVD])[None]
    return (y_p.reshape(batch, seq, D_MODEL), y_s.reshape(dec_b, dec_s, D_MODEL),
            p_ak, p_av, p_bk, p_bv, s_ak, s_av, s_bk, s_bv)
```

```python
import functools
import math

import jax
import jax.numpy as jnp
from jax import lax
from jax.experimental import pallas as pl
from jax.experimental.pallas import tpu as pltpu

F32 = jnp.float32
BF16 = jnp.bfloat16

D_MODEL = 2048
CHUNK = 64
HEAD_DIM = 64
A_HEADS = 16
A_KV_HEADS = 4
A_PAST_CHUNKS = 2
B_HEADS = 16
B_PAST_CHUNKS = 8
B_REL_CLIP = 256
T5_BUCKETS = 32
T5_MAX_DIST = (A_PAST_CHUNKS + 1) * CHUNK
N_EXPERTS = 32
TOP_K = 4
D_FF = D_MODEL
SWIGLU_ALPHA = 1.702
SWIGLU_LIMIT = 7.0
NORM_EPS = 1e-6
NEG_INF = -1e30
PAST_LEN = 1024

A_QW = A_HEADS * HEAD_DIM
A_KVW = A_KV_HEADS * HEAD_DIM
B_W = B_HEADS * HEAD_DIM
PA = A_PAST_CHUNKS * CHUNK
PB = B_PAST_CHUNKS * CHUNK
LA = PA + CHUNK
LB = PB + CHUNK

LANES = 128
MXU_COLS = 256

A_KVD = 2 * A_KVW
PROJ_W = A_QW + 3 * B_W + 2 * D_MODEL + 2 * A_KVD
PROJ_TN = 512
KV32_W = 2 * B_W + 2 * A_KVD
_NORM_TILES = (0, 1, 2, 3, 4, 5, 16)
_SIGMOID_LO, _SIGMOID_HI = 8, 16

MOE_TM = 512
MOE_TF = 512
MOE_NF = D_FF // MOE_TF
MOE_SHARE = MOE_TM // MOE_NF
COMB_TT = 256
ROUTER_TM = 256
MERGE_TM = 256
ATTN_AHEAD = 6


def _cparams(sem, vmem_mb):
    return pltpu.CompilerParams(dimension_semantics=sem, vmem_limit_bytes=vmem_mb * 1024 * 1024)


def _proj_body(x_ref, g_ref, w_ref, cg_ref, gm_ref, h_ref, kv_ref, xn_s):
    j = pl.program_id(1)

    @pl.when(j == 0)
    def _():
        x = x_ref[...]
        ms = jnp.mean(x * x, axis=-1, keepdims=True)
        xn_s[...] = (x * lax.rsqrt(ms + NORM_EPS) * g_ref[...]).astype(BF16)

    is_norm = functools.reduce(jnp.logical_or, [j == t for t in _NORM_TILES])
    is_sig = jnp.logical_and(j >= _SIGMOID_LO, j < _SIGMOID_HI)

    def strips(epilogue, write_kv, dots_first=False):
        col = [slice(c * MXU_COLS, (c + 1) * MXU_COLS) for c in range(PROJ_TN // MXU_COLS)]
        mm = lambda cs: jnp.dot(xn_s[...], w_ref[:, cs], preferred_element_type=F32)
        accs = [mm(cs) for cs in col] if dots_first else None
        for c, cs in enumerate(col):
            y = epilogue(accs[c] if dots_first else mm(cs), cs)
            h_ref[:, cs] = y.astype(BF16)
            if write_kv:
                kv_ref[:, cs] = y

    def norm(a, cs):
        ss = jnp.dot((a * a).astype(BF16), gm_ref[...], preferred_element_type=F32)
        return a * lax.rsqrt(ss * (1.0 / HEAD_DIM) + NORM_EPS) * cg_ref[:, cs]

    @pl.when(is_norm)
    def _():
        strips(norm, True, dots_first=True)

    @pl.when(is_sig)
    def _():
        strips(lambda a, cs: 0.5 * jnp.tanh(0.5 * a) + 0.5, False)

    @pl.when(jnp.logical_not(jnp.logical_or(is_norm, is_sig)))
    def _():
        strips(lambda a, cs: a, True)


def _kv_tile(j):
    return jnp.clip(j - 4, 0, 3) + (j >= 16).astype(jnp.int32) + (j >= 17).astype(jnp.int32)


def _proj(x2d, g, w_perm, cg, gm, tm):
    n = x2d.shape[0]
    grid = (n // tm, PROJ_W // PROJ_TN)
    return pl.pallas_call(
        _proj_body,
        grid=grid,
        in_specs=[
            pl.BlockSpec((tm, D_MODEL), lambda i, j: (i, 0)),
            pl.BlockSpec((1, D_MODEL), lambda i, j: (0, 0)),
            pl.BlockSpec((D_MODEL, PROJ_TN), lambda i, j: (0, j)),
            pl.BlockSpec((1, PROJ_TN), lambda i, j: (0, j)),
            pl.BlockSpec((MXU_COLS, MXU_COLS), lambda i, j: (0, 0)),
        ],
        out_specs=[
            pl.BlockSpec((tm, PROJ_TN), lambda i, j: (i, j)),
            pl.BlockSpec((tm, PROJ_TN), lambda i, j: (i, _kv_tile(j))),
        ],
        out_shape=[
            jax.ShapeDtypeStruct((n, PROJ_W), BF16),
            jax.ShapeDtypeStruct((n, KV32_W), F32),
        ],
        scratch_shapes=[pltpu.VMEM((tm, D_MODEL), BF16)],
        compiler_params=_cparams(("parallel", "arbitrary"), 48),
        name="proj",
    )(x2d, g, w_perm, cg, gm)


def _attn_body(sink_ref, qa_ref, qb_ref, kbp_ref, kbc_ref, vbp_ref, vbc_ref,
               kap_ref, kac_ref, vap_ref, vac_ref, ba_ref, bb_ref,
               oa_ref, ob_ref, kb_s, vb_s, ka_s, va_s, *, cpb, first_pos, hi_a, hi_b):
    i = pl.program_id(1)
    qb_rows = cpb * CHUNK
    kb_s[0:PB, :] = kbp_ref[0]
    kb_s[PB:PB + qb_rows, :] = kbc_ref[0]
    vb_s[0:PB, :] = vbp_ref[0]
    vb_s[PB:PB + qb_rows, :] = vbc_ref[0]
    ka_s[0:PA, :] = kap_ref[0]
    ka_s[PA:PA + qb_rows, :] = kac_ref[0]
    va_s[0:PA, :] = vap_ref[0]
    va_s[PA:PA + qb_rows, :] = vac_ref[0]

    lane_a = lax.broadcasted_iota(jnp.int32, (1, LA), 1)
    lane_b = lax.broadcasted_iota(jnp.int32, (1, LB), 1)
    low_half = lax.broadcasted_iota(jnp.int32, (1, LANES), 1) < HEAD_DIM
    nt = (((1,), (1,)), ((), ()))

    tasks = [("a", h) for h in range(A_HEADS)] + [("b", h) for h in range(B_HEADS)]

    def scores(task, r0, valid_a, valid_b):
        mixer, h = task
        p, half = h // 2, h % 2
        cols = slice(p * LANES, (p + 1) * LANES)
        if mixer == "a":
            kcols = slice((p // 2) * LANES, (p // 2 + 1) * LANES)
            q_p, k_p, bias, valid = qa_ref[0, pl.ds(r0, CHUNK), cols], ka_s[pl.ds(r0, LA), kcols], ba_ref[h], valid_a
        else:
            q_p, k_p, bias, valid = qb_ref[0, pl.ds(r0, CHUNK), cols], kb_s[pl.ds(r0, LB), cols], bb_ref[h], valid_b
        qm = jnp.where(low_half if half == 0 else jnp.logical_not(low_half), q_p, jnp.zeros_like(q_p))
        s = lax.dot_general(qm, k_p, nt, preferred_element_type=F32) + bias
        if valid is not None:
            s = jnp.where(valid, s, NEG_INF)
        return s

    def attend(task, s, r0):
        mixer, h = task
        p = h // 2
        if mixer == "a":
            kcols = slice((p // 2) * LANES, (p // 2 + 1) * LANES)
            v_p, sink = va_s[pl.ds(r0, LA), kcols], sink_ref[h]
        else:
            v_p, sink = vb_s[pl.ds(r0, LB), slice(p * LANES, (p + 1) * LANES)], None
        m = jnp.max(s, axis=-1, keepdims=True)
        if sink is not None:
            m = jnp.maximum(m, sink)
        e = jnp.exp(s - m)
        l = jnp.sum(e, axis=-1, keepdims=True)
        if sink is not None:
            l = l + jnp.exp(sink - m)
        o = jnp.dot(e.astype(BF16), v_p, preferred_element_type=F32)
        return o / l

    def make_chunk(masked):
        def chunk(jc, carry):
            r0 = pl.multiple_of(jc * CHUNK, CHUNK)
            valid_a = valid_b = None
            if masked:
                start = first_pos + (i * cpb + jc) * CHUNK
                valid_a = jnp.logical_and(lane_a >= jnp.maximum(PA - start, 0), lane_a < hi_a)
                valid_b = jnp.logical_and(lane_b >= jnp.maximum(PB - start, 0), lane_b < hi_b)
            pending, outs = {}, {}
            for t in range(len(tasks) + ATTN_AHEAD):
                if t < len(tasks):
                    pending[t] = scores(tasks[t], r0, valid_a, valid_b)
                d = t - ATTN_AHEAD
                if d >= 0:
                    mixer, h = tasks[d]
                    outs[h % 2] = attend(tasks[d], pending.pop(d), r0)
                    if h % 2 == 1:
                        cols = slice((h // 2) * LANES, (h // 2 + 1) * LANES)
                        o_ref = oa_ref if mixer == "a" else ob_ref
                        o_ref[0, pl.ds(r0, CHUNK), cols] = jnp.where(low_half, outs[0], outs[1]).astype(BF16)
            return carry
        return chunk

    n_masked_blocks = pl.cdiv(max(PB - first_pos, 0), cpb * CHUNK)
    if hi_a < LA or hi_b < LB:
        lax.fori_loop(0, cpb, make_chunk(True), 0)
    else:
        @pl.when(i < n_masked_blocks)
        def _():
            lax.fori_loop(0, cpb, make_chunk(True), 0)

        @pl.when(i >= n_masked_blocks)
        def _():
            lax.fori_loop(0, cpb, make_chunk(False), 0)


def _attn(sinks, hq, kbp, kbc, vbp, vbc, kap, kac, vap, vac, bias_a, bias_b, *,
          cpb, first_pos, hi_a, hi_b, cols):
    nb, s, _ = hq.shape
    qb_rows = cpb * CHUNK
    nblk = s // qb_rows
    rb = qb_rows // PB if qb_rows >= PB else None
    ra = qb_rows // PA if qb_rows >= PA else None

    def prev_map(ratio, col):
        if ratio is None:
            return lambda b, i, *_: (b, 0, col)
        return lambda b, i, *_: (b, jnp.maximum(i * ratio - 1, 0), col)

    def cur_map(col):
        return lambda b, i, *_: (b, i, col)

    in_specs = [
        pl.BlockSpec((1, qb_rows, A_QW), cur_map(cols["qa"])),
        pl.BlockSpec((1, qb_rows, B_W), cur_map(cols["qb"])),
        pl.BlockSpec((1, PB, B_W), prev_map(rb, cols["kbp"])),
        pl.BlockSpec((1, qb_rows, B_W), cur_map(cols["kbc"])),
        pl.BlockSpec((1, PB, B_W), prev_map(rb, cols["vbp"])),
        pl.BlockSpec((1, qb_rows, B_W), cur_map(cols["vbc"])),
        pl.BlockSpec((1, PA, A_KVD), prev_map(ra, cols["kap"])),
        pl.BlockSpec((1, qb_rows, A_KVD), cur_map(cols["kac"])),
        pl.BlockSpec((1, PA, A_KVD), prev_map(ra, cols["vap"])),
        pl.BlockSpec((1, qb_rows, A_KVD), cur_map(cols["vac"])),
        pl.BlockSpec((A_HEADS, CHUNK, LA), lambda b, i, *_: (0, 0, 0)),
        pl.BlockSpec((B_HEADS, CHUNK, LB), lambda b, i, *_: (0, 0, 0)),
    ]
    out_specs = [
        pl.BlockSpec((1, qb_rows, A_QW), lambda b, i, *_: (b, i, 0)),
        pl.BlockSpec((1, qb_rows, B_W), lambda b, i, *_: (b, i, 0)),
    ]
    body = functools.partial(_attn_body, cpb=cpb, first_pos=first_pos, hi_a=hi_a, hi_b=hi_b)
    return pl.pallas_call(
        body,
        grid_spec=pltpu.PrefetchScalarGridSpec(
            num_scalar_prefetch=1,
            grid=(nb, nblk),
            in_specs=in_specs,
            out_specs=out_specs,
            scratch_shapes=[
                pltpu.VMEM((PB + qb_rows, B_W), BF16),
                pltpu.VMEM((PB + qb_rows, B_W), BF16),
                pltpu.VMEM((PA + qb_rows, A_KVD), BF16),
                pltpu.VMEM((PA + qb_rows, A_KVD), BF16),
            ],
        ),
        out_shape=[
            jax.ShapeDtypeStruct((nb, s, A_QW), BF16),
            jax.ShapeDtypeStruct((nb, s, B_W), BF16),
        ],
        compiler_params=_cparams(("parallel", "arbitrary"), 48),
        name="attn",
    )(sinks, hq, hq, kbp, kbc, vbp, vbc, kap, kac, vap, vac, bias_a, bias_b)


def _merge_body(oa_p, ob_p, ga_p, gb_p, x_p, oa_s, ob_s, ga_s, gb_s, x_s,
                woa_ref, wob_ref, wout_ref, fg_ref, wr_ref, br_ref, x2_ref, xn_ref, lg_ref, *,
                n_prompt_tiles):
    def tile(oa_ref, ob_ref, ga_ref, gb_ref, x_ref):
        ya = jnp.dot(oa_ref[...], woa_ref[...], preferred_element_type=F32)
        yb = jnp.dot(ob_ref[...], wob_ref[...], preferred_element_type=F32)
        z = ga_ref[...].astype(F32) * ya + gb_ref[...].astype(F32) * yb
        y = jnp.dot(z.astype(BF16), wout_ref[...], preferred_element_type=F32)
        x2 = x_ref[...] + y
        x2_ref[...] = x2
        ms = jnp.mean(x2 * x2, axis=-1, keepdims=True)
        xn = x2 * lax.rsqrt(ms + NORM_EPS) * fg_ref[...]
        xn_ref[...] = xn
        lg_ref[...] = jnp.dot(xn.astype(BF16), wr_ref[...], preferred_element_type=F32) + br_ref[...]

    i = pl.program_id(0)

    @pl.when(i < n_prompt_tiles)
    def _():
        tile(oa_p, ob_p, ga_p, gb_p, x_p)

    @pl.when(i >= n_prompt_tiles)
    def _():
        tile(oa_s, ob_s, ga_s, gb_s, x_s)


def _merge(prompt, sample, w_oa, w_ob, w_out, fg, wr, br):
    tm = MERGE_TM
    n_p, n_s = prompt[3].shape[0], sample[3].shape[0]
    npt, nst = n_p // tm, n_s // tm
    n_out = n_p + n_s
    const = lambda i: (0, 0)
    resident = functools.partial(pl.BlockSpec, index_map=const, pipeline_mode=pl.Buffered(1))

    def token_specs(row):
        return [
            pl.BlockSpec((tm, A_QW), lambda i: (row(i), 0)),
            pl.BlockSpec((tm, B_W), lambda i: (row(i), 0)),
            pl.BlockSpec((tm, D_MODEL), lambda i: (row(i), 2)),
            pl.BlockSpec((tm, D_MODEL), lambda i: (row(i), 3)),
            pl.BlockSpec((tm, D_MODEL), lambda i: (row(i), 0)),
        ]

    in_specs = (token_specs(lambda i: jnp.minimum(i, npt - 1))
                + token_specs(lambda i: jnp.clip(i - npt, 0, nst - 1))
                + [resident((A_QW, D_MODEL)), resident((B_W, D_MODEL)), resident((D_MODEL, D_MODEL)),
                   resident((1, D_MODEL)), resident((D_MODEL, LANES)), resident((1, LANES))])
    oa_p, ob_p, h_p, x_p = prompt
    oa_s, ob_s, h_s, x_s = sample
    return pl.pallas_call(
        functools.partial(_merge_body, n_prompt_tiles=npt),
        grid=(npt + nst,),
        in_specs=in_specs,
        out_specs=[
            pl.BlockSpec((tm, D_MODEL), lambda i: (i, 0)),
            pl.BlockSpec((tm, D_MODEL), lambda i: (i, 0)),
            pl.BlockSpec((tm, LANES), lambda i: (i, 0)),
        ],
        out_shape=[
            jax.ShapeDtypeStruct((n_out, D_MODEL), F32),
            jax.ShapeDtypeStruct((n_out, D_MODEL), F32),
            jax.ShapeDtypeStruct((n_out, LANES), F32),
        ],
        compiler_params=_cparams(("arbitrary",), 56),
        name="merge",
    )(oa_p, ob_p, h_p, h_p, x_p, oa_s, ob_s, h_s, h_s, x_s, w_oa, w_ob, w_out, fg, wr, br)


def _router_body(lg_ref, tri_ref, idx_ref, gate_ref, rank_ref, cnt_ref, carry):
    i = pl.program_id(0)

    @pl.when(i == 0)
    def _():
        carry[...] = jnp.zeros_like(carry)

    lane = lax.broadcasted_iota(jnp.int32, lg_ref.shape, 1)
    x = jnp.where(lane < N_EXPERTS, lg_ref[...], -jnp.inf)
    vals, hots = [], []
    idx_out = jnp.zeros(lg_ref.shape, jnp.int32)
    for k in range(TOP_K):
        m = jnp.max(x, axis=-1, keepdims=True)
        am = jnp.min(jnp.where(x == m, lane, LANES), axis=-1, keepdims=True)
        hot = lane == am
        vals.append(m)
        hots.append(hot)
        idx_out = jnp.where(lane == k, am, idx_out)
        x = jnp.where(hot, -jnp.inf, x)
    es = [jnp.exp(v - vals[0]) for v in vals]
    denom = functools.reduce(lambda a, b: a + b, es)
    sel = functools.reduce(jnp.logical_or, hots)
    sel_f = jnp.where(sel, 1.0, 0.0)
    before = jnp.dot(tri_ref[...], sel_f.astype(BF16), preferred_element_type=F32) + carry[...]
    gate_out = jnp.zeros(lg_ref.shape, F32)
    rank_out = jnp.zeros(lg_ref.shape, F32)
    for k in range(TOP_K):
        gate_out = jnp.where(lane == k, es[k] / denom, gate_out)
        rk = jnp.sum(jnp.where(hots[k], before, 0.0), axis=-1, keepdims=True)
        rank_out = jnp.where(lane == k, rk, rank_out)
    idx_ref[...] = idx_out
    gate_ref[...] = gate_out
    rank_ref[...] = rank_out.astype(jnp.int32)
    total = carry[...] + jnp.sum(sel_f, axis=0, keepdims=True)
    carry[...] = total
    cnt_ref[...] = total.astype(jnp.int32)


def _router(logits):
    n = logits.shape[0]
    tm = ROUTER_TM
    r = jnp.arange(tm)
    tri = (r[None, :] < r[:, None]).astype(BF16)
    tile = pl.BlockSpec((tm, LANES), lambda i: (i, 0))
    return pl.pallas_call(
        _router_body,
        grid=(n // tm,),
        in_specs=[tile, pl.BlockSpec((tm, tm), lambda i: (0, 0))],
        out_specs=[tile, tile, tile, pl.BlockSpec((1, LANES), lambda i: (0, 0))],
        out_shape=[
            jax.ShapeDtypeStruct((n, LANES), jnp.int32),
            jax.ShapeDtypeStruct((n, LANES), F32),
            jax.ShapeDtypeStruct((n, LANES), jnp.int32),
            jax.ShapeDtypeStruct((1, LANES), jnp.int32),
        ],
        scratch_shapes=[pltpu.VMEM((1, LANES), F32)],
        compiler_params=_cparams(("arbitrary",), 16),
        name="router",
    )(logits, tri)


def _row_gather(idx_ref, n, src_hbm, dst, sem):
    def body(r, c):
        t = idx_ref[0, 0, r]
        pltpu.make_async_copy(src_hbm.at[pl.ds(t, 1), :], dst.at[pl.ds(r, 1), :], sem).start()
        return c
    lax.fori_loop(0, n, body, 0)


def _expert_body(wblk_ref, wexp_ref, wlo_ref, whi_ref, wfirst_ref, nv_ref, tokc_ref, tokn_ref, xn_hbm,
                 w1g_ref, w1u_ref, b1g_ref, b1u_ref, w2_ref, b2_ref,
                 out_ref, xg, xb, act_s, sem):
    i = pl.program_id(0)
    f = pl.program_id(1)
    nf = MOE_NF
    nv = nv_ref[0]
    slot = i % 2

    def rows_landed(s):
        return pltpu.make_async_copy(xn_hbm.at[pl.ds(0, MOE_TM), :], xg.at[s], sem.at[s])

    @pl.when(jnp.logical_and(i == nv, f == 0))
    def _():
        rows_landed(slot).wait()

    @pl.when(jnp.logical_and(i < nv, f == 0))
    def _():
        @pl.when(i == 0)
        def _():
            _row_gather(tokc_ref, MOE_TM, xn_hbm, xg.at[0], sem.at[0])

        rows_landed(slot).wait()
        xb[...] = xg[slot].astype(BF16)

    @pl.when(i < nv)
    def _():
        base = f * MOE_SHARE
        for r in range(MOE_SHARE):
            t = tokn_ref[0, 0, base + r]
            pltpu.make_async_copy(xn_hbm.at[pl.ds(t, 1), :], xg.at[1 - slot, pl.ds(base + r, 1), :],
                                  sem.at[1 - slot]).start()
        x = xb[...]
        for c in range(MOE_TF // MXU_COLS):
            cs = slice(c * MXU_COLS, (c + 1) * MXU_COLS)
            hg = jnp.dot(x, w1g_ref[0, :, cs], preferred_element_type=F32) + b1g_ref[0, :, cs]
            hu = jnp.dot(x, w1u_ref[0, :, cs], preferred_element_type=F32) + b1u_ref[0, :, cs]
            hg = jnp.minimum(hg, SWIGLU_LIMIT)
            hu = jnp.clip(hu, -SWIGLU_LIMIT, SWIGLU_LIMIT)
            sig = 0.5 * jnp.tanh((0.5 * SWIGLU_ALPHA) * hg) + 0.5
            act_s[f, :, cs] = (hg * sig * (hu + 1.0)).astype(BF16)

    last = jnp.logical_and(i < nv, f == nf - 1)

    def ffn_out():
        act = jnp.concatenate([act_s[k] for k in range(nf)], axis=-1)
        return jnp.dot(act, w2_ref[0], preferred_element_type=F32) + b2_ref[0]

    @pl.when(jnp.logical_and(last, wfirst_ref[i] == 1))
    def _():
        out_ref[...] = ffn_out()

    @pl.when(jnp.logical_and(last, wfirst_ref[i] == 0))
    def _():
        row = lax.broadcasted_iota(jnp.int32, (MOE_TM, 1), 0)
        mine = jnp.logical_and(row >= wlo_ref[i], row < whi_ref[i])
        out_ref[...] = jnp.where(mine, ffn_out(), out_ref[...])


def _experts(work, sorted_tok3, xn, w1, b1, w2, b2):
    w_blk, w_exp, w_lo, w_hi, w_first, n_work = work
    n_items = w_blk.shape[0]
    nblk = sorted_tok3.shape[0]
    nf = MOE_NF

    def fz(i, f, nv):
        return jnp.where(i < nv[0], f, nf - 1)

    def tok_map(shift):
        return lambda i, f, wb, *_: (wb[jnp.minimum(i + shift, n_items - 1)], 0, 0)

    in_specs = [
        pl.BlockSpec((1, 1, MOE_TM), tok_map(0), memory_space=pltpu.SMEM),
        pl.BlockSpec((1, 1, MOE_TM), tok_map(1), memory_space=pltpu.SMEM),
        pl.BlockSpec(memory_space=pl.ANY),
        pl.BlockSpec((1, D_MODEL, MOE_TF), lambda i, f, wb, we, lo, hi, fi, nv: (we[i], 0, fz(i, f, nv))),
        pl.BlockSpec((1, D_MODEL, MOE_TF), lambda i, f, wb, we, lo, hi, fi, nv: (we[i], 0, nf + fz(i, f, nv))),
        pl.BlockSpec((1, 1, MOE_TF), lambda i, f, wb, we, lo, hi, fi, nv: (we[i], 0, fz(i, f, nv))),
        pl.BlockSpec((1, 1, MOE_TF), lambda i, f, wb, we, lo, hi, fi, nv: (we[i], 0, nf + fz(i, f, nv))),
        pl.BlockSpec((1, D_FF, D_MODEL), lambda i, f, wb, we, *_: (we[i], 0, 0)),
        pl.BlockSpec((1, 1, D_MODEL), lambda i, f, wb, we, *_: (we[i], 0, 0)),
    ]
    return pl.pallas_call(
        _expert_body,
        grid_spec=pltpu.PrefetchScalarGridSpec(
            num_scalar_prefetch=6,
            grid=(n_items, nf),
            in_specs=in_specs,
            out_specs=pl.BlockSpec((MOE_TM, D_MODEL), lambda i, f, wb, *_: (wb[i], 0)),
            scratch_shapes=[
                pltpu.VMEM((2, MOE_TM, D_MODEL), F32),
                pltpu.VMEM((MOE_TM, D_MODEL), BF16),
                pltpu.VMEM((nf, MOE_TM, MOE_TF), BF16),
                pltpu.SemaphoreType.DMA((2,)),
            ],
        ),
        out_shape=jax.ShapeDtypeStruct((nblk * MOE_TM, D_MODEL), F32),
        compiler_params=_cparams(("arbitrary", "arbitrary"), 56),
        name="experts",
    )(w_blk, w_exp, w_lo, w_hi, w_first, n_work, sorted_tok3, sorted_tok3, xn, w1, w1, b1, b1, w2, b2)


def _combine_body(posc_ref, posn_ref, x2_ref, gate_ref, ys_hbm, outp_ref, outs_ref, buf, sem, *,
                  n_prompt_tiles):
    i = pl.program_id(0)
    n = pl.num_programs(0)
    slot = i % 2
    nrow = TOP_K * COMB_TT

    def issue(pref, s):
        def body(r, c):
            for k in range(TOP_K):
                p = pref[0, 0, TOP_K * r + k]
                pltpu.make_async_copy(ys_hbm.at[pl.ds(p, 1), :],
                                      buf.at[s, pl.ds(k * COMB_TT + r, 1), :], sem.at[s]).start()
            return c
        lax.fori_loop(0, COMB_TT, body, 0)

    @pl.when(i == 0)
    def _():
        issue(posc_ref, 0)

    pltpu.make_async_copy(ys_hbm.at[pl.ds(0, nrow), :], buf.at[slot], sem.at[slot]).wait()

    @pl.when(i + 1 < n)
    def _():
        issue(posn_ref, 1 - slot)

    y = x2_ref[...]
    g = gate_ref[...]
    for k in range(TOP_K):
        y = y + g[:, k:k + 1] * buf[slot, k * COMB_TT:(k + 1) * COMB_TT, :]

    @pl.when(i < n_prompt_tiles)
    def _():
        outp_ref[...] = y

    @pl.when(i >= n_prompt_tiles)
    def _():
        outs_ref[...] = y


def _combine(pos3, x2, gates, ys, n_prompt):
    n_tok = x2.shape[0]
    nt = n_tok // COMB_TT
    npt = n_prompt // COMB_TT
    assert n_tok - n_prompt == COMB_TT
    body = functools.partial(_combine_body, n_prompt_tiles=npt)
    return pl.pallas_call(
        body,
        grid=(nt,),
        in_specs=[
            pl.BlockSpec((1, 1, TOP_K * COMB_TT), lambda i: (i, 0, 0), memory_space=pltpu.SMEM),
            pl.BlockSpec((1, 1, TOP_K * COMB_TT), lambda i: (jnp.minimum(i + 1, nt - 1), 0, 0),
                         memory_space=pltpu.SMEM),
            pl.BlockSpec((COMB_TT, D_MODEL), lambda i: (i, 0)),
            pl.BlockSpec((COMB_TT, LANES), lambda i: (i, 0)),
            pl.BlockSpec(memory_space=pl.ANY),
        ],
        out_specs=[
            pl.BlockSpec((COMB_TT, D_MODEL), lambda i: (jnp.minimum(i, npt - 1), 0)),
            pl.BlockSpec((COMB_TT, D_MODEL), lambda i: (0, 0)),
        ],
        out_shape=[
            jax.ShapeDtypeStruct((n_prompt, D_MODEL), F32),
            jax.ShapeDtypeStruct((COMB_TT, D_MODEL), F32),
        ],
        scratch_shapes=[
            pltpu.VMEM((2, TOP_K * COMB_TT, D_MODEL), F32),
            pltpu.SemaphoreType.DMA((2,)),
        ],
        compiler_params=_cparams(("arbitrary",), 40),
        name="combine",
    )(pos3, pos3, x2, gates, ys)


def _t5_bucket(rel):
    nb = T5_BUCKETS // 2
    max_exact = nb // 2
    ret = (rel > 0).astype(jnp.int32) * nb
    n = jnp.abs(rel)
    large = max_exact + (jnp.log(jnp.maximum(n, 1).astype(F32) / max_exact)
                         / math.log(T5_MAX_DIST / max_exact) * (nb - max_exact)).astype(jnp.int32)
    large = jnp.minimum(large, nb - 1)
    return ret + jnp.where(n < max_exact, n, large)


def _dup_heads(t):
    lead = t.shape[:-1]
    t = t.reshape(*lead, A_KV_HEADS, 1, HEAD_DIM)
    return jnp.broadcast_to(t, (*lead, A_KV_HEADS, 2, HEAD_DIM)).reshape(*lead, A_KVD)


def _work_items(top_i, rank, counts, n_tok):
    nk = n_tok * TOP_K
    assert nk % MOE_TM == 0
    nblk = nk // MOE_TM
    n_items = nblk + N_EXPERTS
    end = jnp.cumsum(counts)
    start = end - counts
    first_blk = start // MOE_TM
    n_it = jnp.where(counts > 0, (end - 1) // MOE_TM - first_blk + 1, 0)
    it_end = jnp.cumsum(n_it)
    it_start = it_end - n_it
    n_work = it_end[-1]
    experts = jnp.arange(N_EXPERTS, dtype=jnp.int32)
    table = lambda tab, idx: jnp.sum(jnp.where(idx[..., None] == experts, tab, 0), axis=-1)
    w = jnp.arange(n_items, dtype=jnp.int32)
    wc = jnp.minimum(w, n_work - 1)
    w_exp = jnp.minimum(jnp.sum((it_end[None, :] <= wc[:, None]).astype(jnp.int32), axis=1), N_EXPERTS - 1)
    w_blk = table(first_blk, w_exp) + wc - table(it_start, w_exp)
    w_lo = jnp.clip(table(start, w_exp) - w_blk * MOE_TM, 0, MOE_TM)
    w_hi = jnp.clip(table(end, w_exp) - w_blk * MOE_TM, 0, MOE_TM)
    w_first = jnp.concatenate([jnp.ones((1,), jnp.int32), (w_blk[1:] != w_blk[:-1]).astype(jnp.int32)])
    pos = table(start, top_i) + rank
    flat_tok = jnp.arange(nk, dtype=jnp.int32) // TOP_K
    _, sorted_tok = lax.sort_key_val(top_i.reshape(-1), flat_tok, is_stable=True)
    i32 = lambda t: t.astype(jnp.int32)
    work = (i32(w_blk), i32(w_exp), i32(w_lo), i32(w_hi), w_first, i32(n_work).reshape(1))
    return sorted_tok.reshape(nblk, 1, MOE_TM), work, pos.reshape(n_tok // COMB_TT, 1, TOP_K * COMB_TT)


def _band_bias(vals, keys):
    vt = vals.T
    return jnp.stack([vt[:, CHUNK - 1 - i:CHUNK - 1 - i + keys] for i in range(CHUNK)], axis=1).astype(F32)


def kernel(x_prompt, x_sample, cache_a_k, cache_a_v, cache_b_k, cache_b_v, attn_norm, w_in,
           a_q_norm, a_k_norm, b_q_norm, b_k_norm, a_sinks, t5_table, b_rel_table, w_oa, w_ob,
           w_out, ffn_norm, router_w, router_b, w1, b1, w2, b2):
    batch, seq, _ = x_prompt.shape
    dec_b, dec_s, _ = x_sample.shape
    assert attn_norm.shape[0] == 1, "single layer"
    assert dec_s <= CHUNK and PAST_LEN % CHUNK == 0 and PAST_LEN >= PB
    n_prompt = batch * seq
    n_sample = dec_b * dec_s

    wi = w_in[0]
    c = [0]
    for wdt in (A_QW, A_KVW, A_KVW, B_W, B_W, B_W, D_MODEL, D_MODEL):
        c.append(c[-1] + wdt)
    w_qa, w_ka, w_va, w_qb, w_kb, w_vb, w_ga, w_gb = [wi[:, c[k]:c[k + 1]] for k in range(8)]
    w_perm = jnp.concatenate([w_qa, w_qb, w_kb, w_vb, w_ga, w_gb, _dup_heads(w_ka), _dup_heads(w_va)],
                             axis=1).astype(BF16)
    scale = HEAD_DIM ** -0.5
    ones = lambda n: jnp.ones((n,), F32)
    cg = jnp.concatenate([
        jnp.tile(a_q_norm[0] * scale, A_HEADS), jnp.tile(b_q_norm[0] * scale, B_HEADS),
        jnp.tile(b_k_norm[0], B_HEADS), ones(B_W), ones(2 * D_MODEL),
        jnp.tile(a_k_norm[0], 2 * A_KV_HEADS), ones(A_KVD)]).reshape(1, PROJ_W).astype(F32)
    hd = jnp.arange(MXU_COLS) // HEAD_DIM
    gm = (hd[:, None] == hd[None, :]).astype(BF16)
    g_attn = attn_norm[0].reshape(1, D_MODEL)
    rel_a = jnp.arange(LA + CHUNK - 1) - PA - (CHUNK - 1)
    rel_b = jnp.arange(LB + CHUNK - 1) - PB - (CHUNK - 1)
    bias_a = _band_bias(t5_table[_t5_bucket(rel_a)], LA)
    bias_b = _band_bias(b_rel_table[0][jnp.clip(rel_b, -B_REL_CLIP, CHUNK - 1) + B_REL_CLIP], LB)
    sinks = a_sinks[0].astype(F32)
    woa, wob, wout = w_oa[0].astype(BF16), w_ob[0].astype(BF16), w_out[0].astype(BF16)
    fg = ffn_norm[0].reshape(1, D_MODEL)
    wr = jnp.pad(router_w[0], ((0, 0), (0, LANES - N_EXPERTS))).astype(BF16)
    br = jnp.pad(router_b[0], (0, LANES - N_EXPERTS)).reshape(1, LANES)
    w1b, w2b = w1[0].astype(BF16), w2[0].astype(BF16)
    b1r = b1[0].reshape(N_EXPERTS, 1, 2 * D_FF)
    b2r = b2[0].reshape(N_EXPERTS, 1, D_MODEL)

    xp2 = x_prompt.reshape(n_prompt, D_MODEL)
    h_p, kv_p = _proj(xp2, g_attn, w_perm, cg, gm, tm=1024)
    h_p3 = h_p.reshape(batch, seq, PROJ_W)
    cols_p = dict(qa=0, qb=1, kbp=2, kbc=2, vbp=3, vbc=3, kap=16, kac=16, vap=17, vac=17)
    oa_p, ob_p = _attn(sinks, h_p3, h_p3, h_p3, h_p3, h_p3, h_p3, h_p3, h_p3, h_p3, bias_a, bias_b,
                       cpb=8, first_pos=0, hi_a=LA, hi_b=LB, cols=cols_p)
    n_tok = n_prompt + n_sample

    xs_pad = jnp.pad(x_sample, ((0, 0), (0, CHUNK - dec_s), (0, 0))).reshape(dec_b * CHUNK, D_MODEL)
    h_s, kv_s = _proj(xs_pad, g_attn, w_perm, cg, gm, tm=dec_b * CHUNK)
    h_s3 = h_s.reshape(dec_b, CHUNK, PROJ_W)
    ckb = cache_b_k[0].reshape(dec_b, PB, B_W).astype(BF16)
    cvb = cache_b_v[0].reshape(dec_b, PB, B_W).astype(BF16)
    cka = _dup_heads(cache_a_k[0].reshape(dec_b, PA, A_KVW)).astype(BF16)
    cva = _dup_heads(cache_a_v[0].reshape(dec_b, PA, A_KVW)).astype(BF16)
    cols_s = dict(qa=0, qb=1, kbp=0, kbc=2, vbp=0, vbc=3, kap=0, kac=16, vap=0, vac=17)
    oa_s, ob_s = _attn(sinks, h_s3, ckb, h_s3, cvb, h_s3, cka, h_s3, cva, h_s3, bias_a, bias_b,
                       cpb=1, first_pos=PAST_LEN, hi_a=PA + dec_s, hi_b=PB + dec_s, cols=cols_s)
    keep_rows = lambda t: t[:, :dec_s].reshape(n_sample, t.shape[-1])
    x2, xn, logits = _merge(
        (oa_p.reshape(n_prompt, A_QW), ob_p.reshape(n_prompt, B_W), h_p, xp2),
        (keep_rows(oa_s), keep_rows(ob_s), keep_rows(h_s3), x_sample.reshape(n_sample, D_MODEL)),
        woa, wob, wout, fg, wr, br)

    top_i, gates, rank, counts = _router(logits)
    sorted_tok3, work, pos3 = _work_items(top_i[:, :TOP_K], rank[:, :TOP_K], counts[0, :N_EXPERTS], n_tok)
    ys = _experts(work, sorted_tok3, xn, w1b, b1r, w2b, b2r)
    y_p, y_s = _combine(pos3, x2, gates, ys, n_prompt)

    kv_p3 = kv_p.reshape(batch, seq, KV32_W)
    kv_s3 = kv_s.reshape(dec_b, CHUNK, KV32_W)
    undup = lambda t: t.reshape(*t.shape[:-1], A_KV_HEADS, 2, HEAD_DIM)[..., 0, :]
    heads_b = lambda t: t.reshape(*t.shape[:-1], B_HEADS, HEAD_DIM)
    o_kb, o_vb, o_ka, o_va = 0, B_W, 2 * B_W, 2 * B_W + A_KVD
    p_bk = heads_b(kv_p3[:, seq - PB:, o_kb:o_kb + B_W])[None]
    p_bv = heads_b(kv_p3[:, seq - PB:, o_vb:o_vb + B_W])[None]
    p_ak = undup(kv_p3[:, seq - PA:, o_ka:o_ka + A_KVD])[None]
    p_av = undup(kv_p3[:, seq - PA:, o_va:o_va + A_KVD])[None]
    s_bk = heads_b(kv_s3[:, :dec_s, o_kb:o_kb + B_W])[None]
    s_bv = heads_b(kv_s3[:, :dec_s, o_vb:o_vb + B_W])[None]
    s_ak = undup(kv_s3[:, :dec_s, o_ka:o_ka + A_KVD])[None]
    s_av = undup(kv_s3[:, :dec_s, o_va:o_va + A_KVD])[None]
    return (y_p.reshape(batch, seq, D_MODEL), y_s.reshape(dec_b, dec_s, D_MODEL),
            p_ak, p_av, p_bk, p_bv, s_ak, s_av, s_bk, s_bv)
```

```python
import functools
import math

import jax
import jax.numpy as jnp
from jax import lax
from jax.experimental import pallas as pl
from jax.experimental.pallas import tpu as pltpu

F32 = jnp.float32
BF16 = jnp.bfloat16

D_MODEL = 2048
CHUNK = 64
HEAD_DIM = 64
A_HEADS = 16
A_KV_HEADS = 4
A_PAST_CHUNKS = 2
B_HEADS = 16
B_PAST_CHUNKS = 8
B_REL_CLIP = 256
T5_BUCKETS = 32
T5_MAX_DIST = (A_PAST_CHUNKS + 1) * CHUNK
N_EXPERTS = 32
TOP_K = 4
D_FF = D_MODEL
SWIGLU_ALPHA = 1.702
SWIGLU_LIMIT = 7.0
NORM_EPS = 1e-6
NEG_INF = -1e30
PAST_LEN = 1024

A_QW = A_HEADS * HEAD_DIM
A_KVW = A_KV_HEADS * HEAD_DIM
B_W = B_HEADS * HEAD_DIM
PA = A_PAST_CHUNKS * CHUNK
PB = B_PAST_CHUNKS * CHUNK
LA = PA + CHUNK
LB = PB + CHUNK

LANES = 128
MXU_COLS = 256

A_KVD = 2 * A_KVW
PROJ_W = A_QW + 3 * B_W + 2 * D_MODEL + 2 * A_KVD
PROJ_TN = 512
KV32_W = 2 * B_W + 2 * A_KVD
_NORM_TILES = (0, 1, 2, 3, 4, 5, 16)
_SIGMOID_LO, _SIGMOID_HI = 8, 16

MOE_TM = 512
MOE_TF = 512
MOE_NF = D_FF // MOE_TF
MOE_SHARE = MOE_TM // MOE_NF
COMB_TT = 256
ROUTER_TM = 256
MERGE_TM = 256
ATTN_AHEAD = 6


def _cparams(sem, vmem_mb):
    return pltpu.CompilerParams(dimension_semantics=sem, vmem_limit_bytes=vmem_mb * 1024 * 1024)


def _proj_body(x_ref, g_ref, w_ref, cg_ref, gm_ref, h_ref, kv_ref, xn_s):
    j = pl.program_id(1)

    @pl.when(j == 0)
    def _():
        x = x_ref[...]
        ms = jnp.mean(x * x, axis=-1, keepdims=True)
        xn_s[...] = (x * lax.rsqrt(ms + NORM_EPS) * g_ref[...]).astype(BF16)

    is_norm = functools.reduce(jnp.logical_or, [j == t for t in _NORM_TILES])
    is_sig = jnp.logical_and(j >= _SIGMOID_LO, j < _SIGMOID_HI)

    def strips(epilogue, write_kv, dots_first=False):
        col = [slice(c * MXU_COLS, (c + 1) * MXU_COLS) for c in range(PROJ_TN // MXU_COLS)]
        mm = lambda cs: jnp.dot(xn_s[...], w_ref[:, cs], preferred_element_type=F32)
        accs = [mm(cs) for cs in col] if dots_first else None
        for c, cs in enumerate(col):
            y = epilogue(accs[c] if dots_first else mm(cs), cs)
            h_ref[:, cs] = y.astype(BF16)
            if write_kv:
                kv_ref[:, cs] = y

    def norm(a, cs):
        ss = jnp.dot((a * a).astype(BF16), gm_ref[...], preferred_element_type=F32)
        return a * lax.rsqrt(ss * (1.0 / HEAD_DIM) + NORM_EPS) * cg_ref[:, cs]

    @pl.when(is_norm)
    def _():
        strips(norm, True, dots_first=True)

    @pl.when(is_sig)
    def _():
        strips(lambda a, cs: 0.5 * jnp.tanh(0.5 * a) + 0.5, False)

    @pl.when(jnp.logical_not(jnp.logical_or(is_norm, is_sig)))
    def _():
        strips(lambda a, cs: a, True)


def _kv_tile(j):
    return jnp.clip(j - 4, 0, 3) + (j >= 16).astype(jnp.int32) + (j >= 17).astype(jnp.int32)


def _proj(x2d, g, w_perm, cg, gm, tm):
    n = x2d.shape[0]
    grid = (n // tm, PROJ_W // PROJ_TN)
    return pl.pallas_call(
        _proj_body,
        grid=grid,
        in_specs=[
            pl.BlockSpec((tm, D_MODEL), lambda i, j: (i, 0)),
            pl.BlockSpec((1, D_MODEL), lambda i, j: (0, 0)),
            pl.BlockSpec((D_MODEL, PROJ_TN), lambda i, j: (0, j)),
            pl.BlockSpec((1, PROJ_TN), lambda i, j: (0, j)),
            pl.BlockSpec((MXU_COLS, MXU_COLS), lambda i, j: (0, 0)),
        ],
        out_specs=[
            pl.BlockSpec((tm, PROJ_TN), lambda i, j: (i, j)),
            pl.BlockSpec((tm, PROJ_TN), lambda i, j: (i, _kv_tile(j))),
        ],
        out_shape=[
            jax.ShapeDtypeStruct((n, PROJ_W), BF16),
            jax.ShapeDtypeStruct((n, KV32_W), F32),
        ],
        scratch_shapes=[pltpu.VMEM((tm, D_MODEL), BF16)],
        compiler_params=_cparams(("parallel", "arbitrary"), 48),
        name="proj",
    )(x2d, g, w_perm, cg, gm)


def _attn_body(sink_ref, qa_ref, qb_ref, kbp_ref, kbc_ref, vbp_ref, vbc_ref,
               kap_ref, kac_ref, vap_ref, vac_ref, ba_ref, bb_ref,
               oa_ref, ob_ref, kb_s, vb_s, ka_s, va_s, *, cpb, first_pos, hi_a, hi_b):
    i = pl.program_id(1)
    qb_rows = cpb * CHUNK
    kb_s[0:PB, :] = kbp_ref[0]
    kb_s[PB:PB + qb_rows, :] = kbc_ref[0]
    vb_s[0:PB, :] = vbp_ref[0]
    vb_s[PB:PB + qb_rows, :] = vbc_ref[0]
    ka_s[0:PA, :] = kap_ref[0]
    ka_s[PA:PA + qb_rows, :] = kac_ref[0]
    va_s[0:PA, :] = vap_ref[0]
    va_s[PA:PA + qb_rows, :] = vac_ref[0]

    lane_a = lax.broadcasted_iota(jnp.int32, (1, LA), 1)
    lane_b = lax.broadcasted_iota(jnp.int32, (1, LB), 1)
    low_half = lax.broadcasted_iota(jnp.int32, (1, LANES), 1) < HEAD_DIM
    nt = (((1,), (1,)), ((), ()))

    tasks = [("a", h) for h in range(A_HEADS)] + [("b", h) for h in range(B_HEADS)]

    def scores(task, r0, valid_a, valid_b):
        mixer, h = task
        p, half = h // 2, h % 2
        cols = slice(p * LANES, (p + 1) * LANES)
        if mixer == "a":
            kcols = slice((p // 2) * LANES, (p // 2 + 1) * LANES)
            q_p, k_p, bias, valid = qa_ref[0, pl.ds(r0, CHUNK), cols], ka_s[pl.ds(r0, LA), kcols], ba_ref[h], valid_a
        else:
            q_p, k_p, bias, valid = qb_ref[0, pl.ds(r0, CHUNK), cols], kb_s[pl.ds(r0, LB), cols], bb_ref[h], valid_b
        qm = jnp.where(low_half if half == 0 else jnp.logical_not(low_half), q_p, jnp.zeros_like(q_p))
        s = lax.dot_general(qm, k_p, nt, preferred_element_type=F32) + bias
        if valid is not None:
            s = jnp.where(valid, s, NEG_INF)
        return s

    def attend(task, s, r0):
        mixer, h = task
        p = h // 2
        if mixer == "a":
            kcols = slice((p // 2) * LANES, (p // 2 + 1) * LANES)
            v_p, sink = va_s[pl.ds(r0, LA), kcols], sink_ref[h]
        else:
            v_p, sink = vb_s[pl.ds(r0, LB), slice(p * LANES, (p + 1) * LANES)], None
        m = jnp.max(s, axis=-1, keepdims=True)
        if sink is not None:
            m = jnp.maximum(m, sink)
        e = jnp.exp(s - m)
        l = jnp.sum(e, axis=-1, keepdims=True)
        if sink is not None:
            l = l + jnp.exp(sink - m)
        o = jnp.dot(e.astype(BF16), v_p, preferred_element_type=F32)
        return o / l

    def make_chunk(masked):
        def chunk(jc, carry):
            r0 = pl.multiple_of(jc * CHUNK, CHUNK)
            valid_a = valid_b = None
            if masked:
                start = first_pos + (i * cpb + jc) * CHUNK
                valid_a = jnp.logical_and(lane_a >= jnp.maximum(PA - start, 0), lane_a < hi_a)
                valid_b = jnp.logical_and(lane_b >= jnp.maximum(PB - start, 0), lane_b < hi_b)
            pending, outs = {}, {}
            for t in range(len(tasks) + ATTN_AHEAD):
                if t < len(tasks):
                    pending[t] = scores(tasks[t], r0, valid_a, valid_b)
                d = t - ATTN_AHEAD
                if d >= 0:
                    mixer, h = tasks[d]
                    outs[h % 2] = attend(tasks[d], pending.pop(d), r0)
                    if h % 2 == 1:
                        cols = slice((h // 2) * LANES, (h // 2 + 1) * LANES)
                        o_ref = oa_ref if mixer == "a" else ob_ref
                        o_ref[0, pl.ds(r0, CHUNK), cols] = jnp.where(low_half, outs[0], outs[1]).astype(BF16)
            return carry
        return chunk

    n_masked_blocks = pl.cdiv(max(PB - first_pos, 0), cpb * CHUNK)
    if hi_a < LA or hi_b < LB:
        lax.fori_loop(0, cpb, make_chunk(True), 0)
    else:
        @pl.when(i < n_masked_blocks)
        def _():
            lax.fori_loop(0, cpb, make_chunk(True), 0)

        @pl.when(i >= n_masked_blocks)
        def _():
            lax.fori_loop(0, cpb, make_chunk(False), 0)


def _attn(sinks, hq, kbp, kbc, vbp, vbc, kap, kac, vap, vac, bias_a, bias_b, *,
          cpb, first_pos, hi_a, hi_b, cols):
    nb, s, _ = hq.shape
    qb_rows = cpb * CHUNK
    nblk = s // qb_rows
    rb = qb_rows // PB if qb_rows >= PB else None
    ra = qb_rows // PA if qb_rows >= PA else None

    def prev_map(ratio, col):
        if ratio is None:
            return lambda b, i, *_: (b, 0, col)
        return lambda b, i, *_: (b, jnp.maximum(i * ratio - 1, 0), col)

    def cur_map(col):
        return lambda b, i, *_: (b, i, col)

    in_specs = [
        pl.BlockSpec((1, qb_rows, A_QW), cur_map(cols["qa"])),
        pl.BlockSpec((1, qb_rows, B_W), cur_map(cols["qb"])),
        pl.BlockSpec((1, PB, B_W), prev_map(rb, cols["kbp"])),
        pl.BlockSpec((1, qb_rows, B_W), cur_map(cols["kbc"])),
        pl.BlockSpec((1, PB, B_W), prev_map(rb, cols["vbp"])),
        pl.BlockSpec((1, qb_rows, B_W), cur_map(cols["vbc"])),
        pl.BlockSpec((1, PA, A_KVD), prev_map(ra, cols["kap"])),
        pl.BlockSpec((1, qb_rows, A_KVD), cur_map(cols["kac"])),
        pl.BlockSpec((1, PA, A_KVD), prev_map(ra, cols["vap"])),
        pl.BlockSpec((1, qb_rows, A_KVD), cur_map(cols["vac"])),
        pl.BlockSpec((A_HEADS, CHUNK, LA), lambda b, i, *_: (0, 0, 0)),
        pl.BlockSpec((B_HEADS, CHUNK, LB), lambda b, i, *_: (0, 0, 0)),
    ]
    out_specs = [
        pl.BlockSpec((1, qb_rows, A_QW), lambda b, i, *_: (b, i, 0)),
        pl.BlockSpec((1, qb_rows, B_W), lambda b, i, *_: (b, i, 0)),
    ]
    body = functools.partial(_attn_body, cpb=cpb, first_pos=first_pos, hi_a=hi_a, hi_b=hi_b)
    return pl.pallas_call(
        body,
        grid_spec=pltpu.PrefetchScalarGridSpec(
            num_scalar_prefetch=1,
            grid=(nb, nblk),
            in_specs=in_specs,
            out_specs=out_specs,
            scratch_shapes=[
                pltpu.VMEM((PB + qb_rows, B_W), BF16),
                pltpu.VMEM((PB + qb_rows, B_W), BF16),
                pltpu.VMEM((PA + qb_rows, A_KVD), BF16),
                pltpu.VMEM((PA + qb_rows, A_KVD), BF16),
            ],
        ),
        out_shape=[
            jax.ShapeDtypeStruct((nb, s, A_QW), BF16),
            jax.ShapeDtypeStruct((nb, s, B_W), BF16),
        ],
        compiler_params=_cparams(("parallel", "arbitrary"), 48),
        name="attn",
    )(sinks, hq, hq, kbp, kbc, vbp, vbc, kap, kac, vap, vac, bias_a, bias_b)


def _merge_body(oa_p, ob_p, ga_p, gb_p, x_p, oa_s, ob_s, ga_s, gb_s, x_s,
                woa_ref, wob_ref, wout_ref, fg_ref, wr_ref, br_ref, x2_ref, xn_ref, lg_ref, *,
                n_prompt_tiles):
    def tile(oa_ref, ob_ref, ga_ref, gb_ref, x_ref):
        ya = jnp.dot(oa_ref[...], woa_ref[...], preferred_element_type=F32)
        yb = jnp.dot(ob_ref[...], wob_ref[...], preferred_element_type=F32)
        z = ga_ref[...].astype(F32) * ya + gb_ref[...].astype(F32) * yb
        y = jnp.dot(z.astype(BF16), wout_ref[...], preferred_element_type=F32)
        x2 = x_ref[...] + y
        x2_ref[...] = x2
        ms = jnp.mean(x2 * x2, axis=-1, keepdims=True)
        xn = x2 * lax.rsqrt(ms + NORM_EPS) * fg_ref[...]
        xn_ref[...] = xn
        lg_ref[...] = jnp.dot(xn.astype(BF16), wr_ref[...], preferred_element_type=F32) + br_ref[...]

    i = pl.program_id(0)

    @pl.when(i < n_prompt_tiles)
    def _():
        tile(oa_p, ob_p, ga_p, gb_p, x_p)

    @pl.when(i >= n_prompt_tiles)
    def _():
        tile(oa_s, ob_s, ga_s, gb_s, x_s)


def _merge(prompt, sample, w_oa, w_ob, w_out, fg, wr, br):
    tm = MERGE_TM
    n_p, n_s = prompt[3].shape[0], sample[3].shape[0]
    npt, nst = n_p // tm, n_s // tm
    n_out = n_p + n_s
    const = lambda i: (0, 0)
    resident = functools.partial(pl.BlockSpec, index_map=const, pipeline_mode=pl.Buffered(1))

    def token_specs(row):
        return [
            pl.BlockSpec((tm, A_QW), lambda i: (row(i), 0)),
            pl.BlockSpec((tm, B_W), lambda i: (row(i), 0)),
            pl.BlockSpec((tm, D_MODEL), lambda i: (row(i), 2)),
            pl.BlockSpec((tm, D_MODEL), lambda i: (row(i), 3)),
            pl.BlockSpec((tm, D_MODEL), lambda i: (row(i), 0)),
        ]

    in_specs = (token_specs(lambda i: jnp.minimum(i, npt - 1))
                + token_specs(lambda i: jnp.clip(i - npt, 0, nst - 1))
                + [resident((A_QW, D_MODEL)), resident((B_W, D_MODEL)), resident((D_MODEL, D_MODEL)),
                   resident((1, D_MODEL)), resident((D_MODEL, LANES)), resident((1, LANES))])
    oa_p, ob_p, h_p, x_p = prompt
    oa_s, ob_s, h_s, x_s = sample
    return pl.pallas_call(
        functools.partial(_merge_body, n_prompt_tiles=npt),
        grid=(npt + nst,),
        in_specs=in_specs,
        out_specs=[
            pl.BlockSpec((tm, D_MODEL), lambda i: (i, 0)),
            pl.BlockSpec((tm, D_MODEL), lambda i: (i, 0)),
            pl.BlockSpec((tm, LANES), lambda i: (i, 0)),
        ],
        out_shape=[
            jax.ShapeDtypeStruct((n_out, D_MODEL), F32),
            jax.ShapeDtypeStruct((n_out, D_MODEL), F32),
            jax.ShapeDtypeStruct((n_out, LANES), F32),
        ],
        compiler_params=_cparams(("arbitrary",), 56),
        name="merge",
    )(oa_p, ob_p, h_p, h_p, x_p, oa_s, ob_s, h_s, h_s, x_s, w_oa, w_ob, w_out, fg, wr, br)


def _router_body(lg_ref, tri_ref, idx_ref, gate_ref, rank_ref, cnt_ref, carry):
    i = pl.program_id(0)

    @pl.when(i == 0)
    def _():
        carry[...] = jnp.zeros_like(carry)

    lane = lax.broadcasted_iota(jnp.int32, lg_ref.shape, 1)
    x = jnp.where(lane < N_EXPERTS, lg_ref[...], -jnp.inf)
    vals, hots = [], []
    idx_out = jnp.zeros(lg_ref.shape, jnp.int32)
    for k in range(TOP_K):
        m = jnp.max(x, axis=-1, keepdims=True)
        am = jnp.min(jnp.where(x == m, lane, LANES), axis=-1, keepdims=True)
        hot = lane == am
        vals.append(m)
        hots.append(hot)
        idx_out = jnp.where(lane == k, am, idx_out)
        x = jnp.where(hot, -jnp.inf, x)
    es = [jnp.exp(v - vals[0]) for v in vals]
    denom = functools.reduce(lambda a, b: a + b, es)
    sel = functools.reduce(jnp.logical_or, hots)
    sel_f = jnp.where(sel, 1.0, 0.0)
    before = jnp.dot(tri_ref[...], sel_f.astype(BF16), preferred_element_type=F32) + carry[...]
    gate_out = jnp.zeros(lg_ref.shape, F32)
    rank_out = jnp.zeros(lg_ref.shape, F32)
    for k in range(TOP_K):
        gate_out = jnp.where(lane == k, es[k] / denom, gate_out)
        rk = jnp.sum(jnp.where(hots[k], before, 0.0), axis=-1, keepdims=True)
        rank_out = jnp.where(lane == k, rk, rank_out)
    idx_ref[...] = idx_out
    gate_ref[...] = gate_out
    rank_ref[...] = rank_out.astype(jnp.int32)
    total = carry[...] + jnp.sum(sel_f, axis=0, keepdims=True)
    carry[...] = total
    cnt_ref[...] = total.astype(jnp.int32)


def _router(logits):
    n = logits.shape[0]
    tm = ROUTER_TM
    r = jnp.arange(tm)
    tri = (r[None, :] < r[:, None]).astype(BF16)
    tile = pl.BlockSpec((tm, LANES), lambda i: (i, 0))
    return pl.pallas_call(
        _router_body,
        grid=(n // tm,),
        in_specs=[tile, pl.BlockSpec((tm, tm), lambda i: (0, 0))],
        out_specs=[tile, tile, tile, pl.BlockSpec((1, LANES), lambda i: (0, 0))],
        out_shape=[
            jax.ShapeDtypeStruct((n, LANES), jnp.int32),
            jax.ShapeDtypeStruct((n, LANES), F32),
            jax.ShapeDtypeStruct((n, LANES), jnp.int32),
            jax.ShapeDtypeStruct((1, LANES), jnp.int32),
        ],
        scratch_shapes=[pltpu.VMEM((1, LANES), F32)],
        compiler_params=_cparams(("arbitrary",), 16),
        name="router",
    )(logits, tri)


def _row_gather(idx_ref, n, src_hbm, dst, sem):
    def body(r, c):
        t = idx_ref[0, 0, r]
        pltpu.make_async_copy(src_hbm.at[pl.ds(t, 1), :], dst.at[pl.ds(r, 1), :], sem).start()
        return c
    lax.fori_loop(0, n, body, 0)


def _expert_body(wblk_ref, wexp_ref, wlo_ref, whi_ref, wfirst_ref, nv_ref, tokc_ref, tokn_ref,
                 xn_hbm, w1_hbm, b1_ref, w2_ref, b2_ref,
                 out_ref, xg, xb, act_s, w1buf, sem_x, sem_w, *, n_items):
    i = pl.program_id(0)
    nf = MOE_NF
    nv = nv_ref[0]
    slot = i % 2
    e_cur = wexp_ref[i]
    e_next = wexp_ref[jnp.minimum(i + 1, n_items - 1)]

    def rows_landed(s):
        return pltpu.make_async_copy(xn_hbm.at[pl.ds(0, MOE_TM), :], xg.at[s], sem_x.at[s])

    def w1_tile(e, f, s):
        return [pltpu.make_async_copy(w1_hbm.at[e, :, pl.ds(half * D_FF + f * MOE_TF, MOE_TF)],
                                      w1buf.at[s, half], sem_w.at[s]) for half in range(2)]

    @pl.when(i == nv)
    def _():
        rows_landed(slot).wait()
        for c in w1_tile(e_cur, 0, 0):
            c.wait()

    @pl.when(i == 0)
    def _():
        _row_gather(tokc_ref, MOE_TM, xn_hbm, xg.at[0], sem_x.at[0])
        for c in w1_tile(e_cur, 0, 0):
            c.start()

    @pl.when(i < nv)
    def _():
        rows_landed(slot).wait()
        xb[...] = xg[slot].astype(BF16)
        x = xb[...]
        for f in range(nf):
            s = f % 2
            for c in w1_tile(e_cur, f, s):
                c.wait()
            nxt = w1_tile(e_cur, f + 1, 1 - s) if f + 1 < nf else w1_tile(e_next, 0, 1 - s)
            for c in nxt:
                c.start()
            for r in range(f * MOE_SHARE, (f + 1) * MOE_SHARE):
                t = tokn_ref[0, 0, r]
                pltpu.make_async_copy(xn_hbm.at[pl.ds(t, 1), :], xg.at[1 - slot, pl.ds(r, 1), :],
                                      sem_x.at[1 - slot]).start()
            for c in range(MOE_TF // MXU_COLS):
                cs = slice(c * MXU_COLS, (c + 1) * MXU_COLS)
                gs = slice(f * MOE_TF + c * MXU_COLS, f * MOE_TF + (c + 1) * MXU_COLS)
                us = slice(D_FF + gs.start, D_FF + gs.stop)
                hg = jnp.dot(x, w1buf[s, 0, :, cs], preferred_element_type=F32) + b1_ref[0, :, gs]
                hu = jnp.dot(x, w1buf[s, 1, :, cs], preferred_element_type=F32) + b1_ref[0, :, us]
                hg = jnp.minimum(hg, SWIGLU_LIMIT)
                hu = jnp.clip(hu, -SWIGLU_LIMIT, SWIGLU_LIMIT)
                sig = 0.5 * jnp.tanh((0.5 * SWIGLU_ALPHA) * hg) + 0.5
                act_s[:, gs] = (hg * sig * (hu + 1.0)).astype(BF16)

    def ffn_out():
        return jnp.dot(act_s[...], w2_ref[0], preferred_element_type=F32) + b2_ref[0]

    @pl.when(jnp.logical_and(i < nv, wfirst_ref[i] == 1))
    def _():
        out_ref[...] = ffn_out()

    @pl.when(jnp.logical_and(i < nv, wfirst_ref[i] == 0))
    def _():
        row = lax.broadcasted_iota(jnp.int32, (MOE_TM, 1), 0)
        mine = jnp.logical_and(row >= wlo_ref[i], row < whi_ref[i])
        out_ref[...] = jnp.where(mine, ffn_out(), out_ref[...])


def _experts(work, sorted_tok3, xn, w1, b1, w2, b2):
    w_blk, w_exp, w_lo, w_hi, w_first, n_work = work
    n_items = w_blk.shape[0]
    nblk = sorted_tok3.shape[0]
    assert MOE_NF % 2 == 0

    def tok_map(shift):
        return lambda i, wb, *_: (wb[jnp.minimum(i + shift, n_items - 1)], 0, 0)

    by_expert = lambda i, wb, we, *_: (we[i], 0, 0)
    in_specs = [
        pl.BlockSpec((1, 1, MOE_TM), tok_map(0), memory_space=pltpu.SMEM),
        pl.BlockSpec((1, 1, MOE_TM), tok_map(1), memory_space=pltpu.SMEM),
        pl.BlockSpec(memory_space=pl.ANY),
        pl.BlockSpec(memory_space=pl.ANY),
        pl.BlockSpec((1, 1, 2 * D_FF), by_expert),
        pl.BlockSpec((1, D_FF, D_MODEL), by_expert),
        pl.BlockSpec((1, 1, D_MODEL), by_expert),
    ]
    return pl.pallas_call(
        functools.partial(_expert_body, n_items=n_items),
        grid_spec=pltpu.PrefetchScalarGridSpec(
            num_scalar_prefetch=6,
            grid=(n_items,),
            in_specs=in_specs,
            out_specs=pl.BlockSpec((MOE_TM, D_MODEL), lambda i, wb, *_: (wb[i], 0)),
            scratch_shapes=[
                pltpu.VMEM((2, MOE_TM, D_MODEL), F32),
                pltpu.VMEM((MOE_TM, D_MODEL), BF16),
                pltpu.VMEM((MOE_TM, D_FF), BF16),
                pltpu.VMEM((2, 2, D_MODEL, MOE_TF), BF16),
                pltpu.SemaphoreType.DMA((2,)),
                pltpu.SemaphoreType.DMA((2,)),
            ],
        ),
        out_shape=jax.ShapeDtypeStruct((nblk * MOE_TM, D_MODEL), F32),
        compiler_params=_cparams(("arbitrary",), 56),
        name="experts",
    )(w_blk, w_exp, w_lo, w_hi, w_first, n_work, sorted_tok3, sorted_tok3, xn, w1, b1, w2, b2)


def _combine_body(posc_ref, posn_ref, x2_ref, gate_ref, ys_hbm, outp_ref, outs_ref, buf, sem, *,
                  n_prompt_tiles):
    i = pl.program_id(0)
    n = pl.num_programs(0)
    slot = i % 2
    nrow = TOP_K * COMB_TT

    def issue(pref, s):
        def body(r, c):
            for k in range(TOP_K):
                p = pref[0, 0, TOP_K * r + k]
                pltpu.make_async_copy(ys_hbm.at[pl.ds(p, 1), :],
                                      buf.at[s, pl.ds(k * COMB_TT + r, 1), :], sem.at[s]).start()
            return c
        lax.fori_loop(0, COMB_TT, body, 0)

    @pl.when(i == 0)
    def _():
        issue(posc_ref, 0)

    pltpu.make_async_copy(ys_hbm.at[pl.ds(0, nrow), :], buf.at[slot], sem.at[slot]).wait()

    @pl.when(i + 1 < n)
    def _():
        issue(posn_ref, 1 - slot)

    y = x2_ref[...]
    g = gate_ref[...]
    for k in range(TOP_K):
        y = y + g[:, k:k + 1] * buf[slot, k * COMB_TT:(k + 1) * COMB_TT, :]

    @pl.when(i < n_prompt_tiles)
    def _():
        outp_ref[...] = y

    @pl.when(i >= n_prompt_tiles)
    def _():
        outs_ref[...] = y


def _combine(pos3, x2, gates, ys, n_prompt):
    n_tok = x2.shape[0]
    nt = n_tok // COMB_TT
    npt = n_prompt // COMB_TT
    assert n_tok - n_prompt == COMB_TT
    body = functools.partial(_combine_body, n_prompt_tiles=npt)
    return pl.pallas_call(
        body,
        grid=(nt,),
        in_specs=[
            pl.BlockSpec((1, 1, TOP_K * COMB_TT), lambda i: (i, 0, 0), memory_space=pltpu.SMEM),
            pl.BlockSpec((1, 1, TOP_K * COMB_TT), lambda i: (jnp.minimum(i + 1, nt - 1), 0, 0),
                         memory_space=pltpu.SMEM),
            pl.BlockSpec((COMB_TT, D_MODEL), lambda i: (i, 0)),
            pl.BlockSpec((COMB_TT, LANES), lambda i: (i, 0)),
            pl.BlockSpec(memory_space=pl.ANY),
        ],
        out_specs=[
            pl.BlockSpec((COMB_TT, D_MODEL), lambda i: (jnp.minimum(i, npt - 1), 0)),
            pl.BlockSpec((COMB_TT, D_MODEL), lambda i: (0, 0)),
        ],
        out_shape=[
            jax.ShapeDtypeStruct((n_prompt, D_MODEL), F32),
            jax.ShapeDtypeStruct((COMB_TT, D_MODEL), F32),
        ],
        scratch_shapes=[
            pltpu.VMEM((2, TOP_K * COMB_TT, D_MODEL), F32),
            pltpu.SemaphoreType.DMA((2,)),
        ],
        compiler_params=_cparams(("arbitrary",), 40),
        name="combine",
    )(pos3, pos3, x2, gates, ys)


def _t5_bucket(rel):
    nb = T5_BUCKETS // 2
    max_exact = nb // 2
    ret = (rel > 0).astype(jnp.int32) * nb
    n = jnp.abs(rel)
    large = max_exact + (jnp.log(jnp.maximum(n, 1).astype(F32) / max_exact)
                         / math.log(T5_MAX_DIST / max_exact) * (nb - max_exact)).astype(jnp.int32)
    large = jnp.minimum(large, nb - 1)
    return ret + jnp.where(n < max_exact, n, large)


def _dup_heads(t):
    lead = t.shape[:-1]
    t = t.reshape(*lead, A_KV_HEADS, 1, HEAD_DIM)
    return jnp.broadcast_to(t, (*lead, A_KV_HEADS, 2, HEAD_DIM)).reshape(*lead, A_KVD)


def _work_items(top_i, rank, counts, n_tok):
    nk = n_tok * TOP_K
    assert nk % MOE_TM == 0
    nblk = nk // MOE_TM
    n_items = nblk + N_EXPERTS
    end = jnp.cumsum(counts)
    start = end - counts
    first_blk = start // MOE_TM
    n_it = jnp.where(counts > 0, (end - 1) // MOE_TM - first_blk + 1, 0)
    it_end = jnp.cumsum(n_it)
    it_start = it_end - n_it
    n_work = it_end[-1]
    experts = jnp.arange(N_EXPERTS, dtype=jnp.int32)
    table = lambda tab, idx: jnp.sum(jnp.where(idx[..., None] == experts, tab, 0), axis=-1)
    w = jnp.arange(n_items, dtype=jnp.int32)
    wc = jnp.minimum(w, n_work - 1)
    w_exp = jnp.minimum(jnp.sum((it_end[None, :] <= wc[:, None]).astype(jnp.int32), axis=1), N_EXPERTS - 1)
    w_blk = table(first_blk, w_exp) + wc - table(it_start, w_exp)
    w_lo = jnp.clip(table(start, w_exp) - w_blk * MOE_TM, 0, MOE_TM)
    w_hi = jnp.clip(table(end, w_exp) - w_blk * MOE_TM, 0, MOE_TM)
    w_first = jnp.concatenate([jnp.ones((1,), jnp.int32), (w_blk[1:] != w_blk[:-1]).astype(jnp.int32)])
    pos = table(start, top_i) + rank
    flat_tok = jnp.arange(nk, dtype=jnp.int32) // TOP_K
    _, sorted_tok = lax.sort_key_val(top_i.reshape(-1), flat_tok, is_stable=True)
    i32 = lambda t: t.astype(jnp.int32)
    work = (i32(w_blk), i32(w_exp), i32(w_lo), i32(w_hi), w_first, i32(n_work).reshape(1))
    return sorted_tok.reshape(nblk, 1, MOE_TM), work, pos.reshape(n_tok // COMB_TT, 1, TOP_K * COMB_TT)


def _band_bias(vals, keys):
    vt = vals.T
    return jnp.stack([vt[:, CHUNK - 1 - i:CHUNK - 1 - i + keys] for i in range(CHUNK)], axis=1).astype(F32)


def kernel(x_prompt, x_sample, cache_a_k, cache_a_v, cache_b_k, cache_b_v, attn_norm, w_in,
           a_q_norm, a_k_norm, b_q_norm, b_k_norm, a_sinks, t5_table, b_rel_table, w_oa, w_ob,
           w_out, ffn_norm, router_w, router_b, w1, b1, w2, b2):
    batch, seq, _ = x_prompt.shape
    dec_b, dec_s, _ = x_sample.shape
    assert attn_norm.shape[0] == 1, "single layer"
    assert dec_s <= CHUNK and PAST_LEN % CHUNK == 0 and PAST_LEN >= PB
    n_prompt = batch * seq
    n_sample = dec_b * dec_s

    wi = w_in[0]
    c = [0]
    for wdt in (A_QW, A_KVW, A_KVW, B_W, B_W, B_W, D_MODEL, D_MODEL):
        c.append(c[-1] + wdt)
    w_qa, w_ka, w_va, w_qb, w_kb, w_vb, w_ga, w_gb = [wi[:, c[k]:c[k + 1]] for k in range(8)]
    w_perm = jnp.concatenate([w_qa, w_qb, w_kb, w_vb, w_ga, w_gb, _dup_heads(w_ka), _dup_heads(w_va)],
                             axis=1).astype(BF16)
    scale = HEAD_DIM ** -0.5
    ones = lambda n: jnp.ones((n,), F32)
    cg = jnp.concatenate([
        jnp.tile(a_q_norm[0] * scale, A_HEADS), jnp.tile(b_q_norm[0] * scale, B_HEADS),
        jnp.tile(b_k_norm[0], B_HEADS), ones(B_W), ones(2 * D_MODEL),
        jnp.tile(a_k_norm[0], 2 * A_KV_HEADS), ones(A_KVD)]).reshape(1, PROJ_W).astype(F32)
    hd = jnp.arange(MXU_COLS) // HEAD_DIM
    gm = (hd[:, None] == hd[None, :]).astype(BF16)
    g_attn = attn_norm[0].reshape(1, D_MODEL)
    rel_a = jnp.arange(LA + CHUNK - 1) - PA - (CHUNK - 1)
    rel_b = jnp.arange(LB + CHUNK - 1) - PB - (CHUNK - 1)
    bias_a = _band_bias(t5_table[_t5_bucket(rel_a)], LA)
    bias_b = _band_bias(b_rel_table[0][jnp.clip(rel_b, -B_REL_CLIP, CHUNK - 1) + B_REL_CLIP], LB)
    sinks = a_sinks[0].astype(F32)
    woa, wob, wout = w_oa[0].astype(BF16), w_ob[0].astype(BF16), w_out[0].astype(BF16)
    fg = ffn_norm[0].reshape(1, D_MODEL)
    wr = jnp.pad(router_w[0], ((0, 0), (0, LANES - N_EXPERTS))).astype(BF16)
    br = jnp.pad(router_b[0], (0, LANES - N_EXPERTS)).reshape(1, LANES)
    w1b, w2b = w1[0].astype(BF16), w2[0].astype(BF16)
    b1r = b1[0].reshape(N_EXPERTS, 1, 2 * D_FF)
    b2r = b2[0].reshape(N_EXPERTS, 1, D_MODEL)

    xp2 = x_prompt.reshape(n_prompt, D_MODEL)
    h_p, kv_p = _proj(xp2, g_attn, w_perm, cg, gm, tm=1024)
    h_p3 = h_p.reshape(batch, seq, PROJ_W)
    cols_p = dict(qa=0, qb=1, kbp=2, kbc=2, vbp=3, vbc=3, kap=16, kac=16, vap=17, vac=17)
    oa_p, ob_p = _attn(sinks, h_p3, h_p3, h_p3, h_p3, h_p3, h_p3, h_p3, h_p3, h_p3, bias_a, bias_b,
                       cpb=8, first_pos=0, hi_a=LA, hi_b=LB, cols=cols_p)
    n_tok = n_prompt + n_sample

    xs_pad = jnp.pad(x_sample, ((0, 0), (0, CHUNK - dec_s), (0, 0))).reshape(dec_b * CHUNK, D_MODEL)
    h_s, kv_s = _proj(xs_pad, g_attn, w_perm, cg, gm, tm=dec_b * CHUNK)
    h_s3 = h_s.reshape(dec_b, CHUNK, PROJ_W)
    ckb = cache_b_k[0].reshape(dec_b, PB, B_W).astype(BF16)
    cvb = cache_b_v[0].reshape(dec_b, PB, B_W).astype(BF16)
    cka = _dup_heads(cache_a_k[0].reshape(dec_b, PA, A_KVW)).astype(BF16)
    cva = _dup_heads(cache_a_v[0].reshape(dec_b, PA, A_KVW)).astype(BF16)
    cols_s = dict(qa=0, qb=1, kbp=0, kbc=2, vbp=0, vbc=3, kap=0, kac=16, vap=0, vac=17)
    oa_s, ob_s = _attn(sinks, h_s3, ckb, h_s3, cvb, h_s3, cka, h_s3, cva, h_s3, bias_a, bias_b,
                       cpb=1, first_pos=PAST_LEN, hi_a=PA + dec_s, hi_b=PB + dec_s, cols=cols_s)
    keep_rows = lambda t: t[:, :dec_s].reshape(n_sample, t.shape[-1])
    x2, xn, logits = _merge(
        (oa_p.reshape(n_prompt, A_QW), ob_p.reshape(n_prompt, B_W), h_p, xp2),
        (keep_rows(oa_s), keep_rows(ob_s), keep_rows(h_s3), x_sample.reshape(n_sample, D_MODEL)),
        woa, wob, wout, fg, wr, br)

    top_i, gates, rank, counts = _router(logits)
    sorted_tok3, work, pos3 = _work_items(top_i[:, :TOP_K], rank[:, :TOP_K], counts[0, :N_EXPERTS], n_tok)
    ys = _experts(work, sorted_tok3, xn, w1b, b1r, w2b, b2r)
    y_p, y_s = _combine(pos3, x2, gates, ys, n_prompt)

    kv_p3 = kv_p.reshape(batch, seq, KV32_W)
    kv_s3 = kv_s.reshape(dec_b, CHUNK, KV32_W)
    undup = lambda t: t.reshape(*t.shape[:-1], A_KV_HEADS, 2, HEAD_DIM)[..., 0, :]
    heads_b = lambda t: t.reshape(*t.shape[:-1], B_HEADS, HEAD_DIM)
    o_kb, o_vb, o_ka, o_va = 0, B_W, 2 * B_W, 2 * B_W + A_KVD
    p_bk = heads_b(kv_p3[:, seq - PB:, o_kb:o_kb + B_W])[None]
    p_bv = heads_b(kv_p3[:, seq - PB:, o_vb:o_vb + B_W])[None]
    p_ak = undup(kv_p3[:, seq - PA:, o_ka:o_ka + A_KVD])[None]
    p_av = undup(kv_p3[:, seq - PA:, o_va:o_va + A_KVD])[None]
    s_bk = heads_b(kv_s3[:, :dec_s, o_kb:o_kb + B_W])[None]
    s_bv = heads_b(kv_s3[:, :dec_s, o_vb:o_vb + B_W])[None]
    s_ak = undup(kv_s3[:, :dec_s, o_ka:o_ka + A_KVD])[None]
    s_av = undup(kv_s3[:, :dec_s, o_va:o_va + A_KVD])[None]
    return (y_p.reshape(batch, seq, D_MODEL), y_s.reshape(dec_b, dec_s, D_MODEL),
            p_ak, p_av, p_bk, p_bv, s_ak, s_av, s_bk, s_bv)
```

```python
import functools
import math

import jax
import jax.numpy as jnp
from jax import lax
from jax.experimental import pallas as pl
from jax.experimental.pallas import tpu as pltpu

F32 = jnp.float32
BF16 = jnp.bfloat16

D_MODEL = 2048
CHUNK = 64
HEAD_DIM = 64
A_HEADS = 16
A_KV_HEADS = 4
A_PAST_CHUNKS = 2
B_HEADS = 16
B_PAST_CHUNKS = 8
B_REL_CLIP = 256
T5_BUCKETS = 32
T5_MAX_DIST = (A_PAST_CHUNKS + 1) * CHUNK
N_EXPERTS = 32
TOP_K = 4
D_FF = D_MODEL
SWIGLU_ALPHA = 1.702
SWIGLU_LIMIT = 7.0
NORM_EPS = 1e-6
NEG_INF = -1e30
LOG2E = math.log2(math.e)
PAST_LEN = 1024

A_QW = A_HEADS * HEAD_DIM
A_KVW = A_KV_HEADS * HEAD_DIM
B_W = B_HEADS * HEAD_DIM
PA = A_PAST_CHUNKS * CHUNK
PB = B_PAST_CHUNKS * CHUNK
LA = PA + CHUNK
LB = PB + CHUNK

LANES = 128
MXU_COLS = 256

A_KVD = 2 * A_KVW
PROJ_W = A_QW + 3 * B_W + 2 * D_MODEL + 2 * A_KVD
PROJ_TN = 512
KV32_W = 2 * B_W + 2 * A_KVD
_NORM_TILES = (0, 1, 2, 3, 4, 5, 16)
_SIGMOID_LO, _SIGMOID_HI = 8, 16

MOE_TM = 512
MOE_TF = 512
MOE_NF = D_FF // MOE_TF
MOE_SHARE = MOE_TM // MOE_NF
COMB_TT = 256
ROUTER_TM = 256
MERGE_TM = 256
ATTN_AHEAD = 6


def _cparams(sem, vmem_mb):
    return pltpu.CompilerParams(dimension_semantics=sem, vmem_limit_bytes=vmem_mb * 1024 * 1024)


def _proj_body(x_ref, g_ref, w_ref, cg_ref, gm_ref, h_ref, kv_ref, xn_s):
    j = pl.program_id(1)

    @pl.when(j == 0)
    def _():
        x = x_ref[...]
        ms = jnp.mean(x * x, axis=-1, keepdims=True)
        xn_s[...] = (x * lax.rsqrt(ms + NORM_EPS) * g_ref[...]).astype(BF16)

    is_norm = functools.reduce(jnp.logical_or, [j == t for t in _NORM_TILES])
    is_sig = jnp.logical_and(j >= _SIGMOID_LO, j < _SIGMOID_HI)

    def strips(epilogue, write_kv, dots_first=False):
        col = [slice(c * MXU_COLS, (c + 1) * MXU_COLS) for c in range(PROJ_TN // MXU_COLS)]
        mm = lambda cs: jnp.dot(xn_s[...], w_ref[:, cs], preferred_element_type=F32)
        accs = [mm(cs) for cs in col] if dots_first else None
        for c, cs in enumerate(col):
            y = epilogue(accs[c] if dots_first else mm(cs), cs)
            h_ref[:, cs] = y.astype(BF16)
            if write_kv:
                kv_ref[:, cs] = y

    def norm(a, cs):
        ss = jnp.dot((a * a).astype(BF16), gm_ref[...], preferred_element_type=F32)
        return a * lax.rsqrt(ss * (1.0 / HEAD_DIM) + NORM_EPS) * cg_ref[:, cs]

    @pl.when(is_norm)
    def _():
        strips(norm, True, dots_first=True)

    @pl.when(is_sig)
    def _():
        strips(lambda a, cs: 0.5 * jnp.tanh(0.5 * a) + 0.5, False)

    @pl.when(jnp.logical_not(jnp.logical_or(is_norm, is_sig)))
    def _():
        strips(lambda a, cs: a, True)


def _kv_tile(j):
    return jnp.clip(j - 4, 0, 3) + (j >= 16).astype(jnp.int32) + (j >= 17).astype(jnp.int32)


def _proj(x2d, g, w_perm, cg, gm, tm):
    n = x2d.shape[0]
    grid = (n // tm, PROJ_W // PROJ_TN)
    return pl.pallas_call(
        _proj_body,
        grid=grid,
        in_specs=[
            pl.BlockSpec((tm, D_MODEL), lambda i, j: (i, 0)),
            pl.BlockSpec((1, D_MODEL), lambda i, j: (0, 0)),
            pl.BlockSpec((D_MODEL, PROJ_TN), lambda i, j: (0, j)),
            pl.BlockSpec((1, PROJ_TN), lambda i, j: (0, j)),
            pl.BlockSpec((MXU_COLS, MXU_COLS), lambda i, j: (0, 0)),
        ],
        out_specs=[
            pl.BlockSpec((tm, PROJ_TN), lambda i, j: (i, j)),
            pl.BlockSpec((tm, PROJ_TN), lambda i, j: (i, _kv_tile(j))),
        ],
        out_shape=[
            jax.ShapeDtypeStruct((n, PROJ_W), BF16),
            jax.ShapeDtypeStruct((n, KV32_W), F32),
        ],
        scratch_shapes=[pltpu.VMEM((tm, D_MODEL), BF16)],
        compiler_params=_cparams(("parallel", "arbitrary"), 48),
        name="proj",
    )(x2d, g, w_perm, cg, gm)


def _attn_body(sink_ref, qa_ref, qb_ref, kbp_ref, kbc_ref, vbp_ref, vbc_ref,
               kap_ref, kac_ref, vap_ref, vac_ref, ba_ref, bb_ref,
               oa_ref, ob_ref, kb_s, vb_s, ka_s, va_s, *, cpb, first_pos, hi_a, hi_b):
    i = pl.program_id(1)
    qb_rows = cpb * CHUNK
    kb_s[0:PB, :] = kbp_ref[0]
    kb_s[PB:PB + qb_rows, :] = kbc_ref[0]
    vb_s[0:PB, :] = vbp_ref[0]
    vb_s[PB:PB + qb_rows, :] = vbc_ref[0]
    ka_s[0:PA, :] = kap_ref[0]
    ka_s[PA:PA + qb_rows, :] = kac_ref[0]
    va_s[0:PA, :] = vap_ref[0]
    va_s[PA:PA + qb_rows, :] = vac_ref[0]

    lane_a = lax.broadcasted_iota(jnp.int32, (1, LA), 1)
    lane_b = lax.broadcasted_iota(jnp.int32, (1, LB), 1)
    low_half = lax.broadcasted_iota(jnp.int32, (1, LANES), 1) < HEAD_DIM
    nt = (((1,), (1,)), ((), ()))

    tasks = [("a", h) for h in range(A_HEADS)] + [("b", h) for h in range(B_HEADS)]

    def scores(task, r0, valid_a, valid_b):
        mixer, h = task
        p, half = h // 2, h % 2
        cols = slice(p * LANES, (p + 1) * LANES)
        if mixer == "a":
            kcols = slice((p // 2) * LANES, (p // 2 + 1) * LANES)
            q_p, k_p, bias, valid = qa_ref[0, pl.ds(r0, CHUNK), cols], ka_s[pl.ds(r0, LA), kcols], ba_ref[h], valid_a
        else:
            q_p, k_p, bias, valid = qb_ref[0, pl.ds(r0, CHUNK), cols], kb_s[pl.ds(r0, LB), cols], bb_ref[h], valid_b
        qm = jnp.where(low_half if half == 0 else jnp.logical_not(low_half), q_p, jnp.zeros_like(q_p))
        s = lax.dot_general(qm, k_p, nt, preferred_element_type=F32) + bias
        if valid is not None:
            s = jnp.where(valid, s, NEG_INF)
        return s

    def attend(task, s, r0):
        mixer, h = task
        p = h // 2
        if mixer == "a":
            kcols = slice((p // 2) * LANES, (p // 2 + 1) * LANES)
            v_p, sink = va_s[pl.ds(r0, LA), kcols], sink_ref[h]
        else:
            v_p, sink = vb_s[pl.ds(r0, LB), slice(p * LANES, (p + 1) * LANES)], None
        m = jnp.max(s, axis=-1, keepdims=True)
        if sink is not None:
            m = jnp.maximum(m, sink)
        e = jnp.exp2(s - m)
        l = jnp.sum(e, axis=-1, keepdims=True)
        if sink is not None:
            l = l + jnp.exp2(sink - m)
        o = jnp.dot(e.astype(BF16), v_p, preferred_element_type=F32)
        return o / l

    def make_chunk(masked):
        def chunk(jc, carry):
            r0 = pl.multiple_of(jc * CHUNK, CHUNK)
            valid_a = valid_b = None
            if masked:
                start = first_pos + (i * cpb + jc) * CHUNK
                valid_a = jnp.logical_and(lane_a >= jnp.maximum(PA - start, 0), lane_a < hi_a)
                valid_b = jnp.logical_and(lane_b >= jnp.maximum(PB - start, 0), lane_b < hi_b)
            pending, outs = {}, {}
            for t in range(len(tasks) + ATTN_AHEAD):
                if t < len(tasks):
                    pending[t] = scores(tasks[t], r0, valid_a, valid_b)
                d = t - ATTN_AHEAD
                if d >= 0:
                    mixer, h = tasks[d]
                    outs[h % 2] = attend(tasks[d], pending.pop(d), r0)
                    if h % 2 == 1:
                        cols = slice((h // 2) * LANES, (h // 2 + 1) * LANES)
                        o_ref = oa_ref if mixer == "a" else ob_ref
                        o_ref[0, pl.ds(r0, CHUNK), cols] = jnp.where(low_half, outs[0], outs[1]).astype(BF16)
            return carry
        return chunk

    n_masked_blocks = pl.cdiv(max(PB - first_pos, 0), cpb * CHUNK)
    if hi_a < LA or hi_b < LB:
        lax.fori_loop(0, cpb, make_chunk(True), 0)
    else:
        @pl.when(i < n_masked_blocks)
        def _():
            lax.fori_loop(0, cpb, make_chunk(True), 0)

        @pl.when(i >= n_masked_blocks)
        def _():
            lax.fori_loop(0, cpb, make_chunk(False), 0)


def _attn(sinks, hq, kbp, kbc, vbp, vbc, kap, kac, vap, vac, bias_a, bias_b, *,
          cpb, first_pos, hi_a, hi_b, cols):
    nb, s, _ = hq.shape
    qb_rows = cpb * CHUNK
    nblk = s // qb_rows
    rb = qb_rows // PB if qb_rows >= PB else None
    ra = qb_rows // PA if qb_rows >= PA else None

    def prev_map(ratio, col):
        if ratio is None:
            return lambda b, i, *_: (b, 0, col)
        return lambda b, i, *_: (b, jnp.maximum(i * ratio - 1, 0), col)

    def cur_map(col):
        return lambda b, i, *_: (b, i, col)

    in_specs = [
        pl.BlockSpec((1, qb_rows, A_QW), cur_map(cols["qa"])),
        pl.BlockSpec((1, qb_rows, B_W), cur_map(cols["qb"])),
        pl.BlockSpec((1, PB, B_W), prev_map(rb, cols["kbp"])),
        pl.BlockSpec((1, qb_rows, B_W), cur_map(cols["kbc"])),
        pl.BlockSpec((1, PB, B_W), prev_map(rb, cols["vbp"])),
        pl.BlockSpec((1, qb_rows, B_W), cur_map(cols["vbc"])),
        pl.BlockSpec((1, PA, A_KVD), prev_map(ra, cols["kap"])),
        pl.BlockSpec((1, qb_rows, A_KVD), cur_map(cols["kac"])),
        pl.BlockSpec((1, PA, A_KVD), prev_map(ra, cols["vap"])),
        pl.BlockSpec((1, qb_rows, A_KVD), cur_map(cols["vac"])),
        pl.BlockSpec((A_HEADS, CHUNK, LA), lambda b, i, *_: (0, 0, 0)),
        pl.BlockSpec((B_HEADS, CHUNK, LB), lambda b, i, *_: (0, 0, 0)),
    ]
    out_specs = [
        pl.BlockSpec((1, qb_rows, A_QW), lambda b, i, *_: (b, i, 0)),
        pl.BlockSpec((1, qb_rows, B_W), lambda b, i, *_: (b, i, 0)),
    ]
    body = functools.partial(_attn_body, cpb=cpb, first_pos=first_pos, hi_a=hi_a, hi_b=hi_b)
    return pl.pallas_call(
        body,
        grid_spec=pltpu.PrefetchScalarGridSpec(
            num_scalar_prefetch=1,
            grid=(nb, nblk),
            in_specs=in_specs,
            out_specs=out_specs,
            scratch_shapes=[
                pltpu.VMEM((PB + qb_rows, B_W), BF16),
                pltpu.VMEM((PB + qb_rows, B_W), BF16),
                pltpu.VMEM((PA + qb_rows, A_KVD), BF16),
                pltpu.VMEM((PA + qb_rows, A_KVD), BF16),
            ],
        ),
        out_shape=[
            jax.ShapeDtypeStruct((nb, s, A_QW), BF16),
            jax.ShapeDtypeStruct((nb, s, B_W), BF16),
        ],
        compiler_params=_cparams(("parallel", "arbitrary"), 48),
        name="attn",
    )(sinks, hq, hq, kbp, kbc, vbp, vbc, kap, kac, vap, vac, bias_a, bias_b)


def _merge_body(oa_p, ob_p, ga_p, gb_p, x_p, oa_s, ob_s, ga_s, gb_s, x_s,
                woa_ref, wob_ref, wout_ref, fg_ref, wr_ref, br_ref, x2_ref, xn_ref, lg_ref, *,
                n_prompt_tiles):
    def tile(oa_ref, ob_ref, ga_ref, gb_ref, x_ref):
        ya = jnp.dot(oa_ref[...], woa_ref[...], preferred_element_type=F32)
        yb = jnp.dot(ob_ref[...], wob_ref[...], preferred_element_type=F32)
        z = ga_ref[...].astype(F32) * ya + gb_ref[...].astype(F32) * yb
        y = jnp.dot(z.astype(BF16), wout_ref[...], preferred_element_type=F32)
        x2 = x_ref[...] + y
        x2_ref[...] = x2
        ms = jnp.mean(x2 * x2, axis=-1, keepdims=True)
        xn = x2 * lax.rsqrt(ms + NORM_EPS) * fg_ref[...]
        xn_ref[...] = xn
        lg_ref[...] = jnp.dot(xn.astype(BF16), wr_ref[...], preferred_element_type=F32) + br_ref[...]

    i = pl.program_id(0)

    @pl.when(i < n_prompt_tiles)
    def _():
        tile(oa_p, ob_p, ga_p, gb_p, x_p)

    @pl.when(i >= n_prompt_tiles)
    def _():
        tile(oa_s, ob_s, ga_s, gb_s, x_s)


def _merge(prompt, sample, w_oa, w_ob, w_out, fg, wr, br):
    tm = MERGE_TM
    n_p, n_s = prompt[3].shape[0], sample[3].shape[0]
    npt, nst = n_p // tm, n_s // tm
    n_out = n_p + n_s
    const = lambda i: (0, 0)
    resident = functools.partial(pl.BlockSpec, index_map=const, pipeline_mode=pl.Buffered(1))

    def token_specs(row):
        return [
            pl.BlockSpec((tm, A_QW), lambda i: (row(i), 0)),
            pl.BlockSpec((tm, B_W), lambda i: (row(i), 0)),
            pl.BlockSpec((tm, D_MODEL), lambda i: (row(i), 2)),
            pl.BlockSpec((tm, D_MODEL), lambda i: (row(i), 3)),
            pl.BlockSpec((tm, D_MODEL), lambda i: (row(i), 0)),
        ]

    in_specs = (token_specs(lambda i: jnp.minimum(i, npt - 1))
                + token_specs(lambda i: jnp.clip(i - npt, 0, nst - 1))
                + [resident((A_QW, D_MODEL)), resident((B_W, D_MODEL)), resident((D_MODEL, D_MODEL)),
                   resident((1, D_MODEL)), resident((D_MODEL, LANES)), resident((1, LANES))])
    oa_p, ob_p, h_p, x_p = prompt
    oa_s, ob_s, h_s, x_s = sample
    return pl.pallas_call(
        functools.partial(_merge_body, n_prompt_tiles=npt),
        grid=(npt + nst,),
        in_specs=in_specs,
        out_specs=[
            pl.BlockSpec((tm, D_MODEL), lambda i: (i, 0)),
            pl.BlockSpec((tm, D_MODEL), lambda i: (i, 0)),
            pl.BlockSpec((tm, LANES), lambda i: (i, 0)),
        ],
        out_shape=[
            jax.ShapeDtypeStruct((n_out, D_MODEL), F32),
            jax.ShapeDtypeStruct((n_out, D_MODEL), F32),
            jax.ShapeDtypeStruct((n_out, LANES), F32),
        ],
        compiler_params=_cparams(("arbitrary",), 56),
        name="merge",
    )(oa_p, ob_p, h_p, h_p, x_p, oa_s, ob_s, h_s, h_s, x_s, w_oa, w_ob, w_out, fg, wr, br)


def _router_body(lg_ref, idx_ref, gate_ref, cnt_ref, carry):
    i = pl.program_id(0)

    @pl.when(i == 0)
    def _():
        carry[...] = jnp.zeros_like(carry)

    lane = lax.broadcasted_iota(jnp.int32, lg_ref.shape, 1)
    x = jnp.where(lane < N_EXPERTS, lg_ref[...], -jnp.inf)
    vals, hots = [], []
    idx_out = jnp.zeros(lg_ref.shape, jnp.int32)
    for k in range(TOP_K):
        m = jnp.max(x, axis=-1, keepdims=True)
        am = jnp.min(jnp.where(x == m, lane, LANES), axis=-1, keepdims=True)
        hot = lane == am
        vals.append(m)
        hots.append(hot)
        idx_out = jnp.where(lane == k, am, idx_out)
        x = jnp.where(hot, -jnp.inf, x)
    es = [jnp.exp(v - vals[0]) for v in vals]
    denom = functools.reduce(lambda a, b: a + b, es)
    gate_out = jnp.zeros(lg_ref.shape, F32)
    for k in range(TOP_K):
        gate_out = jnp.where(lane == k, es[k] / denom, gate_out)
    idx_ref[...] = idx_out
    gate_ref[...] = gate_out
    sel = jnp.where(functools.reduce(jnp.logical_or, hots), 1.0, 0.0)
    total = carry[...] + jnp.sum(sel, axis=0, keepdims=True)
    carry[...] = total
    cnt_ref[...] = total.astype(jnp.int32)


def _router(logits):
    n = logits.shape[0]
    tm = ROUTER_TM
    tile = pl.BlockSpec((tm, LANES), lambda i: (i, 0))
    return pl.pallas_call(
        _router_body,
        grid=(n // tm,),
        in_specs=[tile],
        out_specs=[tile, tile, pl.BlockSpec((1, LANES), lambda i: (0, 0))],
        out_shape=[
            jax.ShapeDtypeStruct((n, LANES), jnp.int32),
            jax.ShapeDtypeStruct((n, LANES), F32),
            jax.ShapeDtypeStruct((1, LANES), jnp.int32),
        ],
        scratch_shapes=[pltpu.VMEM((1, LANES), F32)],
        compiler_params=_cparams(("arbitrary",), 16),
        name="router",
    )(logits)


def _expert_body(wblk_ref, wexp_ref, wlo_ref, whi_ref, nv_ref, idxp_ref, idxc_ref, idxn_ref,
                 xn_hbm, w1_hbm, b1_ref, w2_ref, b2_ref,
                 ys_hbm, xg, xb, act_s, w1buf, yo, sem_x, sem_w, sem_y, *, n_items, n_rows):
    i = pl.program_id(0)
    nf = MOE_NF
    nv = nv_ref[0]
    slot = i % 2
    e_cur = wexp_ref[i]
    e_next = wexp_ref[jnp.minimum(i + 1, n_items - 1)]
    prev = jnp.maximum(i - 1, 0)
    lo_p = jnp.where(i == 0, 0, wlo_ref[prev])
    hi_p = jnp.where(i == 0, 0, whi_ref[prev])

    def rows_landed(s):
        return pltpu.make_async_copy(xn_hbm.at[pl.ds(0, MOE_TM), :], xg.at[s], sem_x.at[s])

    def rows_sent(s):
        return pltpu.make_async_copy(yo.at[s], ys_hbm.at[pl.ds(0, MOE_TM), :], sem_y.at[s])

    def w1_tile(e, f, s):
        return pltpu.make_async_copy(w1_hbm.at[e, f], w1buf.at[s], sem_w.at[s])

    def fetch_row(r, idx_ref, s):
        t = lax.shift_right_logical(idx_ref[0, 0, r], TOP_K.bit_length() - 1)
        pltpu.make_async_copy(xn_hbm.at[pl.ds(t, 1), :], xg.at[s, pl.ds(r, 1), :], sem_x.at[s]).start()

    def send_row(r, idx_ref, lo, hi, s):
        mine = jnp.logical_and(r >= lo, r < hi)
        idx = idx_ref[0, 0, r]
        slot_row = (idx & (TOP_K - 1)) * (n_rows // TOP_K) + lax.shift_right_logical(idx, TOP_K.bit_length() - 1)
        dst = jnp.where(mine, slot_row, n_rows + s * MOE_TM + r)
        pltpu.make_async_copy(yo.at[s, pl.ds(r, 1), :], ys_hbm.at[pl.ds(dst, 1), :], sem_y.at[s]).start()

    @pl.when(i == 0)
    def _():
        lax.fori_loop(0, MOE_TM, lambda r, c: (fetch_row(r, idxc_ref, 0), c)[1], 0)
        w1_tile(e_cur, 0, 0).start()
        yo[...] = jnp.zeros_like(yo)
        for s in range(2):
            spare = pltpu.make_async_copy(yo.at[s], ys_hbm.at[pl.ds(n_rows + s * MOE_TM, MOE_TM), :], sem_y.at[s])
            spare.start()
            spare.wait()

    @pl.when(jnp.logical_and(i >= 1, i <= nv))
    def _():
        rows_sent(slot).wait()

    @pl.when(i == nv)
    def _():
        rows_landed(slot).wait()
        w1_tile(e_cur, 0, 0).wait()
        lax.fori_loop(0, MOE_TM, lambda r, c: (send_row(r, idxp_ref, lo_p, hi_p, 1 - slot), c)[1], 0)
        rows_sent(1 - slot).wait()

    @pl.when(i < nv)
    def _():
        rows_landed(slot).wait()
        xb[...] = xg[slot].astype(BF16)
        x = xb[...]
        for f in range(nf):
            s = f % 2
            w1_tile(e_cur, f, s).wait()
            (w1_tile(e_cur, f + 1, 1 - s) if f + 1 < nf else w1_tile(e_next, 0, 1 - s)).start()
            for r in range(f * MOE_SHARE, (f + 1) * MOE_SHARE):
                fetch_row(r, idxn_ref, 1 - slot)
                send_row(r, idxp_ref, lo_p, hi_p, 1 - slot)
            for c in range(MOE_TF // MXU_COLS):
                cs = slice(c * MXU_COLS, (c + 1) * MXU_COLS)
                gs = slice(f * MOE_TF + c * MXU_COLS, f * MOE_TF + (c + 1) * MXU_COLS)
                us = slice(D_FF + gs.start, D_FF + gs.stop)
                hg = jnp.dot(x, w1buf[s, 0, :, cs], preferred_element_type=F32) + b1_ref[0, :, gs]
                hu = jnp.dot(x, w1buf[s, 1, :, cs], preferred_element_type=F32) + b1_ref[0, :, us]
                hg = jnp.minimum(hg, SWIGLU_LIMIT)
                hu = jnp.clip(hu, -SWIGLU_LIMIT, SWIGLU_LIMIT)
                sig = 0.5 * jnp.tanh((0.5 * SWIGLU_ALPHA) * hg) + 0.5
                act_s[:, gs] = (hg * sig * (hu + 1.0)).astype(BF16)
        yo[slot] = jnp.dot(act_s[...], w2_ref[0], preferred_element_type=F32) + b2_ref[0]


def _experts(work, sorted_idx3, xn, w1t, b1, w2, b2):
    w_blk, w_exp, w_lo, w_hi, n_work = work
    n_items = w_blk.shape[0]
    n_rows = sorted_idx3.shape[0] * MOE_TM
    assert MOE_NF % 2 == 0

    def idx_map(shift):
        return lambda i, wb, *_: (wb[jnp.clip(i + shift, 0, n_items - 1)], 0, 0)

    by_expert = lambda i, wb, we, *_: (we[i], 0, 0)
    in_specs = [
        pl.BlockSpec((1, 1, MOE_TM), idx_map(-1), memory_space=pltpu.SMEM),
        pl.BlockSpec((1, 1, MOE_TM), idx_map(0), memory_space=pltpu.SMEM),
        pl.BlockSpec((1, 1, MOE_TM), idx_map(1), memory_space=pltpu.SMEM),
        pl.BlockSpec(memory_space=pl.ANY),
        pl.BlockSpec(memory_space=pl.ANY),
        pl.BlockSpec((1, 1, 2 * D_FF), by_expert),
        pl.BlockSpec((1, D_FF, D_MODEL), by_expert),
        pl.BlockSpec((1, 1, D_MODEL), by_expert),
    ]
    return pl.pallas_call(
        functools.partial(_expert_body, n_items=n_items, n_rows=n_rows),
        grid_spec=pltpu.PrefetchScalarGridSpec(
            num_scalar_prefetch=5,
            grid=(n_items,),
            in_specs=in_specs,
            out_specs=pl.BlockSpec(memory_space=pl.ANY),
            scratch_shapes=[
                pltpu.VMEM((2, MOE_TM, D_MODEL), F32),
                pltpu.VMEM((MOE_TM, D_MODEL), BF16),
                pltpu.VMEM((MOE_TM, D_FF), BF16),
                pltpu.VMEM((2, 2, D_MODEL, MOE_TF), BF16),
                pltpu.VMEM((2, MOE_TM, D_MODEL), F32),
                pltpu.SemaphoreType.DMA((2,)),
                pltpu.SemaphoreType.DMA((2,)),
                pltpu.SemaphoreType.DMA((2,)),
            ],
        ),
        out_shape=jax.ShapeDtypeStruct((n_rows + 2 * MOE_TM, D_MODEL), F32),
        compiler_params=_cparams(("arbitrary",), 56),
        name="experts",
    )(w_blk, w_exp, w_lo, w_hi, n_work, sorted_idx3, sorted_idx3, sorted_idx3, xn, w1t, b1, w2, b2)


def _combine_body(x2_ref, gate_ref, *rest, n_prompt_tiles):
    ys_refs, (outp_ref, outs_ref) = rest[:TOP_K], rest[TOP_K:]
    i = pl.program_id(0)
    y = x2_ref[...]
    g = gate_ref[...]
    for k in range(TOP_K):
        y = y + g[:, k:k + 1] * ys_refs[k][...]

    @pl.when(i < n_prompt_tiles)
    def _():
        outp_ref[...] = y

    @pl.when(i >= n_prompt_tiles)
    def _():
        outs_ref[...] = y


def _combine(x2, gates, ys, n_prompt):
    n_tok = x2.shape[0]
    nt = n_tok // COMB_TT
    npt = n_prompt // COMB_TT
    assert n_tok - n_prompt == COMB_TT
    body = functools.partial(_combine_body, n_prompt_tiles=npt)
    return pl.pallas_call(
        body,
        grid=(nt,),
        in_specs=[
            pl.BlockSpec((COMB_TT, D_MODEL), lambda i: (i, 0)),
            pl.BlockSpec((COMB_TT, LANES), lambda i: (i, 0)),
        ] + [pl.BlockSpec((COMB_TT, D_MODEL), lambda i, k=k: (k * nt + i, 0)) for k in range(TOP_K)],
        out_specs=[
            pl.BlockSpec((COMB_TT, D_MODEL), lambda i: (jnp.minimum(i, npt - 1), 0)),
            pl.BlockSpec((COMB_TT, D_MODEL), lambda i: (0, 0)),
        ],
        out_shape=[
            jax.ShapeDtypeStruct((n_prompt, D_MODEL), F32),
            jax.ShapeDtypeStruct((COMB_TT, D_MODEL), F32),
        ],
        compiler_params=_cparams(("arbitrary",), 40),
        name="combine",
    )(x2, gates, *([ys] * TOP_K))


def _t5_bucket(rel):
    nb = T5_BUCKETS // 2
    max_exact = nb // 2
    ret = (rel > 0).astype(jnp.int32) * nb
    n = jnp.abs(rel)
    large = max_exact + (jnp.log(jnp.maximum(n, 1).astype(F32) / max_exact)
                         / math.log(T5_MAX_DIST / max_exact) * (nb - max_exact)).astype(jnp.int32)
    large = jnp.minimum(large, nb - 1)
    return ret + jnp.where(n < max_exact, n, large)


def _dup_heads(t):
    lead = t.shape[:-1]
    t = t.reshape(*lead, A_KV_HEADS, 1, HEAD_DIM)
    return jnp.broadcast_to(t, (*lead, A_KV_HEADS, 2, HEAD_DIM)).reshape(*lead, A_KVD)


def _work_items(top_i, counts, n_tok):
    nk = n_tok * TOP_K
    assert nk % MOE_TM == 0
    nblk = nk // MOE_TM
    n_items = nblk + N_EXPERTS
    end = jnp.cumsum(counts)
    start = end - counts
    first_blk = start // MOE_TM
    n_it = jnp.where(counts > 0, (end - 1) // MOE_TM - first_blk + 1, 0)
    it_end = jnp.cumsum(n_it)
    it_start = it_end - n_it
    n_work = it_end[-1]
    experts = jnp.arange(N_EXPERTS, dtype=jnp.int32)
    table = lambda tab, idx: jnp.sum(jnp.where(idx[..., None] == experts, tab, 0), axis=-1)
    w = jnp.arange(n_items, dtype=jnp.int32)
    wc = jnp.minimum(w, n_work - 1)
    w_exp = jnp.minimum(jnp.sum((it_end[None, :] <= wc[:, None]).astype(jnp.int32), axis=1), N_EXPERTS - 1)
    w_blk = table(first_blk, w_exp) + wc - table(it_start, w_exp)
    w_lo = jnp.clip(table(start, w_exp) - w_blk * MOE_TM, 0, MOE_TM)
    w_hi = jnp.clip(table(end, w_exp) - w_blk * MOE_TM, 0, MOE_TM)
    idx = jnp.arange(nk, dtype=jnp.int32)
    _, sorted_idx = lax.sort_key_val(top_i.reshape(-1), idx, is_stable=True)
    i32 = lambda t: t.astype(jnp.int32)
    work = (i32(w_blk), i32(w_exp), i32(w_lo), i32(w_hi), i32(n_work).reshape(1))
    return sorted_idx.reshape(nblk, 1, MOE_TM), work


def _band_bias(vals, keys):
    vt = vals.T
    return jnp.stack([vt[:, CHUNK - 1 - i:CHUNK - 1 - i + keys] for i in range(CHUNK)], axis=1).astype(F32)


def kernel(x_prompt, x_sample, cache_a_k, cache_a_v, cache_b_k, cache_b_v, attn_norm, w_in,
           a_q_norm, a_k_norm, b_q_norm, b_k_norm, a_sinks, t5_table, b_rel_table, w_oa, w_ob,
           w_out, ffn_norm, router_w, router_b, w1, b1, w2, b2):
    batch, seq, _ = x_prompt.shape
    dec_b, dec_s, _ = x_sample.shape
    assert attn_norm.shape[0] == 1, "single layer"
    assert dec_s <= CHUNK and PAST_LEN % CHUNK == 0 and PAST_LEN >= PB
    n_prompt = batch * seq
    n_sample = dec_b * dec_s

    wi = w_in[0]
    c = [0]
    for wdt in (A_QW, A_KVW, A_KVW, B_W, B_W, B_W, D_MODEL, D_MODEL):
        c.append(c[-1] + wdt)
    w_qa, w_ka, w_va, w_qb, w_kb, w_vb, w_ga, w_gb = [wi[:, c[k]:c[k + 1]] for k in range(8)]
    w_perm = jnp.concatenate([w_qa, w_qb, w_kb, w_vb, w_ga, w_gb, _dup_heads(w_ka), _dup_heads(w_va)],
                             axis=1).astype(BF16)
    scale = HEAD_DIM ** -0.5 * LOG2E
    ones = lambda n: jnp.ones((n,), F32)
    cg = jnp.concatenate([
        jnp.tile(a_q_norm[0] * scale, A_HEADS), jnp.tile(b_q_norm[0] * scale, B_HEADS),
        jnp.tile(b_k_norm[0], B_HEADS), ones(B_W), ones(2 * D_MODEL),
        jnp.tile(a_k_norm[0], 2 * A_KV_HEADS), ones(A_KVD)]).reshape(1, PROJ_W).astype(F32)
    hd = jnp.arange(MXU_COLS) // HEAD_DIM
    gm = (hd[:, None] == hd[None, :]).astype(BF16)
    g_attn = attn_norm[0].reshape(1, D_MODEL)
    rel_a = jnp.arange(LA + CHUNK - 1) - PA - (CHUNK - 1)
    rel_b = jnp.arange(LB + CHUNK - 1) - PB - (CHUNK - 1)
    bias_a = _band_bias(t5_table[_t5_bucket(rel_a)] * LOG2E, LA)
    bias_b = _band_bias(b_rel_table[0][jnp.clip(rel_b, -B_REL_CLIP, CHUNK - 1) + B_REL_CLIP] * LOG2E, LB)
    sinks = a_sinks[0].astype(F32) * LOG2E
    woa, wob, wout = w_oa[0].astype(BF16), w_ob[0].astype(BF16), w_out[0].astype(BF16)
    fg = ffn_norm[0].reshape(1, D_MODEL)
    wr = jnp.pad(router_w[0], ((0, 0), (0, LANES - N_EXPERTS))).astype(BF16)
    br = jnp.pad(router_b[0], (0, LANES - N_EXPERTS)).reshape(1, LANES)
    w1b = jnp.transpose(w1[0].astype(BF16).reshape(N_EXPERTS, D_MODEL, 2, MOE_NF, MOE_TF), (0, 3, 2, 1, 4))
    w2b = w2[0].astype(BF16)
    b1r = b1[0].reshape(N_EXPERTS, 1, 2 * D_FF)
    b2r = b2[0].reshape(N_EXPERTS, 1, D_MODEL)

    xp2 = x_prompt.reshape(n_prompt, D_MODEL)
    h_p, kv_p = _proj(xp2, g_attn, w_perm, cg, gm, tm=1024)
    h_p3 = h_p.reshape(batch, seq, PROJ_W)
    cols_p = dict(qa=0, qb=1, kbp=2, kbc=2, vbp=3, vbc=3, kap=16, kac=16, vap=17, vac=17)
    oa_p, ob_p = _attn(sinks, h_p3, h_p3, h_p3, h_p3, h_p3, h_p3, h_p3, h_p3, h_p3, bias_a, bias_b,
                       cpb=8, first_pos=0, hi_a=LA, hi_b=LB, cols=cols_p)
    n_tok = n_prompt + n_sample

    xs_pad = jnp.pad(x_sample, ((0, 0), (0, CHUNK - dec_s), (0, 0))).reshape(dec_b * CHUNK, D_MODEL)
    h_s, kv_s = _proj(xs_pad, g_attn, w_perm, cg, gm, tm=dec_b * CHUNK)
    h_s3 = h_s.reshape(dec_b, CHUNK, PROJ_W)
    ckb = cache_b_k[0].reshape(dec_b, PB, B_W).astype(BF16)
    cvb = cache_b_v[0].reshape(dec_b, PB, B_W).astype(BF16)
    cka = _dup_heads(cache_a_k[0].reshape(dec_b, PA, A_KVW)).astype(BF16)
    cva = _dup_heads(cache_a_v[0].reshape(dec_b, PA, A_KVW)).astype(BF16)
    cols_s = dict(qa=0, qb=1, kbp=0, kbc=2, vbp=0, vbc=3, kap=0, kac=16, vap=0, vac=17)
    oa_s, ob_s = _attn(sinks, h_s3, ckb, h_s3, cvb, h_s3, cka, h_s3, cva, h_s3, bias_a, bias_b,
                       cpb=1, first_pos=PAST_LEN, hi_a=PA + dec_s, hi_b=PB + dec_s, cols=cols_s)
    keep_rows = lambda t: t[:, :dec_s].reshape(n_sample, t.shape[-1])
    x2, xn, logits = _merge(
        (oa_p.reshape(n_prompt, A_QW), ob_p.reshape(n_prompt, B_W), h_p, xp2),
        (keep_rows(oa_s), keep_rows(ob_s), keep_rows(h_s3), x_sample.reshape(n_sample, D_MODEL)),
        woa, wob, wout, fg, wr, br)

    top_i, gates, counts = _router(logits)
    sorted_idx3, work = _work_items(top_i[:, :TOP_K], counts[0, :N_EXPERTS], n_tok)
    ys = _experts(work, sorted_idx3, xn, w1b, b1r, w2b, b2r)
    y_p, y_s = _combine(x2, gates, ys, n_prompt)

    kv_p3 = kv_p.reshape(batch, seq, KV32_W)
    kv_s3 = kv_s.reshape(dec_b, CHUNK, KV32_W)
    undup = lambda t: t.reshape(*t.shape[:-1], A_KV_HEADS, 2, HEAD_DIM)[..., 0, :]
    heads_b = lambda t: t.reshape(*t.shape[:-1], B_HEADS, HEAD_DIM)
    o_kb, o_vb, o_ka, o_va = 0, B_W, 2 * B_W, 2 * B_W + A_KVD
    p_bk = heads_b(kv_p3[:, seq - PB:, o_kb:o_kb + B_W])[None]
    p_bv = heads_b(kv_p3[:, seq - PB:, o_vb:o_vb + B_W])[None]
    p_ak = undup(kv_p3[:, seq - PA:, o_ka:o_ka + A_KVD])[None]
    p_av = undup(kv_p3[:, seq - PA:, o_va:o_va + A_KVD])[None]
    s_bk = heads_b(kv_s3[:, :dec_s, o_kb:o_kb + B_W])[None]
    s_bv = heads_b(kv_s3[:, :dec_s, o_vb:o_vb + B_W])[None]
    s_ak = undup(kv_s3[:, :dec_s, o_ka:o_ka + A_KVD])[None]
    s_av = undup(kv_s3[:, :dec_s, o_va:o_va + A_KVD])[None]
    return (y_p.reshape(batch, seq, D_MODEL), y_s.reshape(dec_b, dec_s, D_MODEL),
            p_ak, p_av, p_bk, p_bv, s_ak, s_av, s_bk, s_bv)
```

```python
import functools
import math

import jax
import jax.numpy as jnp
from jax import lax
from jax.experimental import pallas as pl
from jax.experimental.pallas import tpu as pltpu

F32 = jnp.float32
BF16 = jnp.bfloat16

D_MODEL = 2048
CHUNK = 64
HEAD_DIM = 64
A_HEADS = 16
A_KV_HEADS = 4
A_PAST_CHUNKS = 2
B_HEADS = 16
B_PAST_CHUNKS = 8
B_REL_CLIP = 256
T5_BUCKETS = 32
T5_MAX_DIST = (A_PAST_CHUNKS + 1) * CHUNK
N_EXPERTS = 32
TOP_K = 4
D_FF = D_MODEL
SWIGLU_ALPHA = 1.702
SWIGLU_LIMIT = 7.0
NORM_EPS = 1e-6
NEG_INF = -1e30
LOG2E = math.log2(math.e)
PAST_LEN = 1024

A_QW = A_HEADS * HEAD_DIM
A_KVW = A_KV_HEADS * HEAD_DIM
B_W = B_HEADS * HEAD_DIM
PA = A_PAST_CHUNKS * CHUNK
PB = B_PAST_CHUNKS * CHUNK
LA = PA + CHUNK
LB = PB + CHUNK

LANES = 128
MXU_COLS = 256

A_KVD = 2 * A_KVW
PROJ_W = A_QW + 3 * B_W + 2 * D_MODEL + 2 * A_KVD
PROJ_TN = 512
KV32_W = 2 * B_W + 2 * A_KVD
_NORM_TILES = (0, 1, 2, 3, 4, 5, 16)
_SIGMOID_LO, _SIGMOID_HI = 8, 16

MOE_TM = 512
MOE_TF = 512
MOE_NF = D_FF // MOE_TF
MOE_SHARE = MOE_TM // MOE_NF
COMB_TT = 256
ROUTER_TM = 256
MERGE_TM = 256
ATTN_AHEAD = 6


def _cparams(sem, vmem_mb):
    return pltpu.CompilerParams(dimension_semantics=sem, vmem_limit_bytes=vmem_mb * 1024 * 1024)


def _proj_body(x_ref, g_ref, w_ref, cg_ref, gm_ref, h_ref, kv_ref, xn_s):
    j = pl.program_id(1)

    @pl.when(j == 0)
    def _():
        x = x_ref[...]
        ms = jnp.mean(x * x, axis=-1, keepdims=True)
        xn_s[...] = (x * lax.rsqrt(ms + NORM_EPS) * g_ref[...]).astype(BF16)

    is_norm = functools.reduce(jnp.logical_or, [j == t for t in _NORM_TILES])
    is_sig = jnp.logical_and(j >= _SIGMOID_LO, j < _SIGMOID_HI)

    def strips(epilogue, write_kv, dots_first=False):
        col = [slice(c * MXU_COLS, (c + 1) * MXU_COLS) for c in range(PROJ_TN // MXU_COLS)]
        mm = lambda cs: jnp.dot(xn_s[...], w_ref[:, cs], preferred_element_type=F32)
        accs = [mm(cs) for cs in col] if dots_first else None
        for c, cs in enumerate(col):
            y = epilogue(accs[c] if dots_first else mm(cs), cs)
            h_ref[:, cs] = y.astype(BF16)
            if write_kv:
                kv_ref[:, cs] = y

    def norm(a, cs):
        ss = jnp.dot((a * a).astype(BF16), gm_ref[...], preferred_element_type=F32)
        return a * lax.rsqrt(ss * (1.0 / HEAD_DIM) + NORM_EPS) * cg_ref[:, cs]

    @pl.when(is_norm)
    def _():
        strips(norm, True, dots_first=True)

    @pl.when(is_sig)
    def _():
        strips(lambda a, cs: 0.5 * jnp.tanh(0.5 * a) + 0.5, False)

    @pl.when(jnp.logical_not(jnp.logical_or(is_norm, is_sig)))
    def _():
        strips(lambda a, cs: a, True)


def _kv_tile(j):
    return jnp.clip(j - 4, 0, 3) + (j >= 16).astype(jnp.int32) + (j >= 17).astype(jnp.int32)


def _proj(x2d, g, w_perm, cg, gm, tm):
    n = x2d.shape[0]
    grid = (n // tm, PROJ_W // PROJ_TN)
    return pl.pallas_call(
        _proj_body,
        grid=grid,
        in_specs=[
            pl.BlockSpec((tm, D_MODEL), lambda i, j: (i, 0)),
            pl.BlockSpec((1, D_MODEL), lambda i, j: (0, 0)),
            pl.BlockSpec((D_MODEL, PROJ_TN), lambda i, j: (0, j)),
            pl.BlockSpec((1, PROJ_TN), lambda i, j: (0, j)),
            pl.BlockSpec((MXU_COLS, MXU_COLS), lambda i, j: (0, 0)),
        ],
        out_specs=[
            pl.BlockSpec((tm, PROJ_TN), lambda i, j: (i, j)),
            pl.BlockSpec((tm, PROJ_TN), lambda i, j: (i, _kv_tile(j))),
        ],
        out_shape=[
            jax.ShapeDtypeStruct((n, PROJ_W), BF16),
            jax.ShapeDtypeStruct((n, KV32_W), F32),
        ],
        scratch_shapes=[pltpu.VMEM((tm, D_MODEL), BF16)],
        compiler_params=_cparams(("parallel", "arbitrary"), 48),
        name="proj",
    )(x2d, g, w_perm, cg, gm)


def _attn_body(sink_ref, qa_ref, qb_ref, kbp_ref, kbc_ref, vbp_ref, vbc_ref,
               kap_ref, kac_ref, vap_ref, vac_ref, ba_ref, bb_ref,
               oa_ref, ob_ref, kb_s, vb_s, ka_s, va_s, *, cpb, first_pos, hi_a, hi_b):
    i = pl.program_id(1)
    qb_rows = cpb * CHUNK
    kb_s[0:PB, :] = kbp_ref[0]
    kb_s[PB:PB + qb_rows, :] = kbc_ref[0]
    vb_s[0:PB, :] = vbp_ref[0]
    vb_s[PB:PB + qb_rows, :] = vbc_ref[0]
    ka_s[0:PA, :] = kap_ref[0]
    ka_s[PA:PA + qb_rows, :] = kac_ref[0]
    va_s[0:PA, :] = vap_ref[0]
    va_s[PA:PA + qb_rows, :] = vac_ref[0]

    lane_a = lax.broadcasted_iota(jnp.int32, (1, LA), 1)
    lane_b = lax.broadcasted_iota(jnp.int32, (1, LB), 1)
    low_half = lax.broadcasted_iota(jnp.int32, (1, LANES), 1) < HEAD_DIM
    nt = (((1,), (1,)), ((), ()))

    tasks = [("a", h) for h in range(A_HEADS)] + [("b", h) for h in range(B_HEADS)]

    def scores(task, r0, valid_a, valid_b):
        mixer, h = task
        p, half = h // 2, h % 2
        cols = slice(p * LANES, (p + 1) * LANES)
        if mixer == "a":
            kcols = slice((p // 2) * LANES, (p // 2 + 1) * LANES)
            q_p, k_p, bias, valid = qa_ref[0, pl.ds(r0, CHUNK), cols], ka_s[pl.ds(r0, LA), kcols], ba_ref[h], valid_a
        else:
            q_p, k_p, bias, valid = qb_ref[0, pl.ds(r0, CHUNK), cols], kb_s[pl.ds(r0, LB), cols], bb_ref[h], valid_b
        qm = jnp.where(low_half if half == 0 else jnp.logical_not(low_half), q_p, jnp.zeros_like(q_p))
        s = lax.dot_general(qm, k_p, nt, preferred_element_type=F32) + bias
        if valid is not None:
            s = jnp.where(valid, s, NEG_INF)
        return s

    def attend(task, s, r0):
        mixer, h = task
        p = h // 2
        if mixer == "a":
            kcols = slice((p // 2) * LANES, (p // 2 + 1) * LANES)
            v_p, sink = va_s[pl.ds(r0, LA), kcols], sink_ref[h]
        else:
            v_p, sink = vb_s[pl.ds(r0, LB), slice(p * LANES, (p + 1) * LANES)], None
        m = jnp.max(s, axis=-1, keepdims=True)
        if sink is not None:
            m = jnp.maximum(m, sink)
        e = jnp.exp2(s - m)
        l = jnp.sum(e, axis=-1, keepdims=True)
        if sink is not None:
            l = l + jnp.exp2(sink - m)
        o = jnp.dot(e.astype(BF16), v_p, preferred_element_type=F32)
        return o / l

    def make_chunk(masked):
        def chunk(jc, carry):
            r0 = pl.multiple_of(jc * CHUNK, CHUNK)
            valid_a = valid_b = None
            if masked:
                start = first_pos + (i * cpb + jc) * CHUNK
                valid_a = jnp.logical_and(lane_a >= jnp.maximum(PA - start, 0), lane_a < hi_a)
                valid_b = jnp.logical_and(lane_b >= jnp.maximum(PB - start, 0), lane_b < hi_b)
            pending, outs = {}, {}
            for t in range(len(tasks) + ATTN_AHEAD):
                if t < len(tasks):
                    pending[t] = scores(tasks[t], r0, valid_a, valid_b)
                d = t - ATTN_AHEAD
                if d >= 0:
                    mixer, h = tasks[d]
                    outs[h % 2] = attend(tasks[d], pending.pop(d), r0)
                    if h % 2 == 1:
                        cols = slice((h // 2) * LANES, (h // 2 + 1) * LANES)
                        o_ref = oa_ref if mixer == "a" else ob_ref
                        o_ref[0, pl.ds(r0, CHUNK), cols] = jnp.where(low_half, outs[0], outs[1]).astype(BF16)
            return carry
        return chunk

    n_masked_blocks = pl.cdiv(max(PB - first_pos, 0), cpb * CHUNK)
    if hi_a < LA or hi_b < LB:
        lax.fori_loop(0, cpb, make_chunk(True), 0)
    else:
        @pl.when(i < n_masked_blocks)
        def _():
            lax.fori_loop(0, cpb, make_chunk(True), 0)

        @pl.when(i >= n_masked_blocks)
        def _():
            lax.fori_loop(0, cpb, make_chunk(False), 0)


def _attn(sinks, hq, kbp, kbc, vbp, vbc, kap, kac, vap, vac, bias_a, bias_b, *,
          cpb, first_pos, hi_a, hi_b, cols):
    nb, s, _ = hq.shape
    qb_rows = cpb * CHUNK
    nblk = s // qb_rows
    rb = qb_rows // PB if qb_rows >= PB else None
    ra = qb_rows // PA if qb_rows >= PA else None

    def prev_map(ratio, col):
        if ratio is None:
            return lambda b, i, *_: (b, 0, col)
        return lambda b, i, *_: (b, jnp.maximum(i * ratio - 1, 0), col)

    def cur_map(col):
        return lambda b, i, *_: (b, i, col)

    in_specs = [
        pl.BlockSpec((1, qb_rows, A_QW), cur_map(cols["qa"])),
        pl.BlockSpec((1, qb_rows, B_W), cur_map(cols["qb"])),
        pl.BlockSpec((1, PB, B_W), prev_map(rb, cols["kbp"])),
        pl.BlockSpec((1, qb_rows, B_W), cur_map(cols["kbc"])),
        pl.BlockSpec((1, PB, B_W), prev_map(rb, cols["vbp"])),
        pl.BlockSpec((1, qb_rows, B_W), cur_map(cols["vbc"])),
        pl.BlockSpec((1, PA, A_KVD), prev_map(ra, cols["kap"])),
        pl.BlockSpec((1, qb_rows, A_KVD), cur_map(cols["kac"])),
        pl.BlockSpec((1, PA, A_KVD), prev_map(ra, cols["vap"])),
        pl.BlockSpec((1, qb_rows, A_KVD), cur_map(cols["vac"])),
        pl.BlockSpec((A_HEADS, CHUNK, LA), lambda b, i, *_: (0, 0, 0)),
        pl.BlockSpec((B_HEADS, CHUNK, LB), lambda b, i, *_: (0, 0, 0)),
    ]
    out_specs = [
        pl.BlockSpec((1, qb_rows, A_QW), lambda b, i, *_: (b, i, 0)),
        pl.BlockSpec((1, qb_rows, B_W), lambda b, i, *_: (b, i, 0)),
    ]
    body = functools.partial(_attn_body, cpb=cpb, first_pos=first_pos, hi_a=hi_a, hi_b=hi_b)
    return pl.pallas_call(
        body,
        grid_spec=pltpu.PrefetchScalarGridSpec(
            num_scalar_prefetch=1,
            grid=(nb, nblk),
            in_specs=in_specs,
            out_specs=out_specs,
            scratch_shapes=[
                pltpu.VMEM((PB + qb_rows, B_W), BF16),
                pltpu.VMEM((PB + qb_rows, B_W), BF16),
                pltpu.VMEM((PA + qb_rows, A_KVD), BF16),
                pltpu.VMEM((PA + qb_rows, A_KVD), BF16),
            ],
        ),
        out_shape=[
            jax.ShapeDtypeStruct((nb, s, A_QW), BF16),
            jax.ShapeDtypeStruct((nb, s, B_W), BF16),
        ],
        compiler_params=_cparams(("parallel", "arbitrary"), 48),
        name="attn",
    )(sinks, hq, hq, kbp, kbc, vbp, vbc, kap, kac, vap, vac, bias_a, bias_b)


def _merge_body(oa_p, ob_p, ga_p, gb_p, x_p, oa_s, ob_s, ga_s, gb_s, x_s,
                woa_ref, wob_ref, wout_ref, fg_ref, wr_ref, br_ref, x2_ref, xn_ref, lg_ref, *,
                n_prompt_tiles):
    def tile(oa_ref, ob_ref, ga_ref, gb_ref, x_ref):
        ya = jnp.dot(oa_ref[...], woa_ref[...], preferred_element_type=F32)
        yb = jnp.dot(ob_ref[...], wob_ref[...], preferred_element_type=F32)
        z = ga_ref[...].astype(F32) * ya + gb_ref[...].astype(F32) * yb
        y = jnp.dot(z.astype(BF16), wout_ref[...], preferred_element_type=F32)
        x2 = x_ref[...] + y
        x2_ref[...] = x2
        ms = jnp.mean(x2 * x2, axis=-1, keepdims=True)
        xn = x2 * lax.rsqrt(ms + NORM_EPS) * fg_ref[...]
        xn_ref[...] = xn
        lg_ref[...] = jnp.dot(xn.astype(BF16), wr_ref[...], preferred_element_type=F32) + br_ref[...]

    i = pl.program_id(0)

    @pl.when(i < n_prompt_tiles)
    def _():
        tile(oa_p, ob_p, ga_p, gb_p, x_p)

    @pl.when(i >= n_prompt_tiles)
    def _():
        tile(oa_s, ob_s, ga_s, gb_s, x_s)


def _merge(prompt, sample, w_oa, w_ob, w_out, fg, wr, br):
    tm = MERGE_TM
    n_p, n_s = prompt[3].shape[0], sample[3].shape[0]
    npt, nst = n_p // tm, n_s // tm
    n_out = n_p + n_s
    const = lambda i: (0, 0)
    resident = functools.partial(pl.BlockSpec, index_map=const, pipeline_mode=pl.Buffered(1))

    def token_specs(row):
        return [
            pl.BlockSpec((tm, A_QW), lambda i: (row(i), 0)),
            pl.BlockSpec((tm, B_W), lambda i: (row(i), 0)),
            pl.BlockSpec((tm, D_MODEL), lambda i: (row(i), 2)),
            pl.BlockSpec((tm, D_MODEL), lambda i: (row(i), 3)),
            pl.BlockSpec((tm, D_MODEL), lambda i: (row(i), 0)),
        ]

    in_specs = (token_specs(lambda i: jnp.minimum(i, npt - 1))
                + token_specs(lambda i: jnp.clip(i - npt, 0, nst - 1))
                + [resident((A_QW, D_MODEL)), resident((B_W, D_MODEL)), resident((D_MODEL, D_MODEL)),
                   resident((1, D_MODEL)), resident((D_MODEL, LANES)), resident((1, LANES))])
    oa_p, ob_p, h_p, x_p = prompt
    oa_s, ob_s, h_s, x_s = sample
    return pl.pallas_call(
        functools.partial(_merge_body, n_prompt_tiles=npt),
        grid=(npt + nst,),
        in_specs=in_specs,
        out_specs=[
            pl.BlockSpec((tm, D_MODEL), lambda i: (i, 0)),
            pl.BlockSpec((tm, D_MODEL), lambda i: (i, 0)),
            pl.BlockSpec((tm, LANES), lambda i: (i, 0)),
        ],
        out_shape=[
            jax.ShapeDtypeStruct((n_out, D_MODEL), F32),
            jax.ShapeDtypeStruct((n_out, D_MODEL), F32),
            jax.ShapeDtypeStruct((n_out, LANES), F32),
        ],
        compiler_params=_cparams(("arbitrary",), 56),
        name="merge",
    )(oa_p, ob_p, h_p, h_p, x_p, oa_s, ob_s, h_s, h_s, x_s, w_oa, w_ob, w_out, fg, wr, br)


def _router_body(lg_ref, tri_ref, idx_ref, gate_ref, rank_ref, cnt_ref, carry):
    i = pl.program_id(0)

    @pl.when(i == 0)
    def _():
        carry[...] = jnp.zeros_like(carry)

    lane = lax.broadcasted_iota(jnp.int32, lg_ref.shape, 1)
    x = jnp.where(lane < N_EXPERTS, lg_ref[...], -jnp.inf)
    vals, hots = [], []
    idx_out = jnp.zeros(lg_ref.shape, jnp.int32)
    for k in range(TOP_K):
        m = jnp.max(x, axis=-1, keepdims=True)
        am = jnp.min(jnp.where(x == m, lane, LANES), axis=-1, keepdims=True)
        hot = lane == am
        vals.append(m)
        hots.append(hot)
        idx_out = jnp.where(lane == k, am, idx_out)
        x = jnp.where(hot, -jnp.inf, x)
    es = [jnp.exp(v - vals[0]) for v in vals]
    denom = functools.reduce(lambda a, b: a + b, es)
    sel = functools.reduce(jnp.logical_or, hots)
    sel_f = jnp.where(sel, 1.0, 0.0)
    before = jnp.dot(tri_ref[...], sel_f.astype(BF16), preferred_element_type=F32) + carry[...]
    gate_out = jnp.zeros(lg_ref.shape, F32)
    rank_out = jnp.zeros(lg_ref.shape, F32)
    for k in range(TOP_K):
        gate_out = jnp.where(lane == k, es[k] / denom, gate_out)
        rk = jnp.sum(jnp.where(hots[k], before, 0.0), axis=-1, keepdims=True)
        rank_out = jnp.where(lane == k, rk, rank_out)
    idx_ref[...] = idx_out
    gate_ref[...] = gate_out
    rank_ref[...] = rank_out.astype(jnp.int32)
    total = carry[...] + jnp.sum(sel_f, axis=0, keepdims=True)
    carry[...] = total
    cnt_ref[...] = total.astype(jnp.int32)


def _router(logits):
    n = logits.shape[0]
    tm = ROUTER_TM
    r = jnp.arange(tm)
    tri = (r[None, :] < r[:, None]).astype(BF16)
    tile = pl.BlockSpec((tm, LANES), lambda i: (i, 0))
    return pl.pallas_call(
        _router_body,
        grid=(n // tm,),
        in_specs=[tile, pl.BlockSpec((tm, tm), lambda i: (0, 0))],
        out_specs=[tile, tile, tile, pl.BlockSpec((1, LANES), lambda i: (0, 0))],
        out_shape=[
            jax.ShapeDtypeStruct((n, LANES), jnp.int32),
            jax.ShapeDtypeStruct((n, LANES), F32),
            jax.ShapeDtypeStruct((n, LANES), jnp.int32),
            jax.ShapeDtypeStruct((1, LANES), jnp.int32),
        ],
        scratch_shapes=[pltpu.VMEM((1, LANES), F32)],
        compiler_params=_cparams(("arbitrary",), 16),
        name="router",
    )(logits, tri)


def _row_gather(idx_ref, n, src_hbm, dst, sem):
    def body(r, c):
        t = idx_ref[0, 0, r]
        pltpu.make_async_copy(src_hbm.at[pl.ds(t, 1), :], dst.at[pl.ds(r, 1), :], sem).start()
        return c
    lax.fori_loop(0, n, body, 0)


def _expert_body(wblk_ref, wexp_ref, wlo_ref, whi_ref, wfirst_ref, nv_ref, tokc_ref, tokn_ref,
                 xn_hbm, w1_hbm, b1_ref, w2_ref, b2_ref,
                 out_ref, xg, xb, act_s, w1buf, sem_x, sem_w, *, n_items):
    i = pl.program_id(0)
    nf = MOE_NF
    nv = nv_ref[0]
    slot = i % 2
    e_cur = wexp_ref[i]
    e_next = wexp_ref[jnp.minimum(i + 1, n_items - 1)]

    def rows_landed(s):
        return pltpu.make_async_copy(xn_hbm.at[pl.ds(0, MOE_TM), :], xg.at[s], sem_x.at[s])

    def w1_tile(e, f, s):
        return [pltpu.make_async_copy(w1_hbm.at[e, :, pl.ds(half * D_FF + f * MOE_TF, MOE_TF)],
                                      w1buf.at[s, half], sem_w.at[s]) for half in range(2)]

    @pl.when(i == nv)
    def _():
        rows_landed(slot).wait()
        for c in w1_tile(e_cur, 0, 0):
            c.wait()

    @pl.when(i == 0)
    def _():
        _row_gather(tokc_ref, MOE_TM, xn_hbm, xg.at[0], sem_x.at[0])
        for c in w1_tile(e_cur, 0, 0):
            c.start()

    @pl.when(i < nv)
    def _():
        rows_landed(slot).wait()
        xb[...] = xg[slot].astype(BF16)
        x = xb[...]
        for f in range(nf):
            s = f % 2
            for c in w1_tile(e_cur, f, s):
                c.wait()
            nxt = w1_tile(e_cur, f + 1, 1 - s) if f + 1 < nf else w1_tile(e_next, 0, 1 - s)
            for c in nxt:
                c.start()
            for r in range(f * MOE_SHARE, (f + 1) * MOE_SHARE):
                t = tokn_ref[0, 0, r]
                pltpu.make_async_copy(xn_hbm.at[pl.ds(t, 1), :], xg.at[1 - slot, pl.ds(r, 1), :],
                                      sem_x.at[1 - slot]).start()
            for c in range(MOE_TF // MXU_COLS):
                cs = slice(c * MXU_COLS, (c + 1) * MXU_COLS)
                gs = slice(f * MOE_TF + c * MXU_COLS, f * MOE_TF + (c + 1) * MXU_COLS)
                us = slice(D_FF + gs.start, D_FF + gs.stop)
                hg = jnp.dot(x, w1buf[s, 0, :, cs], preferred_element_type=F32) + b1_ref[0, :, gs]
                hu = jnp.dot(x, w1buf[s, 1, :, cs], preferred_element_type=F32) + b1_ref[0, :, us]
                hg = jnp.minimum(hg, SWIGLU_LIMIT)
                hu = jnp.clip(hu, -SWIGLU_LIMIT, SWIGLU_LIMIT)
                sig = 0.5 * jnp.tanh((0.5 * SWIGLU_ALPHA) * hg) + 0.5
                act_s[:, gs] = (hg * sig * (hu + 1.0)).astype(BF16)

    def ffn_out():
        return jnp.dot(act_s[...], w2_ref[0], preferred_element_type=F32) + b2_ref[0]

    @pl.when(jnp.logical_and(i < nv, wfirst_ref[i] == 1))
    def _():
        out_ref[...] = ffn_out()

    @pl.when(jnp.logical_and(i < nv, wfirst_ref[i] == 0))
    def _():
        row = lax.broadcasted_iota(jnp.int32, (MOE_TM, 1), 0)
        mine = jnp.logical_and(row >= wlo_ref[i], row < whi_ref[i])
        out_ref[...] = jnp.where(mine, ffn_out(), out_ref[...])


def _experts(work, sorted_tok3, xn, w1, b1, w2, b2):
    w_blk, w_exp, w_lo, w_hi, w_first, n_work = work
    n_items = w_blk.shape[0]
    nblk = sorted_tok3.shape[0]
    assert MOE_NF % 2 == 0

    def tok_map(shift):
        return lambda i, wb, *_: (wb[jnp.minimum(i + shift, n_items - 1)], 0, 0)

    by_expert = lambda i, wb, we, *_: (we[i], 0, 0)
    in_specs = [
        pl.BlockSpec((1, 1, MOE_TM), tok_map(0), memory_space=pltpu.SMEM),
        pl.BlockSpec((1, 1, MOE_TM), tok_map(1), memory_space=pltpu.SMEM),
        pl.BlockSpec(memory_space=pl.ANY),
        pl.BlockSpec(memory_space=pl.ANY),
        pl.BlockSpec((1, 1, 2 * D_FF), by_expert),
        pl.BlockSpec((1, D_FF, D_MODEL), by_expert),
        pl.BlockSpec((1, 1, D_MODEL), by_expert),
    ]
    return pl.pallas_call(
        functools.partial(_expert_body, n_items=n_items),
        grid_spec=pltpu.PrefetchScalarGridSpec(
            num_scalar_prefetch=6,
            grid=(n_items,),
            in_specs=in_specs,
            out_specs=pl.BlockSpec((MOE_TM, D_MODEL), lambda i, wb, *_: (wb[i], 0)),
            scratch_shapes=[
                pltpu.VMEM((2, MOE_TM, D_MODEL), F32),
                pltpu.VMEM((MOE_TM, D_MODEL), BF16),
                pltpu.VMEM((MOE_TM, D_FF), BF16),
                pltpu.VMEM((2, 2, D_MODEL, MOE_TF), BF16),
                pltpu.SemaphoreType.DMA((2,)),
                pltpu.SemaphoreType.DMA((2,)),
            ],
        ),
        out_shape=jax.ShapeDtypeStruct((nblk * MOE_TM, D_MODEL), F32),
        compiler_params=_cparams(("arbitrary",), 56),
        name="experts",
    )(w_blk, w_exp, w_lo, w_hi, w_first, n_work, sorted_tok3, sorted_tok3, xn, w1, b1, w2, b2)


def _combine_body(posc_ref, posn_ref, x2_ref, gate_ref, ys_hbm, outp_ref, outs_ref, buf, sem, *,
                  n_prompt_tiles):
    i = pl.program_id(0)
    n = pl.num_programs(0)
    slot = i % 2
    nrow = TOP_K * COMB_TT

    def issue(pref, s):
        for r in range(COMB_TT):
            for k in range(TOP_K):
                p = pref[0, 0, TOP_K * r + k]
                pltpu.make_async_copy(ys_hbm.at[pl.ds(p, 1), :],
                                      buf.at[s, pl.ds(k * COMB_TT + r, 1), :], sem.at[s]).start()

    @pl.when(i == 0)
    def _():
        issue(posc_ref, 0)

    pltpu.make_async_copy(ys_hbm.at[pl.ds(0, nrow), :], buf.at[slot], sem.at[slot]).wait()

    @pl.when(i + 1 < n)
    def _():
        issue(posn_ref, 1 - slot)

    y = x2_ref[...]
    g = gate_ref[...]
    for k in range(TOP_K):
        y = y + g[:, k:k + 1] * buf[slot, k * COMB_TT:(k + 1) * COMB_TT, :]

    @pl.when(i < n_prompt_tiles)
    def _():
        outp_ref[...] = y

    @pl.when(i >= n_prompt_tiles)
    def _():
        outs_ref[...] = y


def _combine(pos3, x2, gates, ys, n_prompt):
    n_tok = x2.shape[0]
    nt = n_tok // COMB_TT
    npt = n_prompt // COMB_TT
    assert n_tok - n_prompt == COMB_TT
    body = functools.partial(_combine_body, n_prompt_tiles=npt)
    return pl.pallas_call(
        body,
        grid=(nt,),
        in_specs=[
            pl.BlockSpec((1, 1, TOP_K * COMB_TT), lambda i: (i, 0, 0), memory_space=pltpu.SMEM),
            pl.BlockSpec((1, 1, TOP_K * COMB_TT), lambda i: (jnp.minimum(i + 1, nt - 1), 0, 0),
                         memory_space=pltpu.SMEM),
            pl.BlockSpec((COMB_TT, D_MODEL), lambda i: (i, 0)),
            pl.BlockSpec((COMB_TT, LANES), lambda i: (i, 0)),
            pl.BlockSpec(memory_space=pl.ANY),
        ],
        out_specs=[
            pl.BlockSpec((COMB_TT, D_MODEL), lambda i: (jnp.minimum(i, npt - 1), 0)),
            pl.BlockSpec((COMB_TT, D_MODEL), lambda i: (0, 0)),
        ],
        out_shape=[
            jax.ShapeDtypeStruct((n_prompt, D_MODEL), F32),
            jax.ShapeDtypeStruct((COMB_TT, D_MODEL), F32),
        ],
        scratch_shapes=[
            pltpu.VMEM((2, TOP_K * COMB_TT, D_MODEL), F32),
            pltpu.SemaphoreType.DMA((2,)),
        ],
        compiler_params=_cparams(("arbitrary",), 40),
        name="combine",
    )(pos3, pos3, x2, gates, ys)


def _t5_bucket(rel):
    nb = T5_BUCKETS // 2
    max_exact = nb // 2
    ret = (rel > 0).astype(jnp.int32) * nb
    n = jnp.abs(rel)
    large = max_exact + (jnp.log(jnp.maximum(n, 1).astype(F32) / max_exact)
                         / math.log(T5_MAX_DIST / max_exact) * (nb - max_exact)).astype(jnp.int32)
    large = jnp.minimum(large, nb - 1)
    return ret + jnp.where(n < max_exact, n, large)


def _dup_heads(t):
    lead = t.shape[:-1]
    t = t.reshape(*lead, A_KV_HEADS, 1, HEAD_DIM)
    return jnp.broadcast_to(t, (*lead, A_KV_HEADS, 2, HEAD_DIM)).reshape(*lead, A_KVD)


def _work_items(top_i, rank, counts, n_tok):
    nk = n_tok * TOP_K
    assert nk % MOE_TM == 0
    nblk = nk // MOE_TM
    n_items = nblk + N_EXPERTS
    end = jnp.cumsum(counts)
    start = end - counts
    first_blk = start // MOE_TM
    n_it = jnp.where(counts > 0, (end - 1) // MOE_TM - first_blk + 1, 0)
    it_end = jnp.cumsum(n_it)
    it_start = it_end - n_it
    n_work = it_end[-1]
    experts = jnp.arange(N_EXPERTS, dtype=jnp.int32)
    table = lambda tab, idx: jnp.sum(jnp.where(idx[..., None] == experts, tab, 0), axis=-1)
    w = jnp.arange(n_items, dtype=jnp.int32)
    wc = jnp.minimum(w, n_work - 1)
    w_exp = jnp.minimum(jnp.sum((it_end[None, :] <= wc[:, None]).astype(jnp.int32), axis=1), N_EXPERTS - 1)
    w_blk = table(first_blk, w_exp) + wc - table(it_start, w_exp)
    w_lo = jnp.clip(table(start, w_exp) - w_blk * MOE_TM, 0, MOE_TM)
    w_hi = jnp.clip(table(end, w_exp) - w_blk * MOE_TM, 0, MOE_TM)
    w_first = jnp.concatenate([jnp.ones((1,), jnp.int32), (w_blk[1:] != w_blk[:-1]).astype(jnp.int32)])
    pos = table(start, top_i) + rank
    flat_tok = jnp.arange(nk, dtype=jnp.int32) // TOP_K
    _, sorted_tok = lax.sort_key_val(top_i.reshape(-1), flat_tok, is_stable=True)
    i32 = lambda t: t.astype(jnp.int32)
    work = (i32(w_blk), i32(w_exp), i32(w_lo), i32(w_hi), w_first, i32(n_work).reshape(1))
    return sorted_tok.reshape(nblk, 1, MOE_TM), work, pos.reshape(n_tok // COMB_TT, 1, TOP_K * COMB_TT)


def _band_bias(vals, keys):
    vt = vals.T
    return jnp.stack([vt[:, CHUNK - 1 - i:CHUNK - 1 - i + keys] for i in range(CHUNK)], axis=1).astype(F32)


def kernel(x_prompt, x_sample, cache_a_k, cache_a_v, cache_b_k, cache_b_v, attn_norm, w_in,
           a_q_norm, a_k_norm, b_q_norm, b_k_norm, a_sinks, t5_table, b_rel_table, w_oa, w_ob,
           w_out, ffn_norm, router_w, router_b, w1, b1, w2, b2):
    batch, seq, _ = x_prompt.shape
    dec_b, dec_s, _ = x_sample.shape
    assert attn_norm.shape[0] == 1, "single layer"
    assert dec_s <= CHUNK and PAST_LEN % CHUNK == 0 and PAST_LEN >= PB
    n_prompt = batch * seq
    n_sample = dec_b * dec_s

    wi = w_in[0]
    c = [0]
    for wdt in (A_QW, A_KVW, A_KVW, B_W, B_W, B_W, D_MODEL, D_MODEL):
        c.append(c[-1] + wdt)
    w_qa, w_ka, w_va, w_qb, w_kb, w_vb, w_ga, w_gb = [wi[:, c[k]:c[k + 1]] for k in range(8)]
    w_perm = jnp.concatenate([w_qa, w_qb, w_kb, w_vb, w_ga, w_gb, _dup_heads(w_ka), _dup_heads(w_va)],
                             axis=1).astype(BF16)
    scale = HEAD_DIM ** -0.5 * LOG2E
    ones = lambda n: jnp.ones((n,), F32)
    cg = jnp.concatenate([
        jnp.tile(a_q_norm[0] * scale, A_HEADS), jnp.tile(b_q_norm[0] * scale, B_HEADS),
        jnp.tile(b_k_norm[0], B_HEADS), ones(B_W), ones(2 * D_MODEL),
        jnp.tile(a_k_norm[0], 2 * A_KV_HEADS), ones(A_KVD)]).reshape(1, PROJ_W).astype(F32)
    hd = jnp.arange(MXU_COLS) // HEAD_DIM
    gm = (hd[:, None] == hd[None, :]).astype(BF16)
    g_attn = attn_norm[0].reshape(1, D_MODEL)
    rel_a = jnp.arange(LA + CHUNK - 1) - PA - (CHUNK - 1)
    rel_b = jnp.arange(LB + CHUNK - 1) - PB - (CHUNK - 1)
    bias_a = _band_bias(t5_table[_t5_bucket(rel_a)] * LOG2E, LA)
    bias_b = _band_bias(b_rel_table[0][jnp.clip(rel_b, -B_REL_CLIP, CHUNK - 1) + B_REL_CLIP] * LOG2E, LB)
    sinks = a_sinks[0].astype(F32) * LOG2E
    woa, wob, wout = w_oa[0].astype(BF16), w_ob[0].astype(BF16), w_out[0].astype(BF16)
    fg = ffn_norm[0].reshape(1, D_MODEL)
    wr = jnp.pad(router_w[0], ((0, 0), (0, LANES - N_EXPERTS))).astype(BF16)
    br = jnp.pad(router_b[0], (0, LANES - N_EXPERTS)).reshape(1, LANES)
    w1b, w2b = w1[0].astype(BF16), w2[0].astype(BF16)
    b1r = b1[0].reshape(N_EXPERTS, 1, 2 * D_FF)
    b2r = b2[0].reshape(N_EXPERTS, 1, D_MODEL)

    xp2 = x_prompt.reshape(n_prompt, D_MODEL)
    h_p, kv_p = _proj(xp2, g_attn, w_perm, cg, gm, tm=1024)
    h_p3 = h_p.reshape(batch, seq, PROJ_W)
    cols_p = dict(qa=0, qb=1, kbp=2, kbc=2, vbp=3, vbc=3, kap=16, kac=16, vap=17, vac=17)
    oa_p, ob_p = _attn(sinks, h_p3, h_p3, h_p3, h_p3, h_p3, h_p3, h_p3, h_p3, h_p3, bias_a, bias_b,
                       cpb=8, first_pos=0, hi_a=LA, hi_b=LB, cols=cols_p)
    n_tok = n_prompt + n_sample

    xs_pad = jnp.pad(x_sample, ((0, 0), (0, CHUNK - dec_s), (0, 0))).reshape(dec_b * CHUNK, D_MODEL)
    h_s, kv_s = _proj(xs_pad, g_attn, w_perm, cg, gm, tm=dec_b * CHUNK)
    h_s3 = h_s.reshape(dec_b, CHUNK, PROJ_W)
    ckb = cache_b_k[0].reshape(dec_b, PB, B_W).astype(BF16)
    cvb = cache_b_v[0].reshape(dec_b, PB, B_W).astype(BF16)
    cka = _dup_heads(cache_a_k[0].reshape(dec_b, PA, A_KVW)).astype(BF16)
    cva = _dup_heads(cache_a_v[0].reshape(dec_b, PA, A_KVW)).astype(BF16)
    cols_s = dict(qa=0, qb=1, kbp=0, kbc=2, vbp=0, vbc=3, kap=0, kac=16, vap=0, vac=17)
    oa_s, ob_s = _attn(sinks, h_s3, ckb, h_s3, cvb, h_s3, cka, h_s3, cva, h_s3, bias_a, bias_b,
                       cpb=1, first_pos=PAST_LEN, hi_a=PA + dec_s, hi_b=PB + dec_s, cols=cols_s)
    keep_rows = lambda t: t[:, :dec_s].reshape(n_sample, t.shape[-1])
    x2, xn, logits = _merge(
        (oa_p.reshape(n_prompt, A_QW), ob_p.reshape(n_prompt, B_W), h_p, xp2),
        (keep_rows(oa_s), keep_rows(ob_s), keep_rows(h_s3), x_sample.reshape(n_sample, D_MODEL)),
        woa, wob, wout, fg, wr, br)

    top_i, gates, rank, counts = _router(logits)
    sorted_tok3, work, pos3 = _work_items(top_i[:, :TOP_K], rank[:, :TOP_K], counts[0, :N_EXPERTS], n_tok)
    ys = _experts(work, sorted_tok3, xn, w1b, b1r, w2b, b2r)
    y_p, y_s = _combine(pos3, x2, gates, ys, n_prompt)

    kv_p3 = kv_p.reshape(batch, seq, KV32_W)
    kv_s3 = kv_s.reshape(dec_b, CHUNK, KV32_W)
    undup = lambda t: t.reshape(*t.shape[:-1], A_KV_HEADS, 2, HEAD_DIM)[..., 0, :]
    heads_b = lambda t: t.reshape(*t.shape[:-1], B_HEADS, HEAD_DIM)
    o_kb, o_vb, o_ka, o_va = 0, B_W, 2 * B_W, 2 * B_W + A_KVD
    p_bk = heads_b(kv_p3[:, seq - PB:, o_kb:o_kb + B_W])[None]
    p_bv = heads_b(kv_p3[:, seq - PB:, o_vb:o_vb + B_W])[None]
    p_ak = undup(kv_p3[:, seq - PA:, o_ka:o_ka + A_KVD])[None]
    p_av = undup(kv_p3[:, seq - PA:, o_va:o_va + A_KVD])[None]
    s_bk = heads_b(kv_s3[:, :dec_s, o_kb:o_kb + B_W])[None]
    s_bv = heads_b(kv_s3[:, :dec_s, o_vb:o_vb + B_W])[None]
    s_ak = undup(kv_s3[:, :dec_s, o_ka:o_ka + A_KVD])[None]
    s_av = undup(kv_s3[:, :dec_s, o_va:o_va + A_KVD])[None]
    return (y_p.reshape(batch, seq, D_MODEL), y_s.reshape(dec_b, dec_s, D_MODEL),
            p_ak, p_av, p_bk, p_bv, s_ak, s_av, s_bk, s_bv)
```

```python
import functools
import math

import jax
import jax.numpy as jnp
from jax import lax
from jax.experimental import pallas as pl
from jax.experimental.pallas import tpu as pltpu

F32 = jnp.float32
BF16 = jnp.bfloat16

D_MODEL = 2048
CHUNK = 64
HEAD_DIM = 64
A_HEADS = 16
A_KV_HEADS = 4
A_PAST_CHUNKS = 2
B_HEADS = 16
B_PAST_CHUNKS = 8
B_REL_CLIP = 256
T5_BUCKETS = 32
T5_MAX_DIST = (A_PAST_CHUNKS + 1) * CHUNK
N_EXPERTS = 32
TOP_K = 4
D_FF = D_MODEL
SWIGLU_ALPHA = 1.702
SWIGLU_LIMIT = 7.0
NORM_EPS = 1e-6
NEG_INF = -1e30
LOG2E = math.log2(math.e)
PAST_LEN = 1024

A_QW = A_HEADS * HEAD_DIM
A_KVW = A_KV_HEADS * HEAD_DIM
B_W = B_HEADS * HEAD_DIM
PA = A_PAST_CHUNKS * CHUNK
PB = B_PAST_CHUNKS * CHUNK
LA = PA + CHUNK
LB = PB + CHUNK

LANES = 128
MXU_COLS = 256

A_KVD = 2 * A_KVW
PROJ_W = A_QW + 3 * B_W + 2 * D_MODEL + 2 * A_KVD
PROJ_TN = 1024
KV32_W = 2 * B_W + 2 * A_KVD
_NORM_TILES = (0, 1, 2)
_SIGMOID_LO, _SIGMOID_HI = 4, 8
_KV_TAIL_TILE = 8

MOE_TM = 512
MOE_TF = 512
MOE_NF = D_FF // MOE_TF
MOE_SHARE = MOE_TM // MOE_NF
COMB_TT = 256
ROUTER_TM = 256
MERGE_TM = 256
ATTN_AHEAD = 3


def _cparams(sem, vmem_mb):
    return pltpu.CompilerParams(dimension_semantics=sem, vmem_limit_bytes=vmem_mb * 1024 * 1024)


def _proj_body(x_ref, g_ref, w_ref, cg_ref, gm_ref, h_ref, kv_ref, xn_s):
    j = pl.program_id(1)

    @pl.when(j == 0)
    def _():
        x = x_ref[...]
        ms = jnp.mean(x * x, axis=-1, keepdims=True)
        xn_s[...] = (x * lax.rsqrt(ms + NORM_EPS) * g_ref[...]).astype(BF16)

    is_norm = functools.reduce(jnp.logical_or, [j == t for t in _NORM_TILES])
    is_sig = jnp.logical_and(j >= _SIGMOID_LO, j < _SIGMOID_HI)

    n_strips = PROJ_TN // MXU_COLS

    def strips(epilogues, write_kv, dots_first=False):
        col = [slice(c * MXU_COLS, (c + 1) * MXU_COLS) for c in range(n_strips)]
        mm = lambda cs: jnp.dot(xn_s[...], w_ref[:, cs], preferred_element_type=F32)
        accs = [mm(cs) for cs in col] if dots_first else None
        for c, cs in enumerate(col):
            y = epilogues[c](accs[c] if dots_first else mm(cs), cs)
            h_ref[:, cs] = y.astype(BF16)
            if write_kv:
                kv_ref[:, cs] = y

    def norm(a, cs):
        ss = jnp.dot((a * a).astype(BF16), gm_ref[...], preferred_element_type=F32)
        return a * lax.rsqrt(ss * (1.0 / HEAD_DIM) + NORM_EPS) * cg_ref[:, cs]

    as_is = lambda a, cs: a
    sigmoid = lambda a, cs: 0.5 * jnp.tanh(0.5 * a) + 0.5

    @pl.when(is_norm)
    def _():
        strips([norm] * n_strips, True, dots_first=True)

    @pl.when(is_sig)
    def _():
        strips([sigmoid] * n_strips, False)

    @pl.when(j == _KV_TAIL_TILE)
    def _():
        strips([norm] * (n_strips // 2) + [as_is] * (n_strips // 2), True, dots_first=True)

    @pl.when(j == _SIGMOID_LO - 1)
    def _():
        strips([as_is] * n_strips, True)


def _kv_tile(j):
    return jnp.clip(j - 2, 0, 1) + (j >= _KV_TAIL_TILE).astype(jnp.int32)


def _proj(x2d, g, w_perm, cg, gm, tm):
    n = x2d.shape[0]
    grid = (n // tm, PROJ_W // PROJ_TN)
    return pl.pallas_call(
        _proj_body,
        grid=grid,
        in_specs=[
            pl.BlockSpec((tm, D_MODEL), lambda i, j: (i, 0)),
            pl.BlockSpec((1, D_MODEL), lambda i, j: (0, 0)),
            pl.BlockSpec((D_MODEL, PROJ_TN), lambda i, j: (0, j)),
            pl.BlockSpec((1, PROJ_TN), lambda i, j: (0, j)),
            pl.BlockSpec((MXU_COLS, MXU_COLS), lambda i, j: (0, 0)),
        ],
        out_specs=[
            pl.BlockSpec((tm, PROJ_TN), lambda i, j: (i, j)),
            pl.BlockSpec((tm, PROJ_TN), lambda i, j: (i, _kv_tile(j))),
        ],
        out_shape=[
            jax.ShapeDtypeStruct((n, PROJ_W), BF16),
            jax.ShapeDtypeStruct((n, KV32_W), F32),
        ],
        scratch_shapes=[pltpu.VMEM((tm, D_MODEL), BF16)],
        compiler_params=_cparams(("parallel", "arbitrary"), 56),
        name="proj",
    )(x2d, g, w_perm, cg, gm)


def _attn_body(sink_ref, qa_ref, qb_ref, kbp_ref, kbc_ref, vbp_ref, vbc_ref,
               kap_ref, kac_ref, vap_ref, vac_ref, ba_ref, bb_ref,
               oa_ref, ob_ref, kb_s, vb_s, ka_s, va_s, *, cpb, first_pos, hi_a, hi_b):
    i = pl.program_id(1)
    qb_rows = cpb * CHUNK
    kb_s[0:PB, :] = kbp_ref[0]
    kb_s[PB:PB + qb_rows, :] = kbc_ref[0]
    vb_s[0:PB, :] = vbp_ref[0]
    vb_s[PB:PB + qb_rows, :] = vbc_ref[0]
    ka_s[0:PA, :] = kap_ref[0]
    ka_s[PA:PA + qb_rows, :] = kac_ref[0]
    va_s[0:PA, :] = vap_ref[0]
    va_s[PA:PA + qb_rows, :] = vac_ref[0]

    lane_a = lax.broadcasted_iota(jnp.int32, (1, LA), 1)
    lane_b = lax.broadcasted_iota(jnp.int32, (1, LB), 1)
    low_half = lax.broadcasted_iota(jnp.int32, (1, LANES), 1) < HEAD_DIM
    nt = (((1,), (1,)), ((), ()))

    tasks = [("a", p) for p in range(A_HEADS // 2)] + [("b", p) for p in range(B_HEADS // 2)]
    upper = lax.broadcasted_iota(jnp.int32, (2 * CHUNK, 1), 0) >= CHUNK

    def scores(task, r0, valid_a, valid_b):
        mixer, p = task
        cols = slice(p * LANES, (p + 1) * LANES)
        if mixer == "a":
            kcols = slice((p // 2) * LANES, (p // 2 + 1) * LANES)
            q_p, k_p, b_ref, valid, keys = qa_ref[0, pl.ds(r0, CHUNK), cols], ka_s[pl.ds(r0, LA), kcols], ba_ref, valid_a, LA
        else:
            q_p, k_p, b_ref, valid, keys = qb_ref[0, pl.ds(r0, CHUNK), cols], kb_s[pl.ds(r0, LB), cols], bb_ref, valid_b, LB
        q2 = jnp.concatenate([q_p, q_p], axis=0)
        qm = jnp.where(jnp.logical_xor(upper, low_half), q2, jnp.zeros_like(q2))
        bias = b_ref[2 * p:2 * p + 2].reshape(2 * CHUNK, keys)
        s = lax.dot_general(qm, k_p, nt, preferred_element_type=F32) + bias
        if valid is not None:
            s = jnp.where(valid, s, NEG_INF)
        return s

    def attend(task, s, r0):
        mixer, p = task
        if mixer == "a":
            kcols = slice((p // 2) * LANES, (p // 2 + 1) * LANES)
            v_p, sink = va_s[pl.ds(r0, LA), kcols], jnp.where(upper, sink_ref[2 * p + 1], sink_ref[2 * p])
        else:
            v_p, sink = vb_s[pl.ds(r0, LB), slice(p * LANES, (p + 1) * LANES)], None
        m = jnp.max(s, axis=-1, keepdims=True)
        if sink is not None:
            m = jnp.maximum(m, sink)
        e = jnp.exp2(s - m)
        l = jnp.sum(e, axis=-1, keepdims=True)
        if sink is not None:
            l = l + jnp.exp2(sink - m)
        o = jnp.dot(e.astype(BF16), v_p, preferred_element_type=F32) / l
        return jnp.where(low_half, o[:CHUNK], o[CHUNK:])

    def make_chunk(masked):
        def chunk(jc, carry):
            r0 = pl.multiple_of(jc * CHUNK, CHUNK)
            valid_a = valid_b = None
            if masked:
                start = first_pos + (i * cpb + jc) * CHUNK
                valid_a = jnp.logical_and(lane_a >= jnp.maximum(PA - start, 0), lane_a < hi_a)
                valid_b = jnp.logical_and(lane_b >= jnp.maximum(PB - start, 0), lane_b < hi_b)
            pending = {}
            for t in range(len(tasks) + ATTN_AHEAD):
                if t < len(tasks):
                    pending[t] = scores(tasks[t], r0, valid_a, valid_b)
                d = t - ATTN_AHEAD
                if d >= 0:
                    mixer, p = tasks[d]
                    o_ref = oa_ref if mixer == "a" else ob_ref
                    o_ref[0, pl.ds(r0, CHUNK), p * LANES:(p + 1) * LANES] = (
                        attend(tasks[d], pending.pop(d), r0).astype(BF16))
            return carry
        return chunk

    n_masked_blocks = pl.cdiv(max(PB - first_pos, 0), cpb * CHUNK)
    if hi_a < LA or hi_b < LB:
        lax.fori_loop(0, cpb, make_chunk(True), 0)
    else:
        @pl.when(i < n_masked_blocks)
        def _():
            lax.fori_loop(0, cpb, make_chunk(True), 0)

        @pl.when(i >= n_masked_blocks)
        def _():
            lax.fori_loop(0, cpb, make_chunk(False), 0)


def _attn(sinks, hq, kbp, kbc, vbp, vbc, kap, kac, vap, vac, bias_a, bias_b, *,
          cpb, first_pos, hi_a, hi_b, cols):
    nb, s, _ = hq.shape
    qb_rows = cpb * CHUNK
    nblk = s // qb_rows
    rb = qb_rows // PB if qb_rows >= PB else None
    ra = qb_rows // PA if qb_rows >= PA else None

    def prev_map(ratio, col):
        if ratio is None:
            return lambda b, i, *_: (b, 0, col)
        return lambda b, i, *_: (b, jnp.maximum(i * ratio - 1, 0), col)

    def cur_map(col):
        return lambda b, i, *_: (b, i, col)

    in_specs = [
        pl.BlockSpec((1, qb_rows, A_QW), cur_map(cols["qa"])),
        pl.BlockSpec((1, qb_rows, B_W), cur_map(cols["qb"])),
        pl.BlockSpec((1, PB, B_W), prev_map(rb, cols["kbp"])),
        pl.BlockSpec((1, qb_rows, B_W), cur_map(cols["kbc"])),
        pl.BlockSpec((1, PB, B_W), prev_map(rb, cols["vbp"])),
        pl.BlockSpec((1, qb_rows, B_W), cur_map(cols["vbc"])),
        pl.BlockSpec((1, PA, A_KVD), prev_map(ra, cols["kap"])),
        pl.BlockSpec((1, qb_rows, A_KVD), cur_map(cols["kac"])),
        pl.BlockSpec((1, PA, A_KVD), prev_map(ra, cols["vap"])),
        pl.BlockSpec((1, qb_rows, A_KVD), cur_map(cols["vac"])),
        pl.BlockSpec((A_HEADS, CHUNK, LA), lambda b, i, *_: (0, 0, 0)),
        pl.BlockSpec((B_HEADS, CHUNK, LB), lambda b, i, *_: (0, 0, 0)),
    ]
    out_specs = [
        pl.BlockSpec((1, qb_rows, A_QW), lambda b, i, *_: (b, i, 0)),
        pl.BlockSpec((1, qb_rows, B_W), lambda b, i, *_: (b, i, 0)),
    ]
    body = functools.partial(_attn_body, cpb=cpb, first_pos=first_pos, hi_a=hi_a, hi_b=hi_b)
    return pl.pallas_call(
        body,
        grid_spec=pltpu.PrefetchScalarGridSpec(
            num_scalar_prefetch=1,
            grid=(nb, nblk),
            in_specs=in_specs,
            out_specs=out_specs,
            scratch_shapes=[
                pltpu.VMEM((PB + qb_rows, B_W), BF16),
                pltpu.VMEM((PB + qb_rows, B_W), BF16),
                pltpu.VMEM((PA + qb_rows, A_KVD), BF16),
                pltpu.VMEM((PA + qb_rows, A_KVD), BF16),
            ],
        ),
        out_shape=[
            jax.ShapeDtypeStruct((nb, s, A_QW), BF16),
            jax.ShapeDtypeStruct((nb, s, B_W), BF16),
        ],
        compiler_params=_cparams(("parallel", "arbitrary"), 48),
        name="attn",
    )(sinks, hq, hq, kbp, kbc, vbp, vbc, kap, kac, vap, vac, bias_a, bias_b)


def _merge_body(oa_p, ob_p, ga_p, gb_p, x_p, oa_s, ob_s, ga_s, gb_s, x_s,
                woa_ref, wob_ref, wout_ref, fg_ref, wr_ref, br_ref, x2_ref, xn_ref, lg_ref, *,
                n_prompt_tiles):
    def tile(oa_ref, ob_ref, ga_ref, gb_ref, x_ref):
        ya = jnp.dot(oa_ref[...], woa_ref[...], preferred_element_type=F32)
        yb = jnp.dot(ob_ref[...], wob_ref[...], preferred_element_type=F32)
        z = ga_ref[...].astype(F32) * ya + gb_ref[...].astype(F32) * yb
        y = jnp.dot(z.astype(BF16), wout_ref[...], preferred_element_type=F32)
        x2 = x_ref[...] + y
        x2_ref[...] = x2
        ms = jnp.mean(x2 * x2, axis=-1, keepdims=True)
        xn = x2 * lax.rsqrt(ms + NORM_EPS) * fg_ref[...]
        xn_ref[...] = xn
        lg_ref[...] = jnp.dot(xn.astype(BF16), wr_ref[...], preferred_element_type=F32) + br_ref[...]

    i = pl.program_id(0)

    @pl.when(i < n_prompt_tiles)
    def _():
        tile(oa_p, ob_p, ga_p, gb_p, x_p)

    @pl.when(i >= n_prompt_tiles)
    def _():
        tile(oa_s, ob_s, ga_s, gb_s, x_s)


def _merge(prompt, sample, w_oa, w_ob, w_out, fg, wr, br):
    tm = MERGE_TM
    n_p, n_s = prompt[3].shape[0], sample[3].shape[0]
    npt, nst = n_p // tm, n_s // tm
    n_out = n_p + n_s
    const = lambda i: (0, 0)
    resident = functools.partial(pl.BlockSpec, index_map=const, pipeline_mode=pl.Buffered(1))

    def token_specs(row):
        return [
            pl.BlockSpec((tm, A_QW), lambda i: (row(i), 0)),
            pl.BlockSpec((tm, B_W), lambda i: (row(i), 0)),
            pl.BlockSpec((tm, D_MODEL), lambda i: (row(i), 2)),
            pl.BlockSpec((tm, D_MODEL), lambda i: (row(i), 3)),
            pl.BlockSpec((tm, D_MODEL), lambda i: (row(i), 0)),
        ]

    in_specs = (token_specs(lambda i: jnp.minimum(i, npt - 1))
                + token_specs(lambda i: jnp.clip(i - npt, 0, nst - 1))
                + [resident((A_QW, D_MODEL)), resident((B_W, D_MODEL)), resident((D_MODEL, D_MODEL)),
                   resident((1, D_MODEL)), resident((D_MODEL, LANES)), resident((1, LANES))])
    oa_p, ob_p, h_p, x_p = prompt
    oa_s, ob_s, h_s, x_s = sample
    return pl.pallas_call(
        functools.partial(_merge_body, n_prompt_tiles=npt),
        grid=(npt + nst,),
        in_specs=in_specs,
        out_specs=[
            pl.BlockSpec((tm, D_MODEL), lambda i: (i, 0)),
            pl.BlockSpec((tm, D_MODEL), lambda i: (i, 0)),
            pl.BlockSpec((tm, LANES), lambda i: (i, 0)),
        ],
        out_shape=[
            jax.ShapeDtypeStruct((n_out, D_MODEL), F32),
            jax.ShapeDtypeStruct((n_out, D_MODEL), F32),
            jax.ShapeDtypeStruct((n_out, LANES), F32),
        ],
        compiler_params=_cparams(("arbitrary",), 56),
        name="merge",
    )(oa_p, ob_p, h_p, h_p, x_p, oa_s, ob_s, h_s, h_s, x_s, w_oa, w_ob, w_out, fg, wr, br)


def _router_body(lg_ref, tri_ref, idx_ref, gate_ref, rank_ref, cnt_ref, carry):
    i = pl.program_id(0)

    @pl.when(i == 0)
    def _():
        carry[...] = jnp.zeros_like(carry)

    lane = lax.broadcasted_iota(jnp.int32, lg_ref.shape, 1)
    x = jnp.where(lane < N_EXPERTS, lg_ref[...], -jnp.inf)
    vals, hots = [], []
    idx_out = jnp.zeros(lg_ref.shape, jnp.int32)
    for k in range(TOP_K):
        m = jnp.max(x, axis=-1, keepdims=True)
        am = jnp.min(jnp.where(x == m, lane, LANES), axis=-1, keepdims=True)
        hot = lane == am
        vals.append(m)
        hots.append(hot)
        idx_out = jnp.where(lane == k, am, idx_out)
        x = jnp.where(hot, -jnp.inf, x)
    es = [jnp.exp(v - vals[0]) for v in vals]
    denom = functools.reduce(lambda a, b: a + b, es)
    sel = functools.reduce(jnp.logical_or, hots)
    sel_f = jnp.where(sel, 1.0, 0.0)
    before = jnp.dot(tri_ref[...], sel_f.astype(BF16), preferred_element_type=F32) + carry[...]
    gate_out = jnp.zeros(lg_ref.shape, F32)
    rank_out = jnp.zeros(lg_ref.shape, F32)
    for k in range(TOP_K):
        gate_out = jnp.where(lane == k, es[k] / denom, gate_out)
        rk = jnp.sum(jnp.where(hots[k], before, 0.0), axis=-1, keepdims=True)
        rank_out = jnp.where(lane == k, rk, rank_out)
    idx_ref[...] = idx_out
    gate_ref[...] = gate_out
    rank_ref[...] = rank_out.astype(jnp.int32)
    total = carry[...] + jnp.sum(sel_f, axis=0, keepdims=True)
    carry[...] = total
    cnt_ref[...] = total.astype(jnp.int32)


def _router(logits):
    n = logits.shape[0]
    tm = ROUTER_TM
    r = jnp.arange(tm)
    tri = (r[None, :] < r[:, None]).astype(BF16)
    tile = pl.BlockSpec((tm, LANES), lambda i: (i, 0))
    return pl.pallas_call(
        _router_body,
        grid=(n // tm,),
        in_specs=[tile, pl.BlockSpec((tm, tm), lambda i: (0, 0))],
        out_specs=[tile, tile, tile, pl.BlockSpec((1, LANES), lambda i: (0, 0))],
        out_shape=[
            jax.ShapeDtypeStruct((n, LANES), jnp.int32),
            jax.ShapeDtypeStruct((n, LANES), F32),
            jax.ShapeDtypeStruct((n, LANES), jnp.int32),
            jax.ShapeDtypeStruct((1, LANES), jnp.int32),
        ],
        scratch_shapes=[pltpu.VMEM((1, LANES), F32)],
        compiler_params=_cparams(("arbitrary",), 16),
        name="router",
    )(logits, tri)


def _row_gather(idx_ref, n, src_hbm, dst, sem):
    def body(r, c):
        t = idx_ref[0, 0, r]
        pltpu.make_async_copy(src_hbm.at[pl.ds(t, 1), :], dst.at[pl.ds(r, 1), :], sem).start()
        return c
    lax.fori_loop(0, n, body, 0)


def _expert_body(wblk_ref, wexp_ref, wlo_ref, whi_ref, wfirst_ref, nv_ref, tokc_ref, tokn_ref,
                 xn_hbm, w1_hbm, b1_ref, w2_ref, b2_ref,
                 out_ref, xg, xb, act_s, w1buf, sem_x, sem_w, *, n_items):
    i = pl.program_id(0)
    nf = MOE_NF
    nv = nv_ref[0]
    slot = i % 2
    e_cur = wexp_ref[i]
    e_next = wexp_ref[jnp.minimum(i + 1, n_items - 1)]

    def rows_landed(s):
        return pltpu.make_async_copy(xn_hbm.at[pl.ds(0, MOE_TM), :], xg.at[s], sem_x.at[s])

    def w1_tile(e, f, s):
        return [pltpu.make_async_copy(w1_hbm.at[e, :, pl.ds(half * D_FF + f * MOE_TF, MOE_TF)],
                                      w1buf.at[s, half], sem_w.at[s]) for half in range(2)]

    @pl.when(i == nv)
    def _():
        rows_landed(slot).wait()
        for c in w1_tile(e_cur, 0, 0):
            c.wait()

    @pl.when(i == 0)
    def _():
        _row_gather(tokc_ref, MOE_TM, xn_hbm, xg.at[0], sem_x.at[0])
        for c in w1_tile(e_cur, 0, 0):
            c.start()

    @pl.when(i < nv)
    def _():
        rows_landed(slot).wait()
        xb[...] = xg[slot].astype(BF16)
        x = xb[...]
        for f in range(nf):
            s = f % 2
            for c in w1_tile(e_cur, f, s):
                c.wait()
            nxt = w1_tile(e_cur, f + 1, 1 - s) if f + 1 < nf else w1_tile(e_next, 0, 1 - s)
            for c in nxt:
                c.start()
            for r in range(f * MOE_SHARE, (f + 1) * MOE_SHARE):
                t = tokn_ref[0, 0, r]
                pltpu.make_async_copy(xn_hbm.at[pl.ds(t, 1), :], xg.at[1 - slot, pl.ds(r, 1), :],
                                      sem_x.at[1 - slot]).start()
            for c in range(MOE_TF // MXU_COLS):
                cs = slice(c * MXU_COLS, (c + 1) * MXU_COLS)
                gs = slice(f * MOE_TF + c * MXU_COLS, f * MOE_TF + (c + 1) * MXU_COLS)
                us = slice(D_FF + gs.start, D_FF + gs.stop)
                hg = jnp.dot(x, w1buf[s, 0, :, cs], preferred_element_type=F32) + b1_ref[0, :, gs]
                hu = jnp.dot(x, w1buf[s, 1, :, cs], preferred_element_type=F32) + b1_ref[0, :, us]
                hg = jnp.minimum(hg, SWIGLU_LIMIT)
                hu = jnp.clip(hu, -SWIGLU_LIMIT, SWIGLU_LIMIT)
                sig = 0.5 * jnp.tanh((0.5 * SWIGLU_ALPHA) * hg) + 0.5
                act_s[:, gs] = (hg * sig * (hu + 1.0)).astype(BF16)

    def ffn_out():
        return jnp.dot(act_s[...], w2_ref[0], preferred_element_type=F32) + b2_ref[0]

    @pl.when(jnp.logical_and(i < nv, wfirst_ref[i] == 1))
    def _():
        out_ref[...] = ffn_out()

    @pl.when(jnp.logical_and(i < nv, wfirst_ref[i] == 0))
    def _():
        row = lax.broadcasted_iota(jnp.int32, (MOE_TM, 1), 0)
        mine = jnp.logical_and(row >= wlo_ref[i], row < whi_ref[i])
        out_ref[...] = jnp.where(mine, ffn_out(), out_ref[...])


def _experts(work, sorted_tok3, xn, w1, b1, w2, b2):
    w_blk, w_exp, w_lo, w_hi, w_first, n_work = work
    n_items = w_blk.shape[0]
    nblk = sorted_tok3.shape[0]
    assert MOE_NF % 2 == 0

    def tok_map(shift):
        return lambda i, wb, *_: (wb[jnp.minimum(i + shift, n_items - 1)], 0, 0)

    by_expert = lambda i, wb, we, *_: (we[i], 0, 0)
    in_specs = [
        pl.BlockSpec((1, 1, MOE_TM), tok_map(0), memory_space=pltpu.SMEM),
        pl.BlockSpec((1, 1, MOE_TM), tok_map(1), memory_space=pltpu.SMEM),
        pl.BlockSpec(memory_space=pl.ANY),
        pl.BlockSpec(memory_space=pl.ANY),
        pl.BlockSpec((1, 1, 2 * D_FF), by_expert),
        pl.BlockSpec((1, D_FF, D_MODEL), by_expert),
        pl.BlockSpec((1, 1, D_MODEL), by_expert),
    ]
    return pl.pallas_call(
        functools.partial(_expert_body, n_items=n_items),
        grid_spec=pltpu.PrefetchScalarGridSpec(
            num_scalar_prefetch=6,
            grid=(n_items,),
            in_specs=in_specs,
            out_specs=pl.BlockSpec((MOE_TM, D_MODEL), lambda i, wb, *_: (wb[i], 0)),
            scratch_shapes=[
                pltpu.VMEM((2, MOE_TM, D_MODEL), F32),
                pltpu.VMEM((MOE_TM, D_MODEL), BF16),
                pltpu.VMEM((MOE_TM, D_FF), BF16),
                pltpu.VMEM((2, 2, D_MODEL, MOE_TF), BF16),
                pltpu.SemaphoreType.DMA((2,)),
                pltpu.SemaphoreType.DMA((2,)),
            ],
        ),
        out_shape=jax.ShapeDtypeStruct((nblk * MOE_TM, D_MODEL), F32),
        compiler_params=_cparams(("arbitrary",), 56),
        name="experts",
    )(w_blk, w_exp, w_lo, w_hi, w_first, n_work, sorted_tok3, sorted_tok3, xn, w1, b1, w2, b2)


def _combine_body(posc_ref, posn_ref, x2_ref, gate_ref, ys_hbm, outp_ref, outs_ref, buf, sem, *,
                  n_prompt_tiles):
    i = pl.program_id(0)
    n = pl.num_programs(0)
    slot = i % 2
    nrow = TOP_K * COMB_TT

    def issue(pref, s):
        for r in range(COMB_TT):
            for k in range(TOP_K):
                p = pref[0, 0, TOP_K * r + k]
                pltpu.make_async_copy(ys_hbm.at[pl.ds(p, 1), :],
                                      buf.at[s, pl.ds(k * COMB_TT + r, 1), :], sem.at[s]).start()

    @pl.when(i == 0)
    def _():
        issue(posc_ref, 0)

    pltpu.make_async_copy(ys_hbm.at[pl.ds(0, nrow), :], buf.at[slot], sem.at[slot]).wait()

    @pl.when(i + 1 < n)
    def _():
        issue(posn_ref, 1 - slot)

    y = x2_ref[...]
    g = gate_ref[...]
    for k in range(TOP_K):
        y = y + g[:, k:k + 1] * buf[slot, k * COMB_TT:(k + 1) * COMB_TT, :]

    @pl.when(i < n_prompt_tiles)
    def _():
        outp_ref[...] = y

    @pl.when(i >= n_prompt_tiles)
    def _():
        outs_ref[...] = y


def _combine(pos3, x2, gates, ys, n_prompt):
    n_tok = x2.shape[0]
    nt = n_tok // COMB_TT
    npt = n_prompt // COMB_TT
    assert n_tok - n_prompt == COMB_TT
    body = functools.partial(_combine_body, n_prompt_tiles=npt)
    return pl.pallas_call(
        body,
        grid=(nt,),
        in_specs=[
            pl.BlockSpec((1, 1, TOP_K * COMB_TT), lambda i: (i, 0, 0), memory_space=pltpu.SMEM),
            pl.BlockSpec((1, 1, TOP_K * COMB_TT), lambda i: (jnp.minimum(i + 1, nt - 1), 0, 0),
                         memory_space=pltpu.SMEM),
            pl.BlockSpec((COMB_TT, D_MODEL), lambda i: (i, 0)),
            pl.BlockSpec((COMB_TT, LANES), lambda i: (i, 0)),
            pl.BlockSpec(memory_space=pl.ANY),
        ],
        out_specs=[
            pl.BlockSpec((COMB_TT, D_MODEL), lambda i: (jnp.minimum(i, npt - 1), 0)),
            pl.BlockSpec((COMB_TT, D_MODEL), lambda i: (0, 0)),
        ],
        out_shape=[
            jax.ShapeDtypeStruct((n_prompt, D_MODEL), F32),
            jax.ShapeDtypeStruct((COMB_TT, D_MODEL), F32),
        ],
        scratch_shapes=[
            pltpu.VMEM((2, TOP_K * COMB_TT, D_MODEL), F32),
            pltpu.SemaphoreType.DMA((2,)),
        ],
        compiler_params=_cparams(("arbitrary",), 40),
        name="combine",
    )(pos3, pos3, x2, gates, ys)


def _t5_bucket(rel):
    nb = T5_BUCKETS // 2
    max_exact = nb // 2
    ret = (rel > 0).astype(jnp.int32) * nb
    n = jnp.abs(rel)
    large = max_exact + (jnp.log(jnp.maximum(n, 1).astype(F32) / max_exact)
                         / math.log(T5_MAX_DIST / max_exact) * (nb - max_exact)).astype(jnp.int32)
    large = jnp.minimum(large, nb - 1)
    return ret + jnp.where(n < max_exact, n, large)


def _dup_heads(t):
    lead = t.shape[:-1]
    t = t.reshape(*lead, A_KV_HEADS, 1, HEAD_DIM)
    return jnp.broadcast_to(t, (*lead, A_KV_HEADS, 2, HEAD_DIM)).reshape(*lead, A_KVD)


def _work_items(top_i, rank, counts, n_tok):
    nk = n_tok * TOP_K
    assert nk % MOE_TM == 0
    nblk = nk // MOE_TM
    n_items = nblk + N_EXPERTS
    end = jnp.cumsum(counts)
    start = end - counts
    first_blk = start // MOE_TM
    n_it = jnp.where(counts > 0, (end - 1) // MOE_TM - first_blk + 1, 0)
    it_end = jnp.cumsum(n_it)
    it_start = it_end - n_it
    n_work = it_end[-1]
    experts = jnp.arange(N_EXPERTS, dtype=jnp.int32)
    table = lambda tab, idx: jnp.sum(jnp.where(idx[..., None] == experts, tab, 0), axis=-1)
    w = jnp.arange(n_items, dtype=jnp.int32)
    wc = jnp.minimum(w, n_work - 1)
    w_exp = jnp.minimum(jnp.sum((it_end[None, :] <= wc[:, None]).astype(jnp.int32), axis=1), N_EXPERTS - 1)
    w_blk = table(first_blk, w_exp) + wc - table(it_start, w_exp)
    w_lo = jnp.clip(table(start, w_exp) - w_blk * MOE_TM, 0, MOE_TM)
    w_hi = jnp.clip(table(end, w_exp) - w_blk * MOE_TM, 0, MOE_TM)
    w_first = jnp.concatenate([jnp.ones((1,), jnp.int32), (w_blk[1:] != w_blk[:-1]).astype(jnp.int32)])
    pos = table(start, top_i) + rank
    flat_tok = jnp.arange(nk, dtype=jnp.int32) // TOP_K
    _, sorted_tok = lax.sort_key_val(top_i.reshape(-1), flat_tok, is_stable=True)
    i32 = lambda t: t.astype(jnp.int32)
    work = (i32(w_blk), i32(w_exp), i32(w_lo), i32(w_hi), w_first, i32(n_work).reshape(1))
    return sorted_tok.reshape(nblk, 1, MOE_TM), work, pos.reshape(n_tok // COMB_TT, 1, TOP_K * COMB_TT)


def _band_bias(vals, keys):
    vt = vals.T
    return jnp.stack([vt[:, CHUNK - 1 - i:CHUNK - 1 - i + keys] for i in range(CHUNK)], axis=1).astype(F32)


def kernel(x_prompt, x_sample, cache_a_k, cache_a_v, cache_b_k, cache_b_v, attn_norm, w_in,
           a_q_norm, a_k_norm, b_q_norm, b_k_norm, a_sinks, t5_table, b_rel_table, w_oa, w_ob,
           w_out, ffn_norm, router_w, router_b, w1, b1, w2, b2):
    batch, seq, _ = x_prompt.shape
    dec_b, dec_s, _ = x_sample.shape
    assert attn_norm.shape[0] == 1, "single layer"
    assert dec_s <= CHUNK and PAST_LEN % CHUNK == 0 and PAST_LEN >= PB
    n_prompt = batch * seq
    n_sample = dec_b * dec_s

    wi = w_in[0]
    c = [0]
    for wdt in (A_QW, A_KVW, A_KVW, B_W, B_W, B_W, D_MODEL, D_MODEL):
        c.append(c[-1] + wdt)
    w_qa, w_ka, w_va, w_qb, w_kb, w_vb, w_ga, w_gb = [wi[:, c[k]:c[k + 1]] for k in range(8)]
    w_perm = jnp.concatenate([w_qa, w_qb, w_kb, w_vb, w_ga, w_gb, _dup_heads(w_ka), _dup_heads(w_va)],
                             axis=1).astype(BF16)
    scale = HEAD_DIM ** -0.5 * LOG2E
    ones = lambda n: jnp.ones((n,), F32)
    cg = jnp.concatenate([
        jnp.tile(a_q_norm[0] * scale, A_HEADS), jnp.tile(b_q_norm[0] * scale, B_HEADS),
        jnp.tile(b_k_norm[0], B_HEADS), ones(B_W), ones(2 * D_MODEL),
        jnp.tile(a_k_norm[0], 2 * A_KV_HEADS), ones(A_KVD)]).reshape(1, PROJ_W).astype(F32)
    hd = jnp.arange(MXU_COLS) // HEAD_DIM
    gm = (hd[:, None] == hd[None, :]).astype(BF16)
    g_attn = attn_norm[0].reshape(1, D_MODEL)
    rel_a = jnp.arange(LA + CHUNK - 1) - PA - (CHUNK - 1)
    rel_b = jnp.arange(LB + CHUNK - 1) - PB - (CHUNK - 1)
    bias_a = _band_bias(t5_table[_t5_bucket(rel_a)] * LOG2E, LA)
    bias_b = _band_bias(b_rel_table[0][jnp.clip(rel_b, -B_REL_CLIP, CHUNK - 1) + B_REL_CLIP] * LOG2E, LB)
    sinks = a_sinks[0].astype(F32) * LOG2E
    woa, wob, wout = w_oa[0].astype(BF16), w_ob[0].astype(BF16), w_out[0].astype(BF16)
    fg = ffn_norm[0].reshape(1, D_MODEL)
    wr = jnp.pad(router_w[0], ((0, 0), (0, LANES - N_EXPERTS))).astype(BF16)
    br = jnp.pad(router_b[0], (0, LANES - N_EXPERTS)).reshape(1, LANES)
    w1b, w2b = w1[0].astype(BF16), w2[0].astype(BF16)
    b1r = b1[0].reshape(N_EXPERTS, 1, 2 * D_FF)
    b2r = b2[0].reshape(N_EXPERTS, 1, D_MODEL)

    xp2 = x_prompt.reshape(n_prompt, D_MODEL)
    h_p, kv_p = _proj(xp2, g_attn, w_perm, cg, gm, tm=1024)
    h_p3 = h_p.reshape(batch, seq, PROJ_W)
    cols_p = dict(qa=0, qb=1, kbp=2, kbc=2, vbp=3, vbc=3, kap=16, kac=16, vap=17, vac=17)
    oa_p, ob_p = _attn(sinks, h_p3, h_p3, h_p3, h_p3, h_p3, h_p3, h_p3, h_p3, h_p3, bias_a, bias_b,
                       cpb=8, first_pos=0, hi_a=LA, hi_b=LB, cols=cols_p)
    n_tok = n_prompt + n_sample

    xs_pad = jnp.pad(x_sample, ((0, 0), (0, CHUNK - dec_s), (0, 0))).reshape(dec_b * CHUNK, D_MODEL)
    h_s, kv_s = _proj(xs_pad, g_attn, w_perm, cg, gm, tm=dec_b * CHUNK)
    h_s3 = h_s.reshape(dec_b, CHUNK, PROJ_W)
    ckb = cache_b_k[0].reshape(dec_b, PB, B_W).astype(BF16)
    cvb = cache_b_v[0].reshape(dec_b, PB, B_W).astype(BF16)
    cka = _dup_heads(cache_a_k[0].reshape(dec_b, PA, A_KVW)).astype(BF16)
    cva = _dup_heads(cache_a_v[0].reshape(dec_b, PA, A_KVW)).astype(BF16)
    cols_s = dict(qa=0, qb=1, kbp=0, kbc=2, vbp=0, vbc=3, kap=0, kac=16, vap=0, vac=17)
    oa_s, ob_s = _attn(sinks, h_s3, ckb, h_s3, cvb, h_s3, cka, h_s3, cva, h_s3, bias_a, bias_b,
                       cpb=1, first_pos=PAST_LEN, hi_a=PA + dec_s, hi_b=PB + dec_s, cols=cols_s)
    keep_rows = lambda t: t[:, :dec_s].reshape(n_sample, t.shape[-1])
    x2, xn, logits = _merge(
        (oa_p.reshape(n_prompt, A_QW), ob_p.reshape(n_prompt, B_W), h_p, xp2),
        (keep_rows(oa_s), keep_rows(ob_s), keep_rows(h_s3), x_sample.reshape(n_sample, D_MODEL)),
        woa, wob, wout, fg, wr, br)

    top_i, gates, rank, counts = _router(logits)
    sorted_tok3, work, pos3 = _work_items(top_i[:, :TOP_K], rank[:, :TOP_K], counts[0, :N_EXPERTS], n_tok)
    ys = _experts(work, sorted_tok3, xn, w1b, b1r, w2b, b2r)
    y_p, y_s = _combine(pos3, x2, gates, ys, n_prompt)

    kv_p3 = kv_p.reshape(batch, seq, KV32_W)
    kv_s3 = kv_s.reshape(dec_b, CHUNK, KV32_W)
    undup = lambda t: t.reshape(*t.shape[:-1], A_KV_HEADS, 2, HEAD_DIM)[..., 0, :]
    heads_b = lambda t: t.reshape(*t.shape[:-1], B_HEADS, HEAD_DIM)
    o_kb, o_vb, o_ka, o_va = 0, B_W, 2 * B_W, 2 * B_W + A_KVD
    p_bk = heads_b(kv_p3[:, seq - PB:, o_kb:o_kb + B_W])[None]
    p_bv = heads_b(kv_p3[:, seq - PB:, o_vb:o_vb + B_W])[None]
    p_ak = undup(kv_p3[:, seq - PA:, o_ka:o_ka + A_KVD])[None]
    p_av = undup(kv_p3[:, seq - PA:, o_va:o_va + A_KVD])[None]
    s_bk = heads_b(kv_s3[:, :dec_s, o_kb:o_kb + B_W])[None]
    s_bv = heads_b(kv_s3[:, :dec_s, o_vb:o_vb + B_W])[None]
    s_ak = undup(kv_s3[:, :dec_s, o_ka:o_ka + A_KVD])[None]
    s_av = undup(kv_s3[:, :dec_s, o_va:o_va + A_KVD])[None]
    return (y_p.reshape(batch, seq, D_MODEL), y_s.reshape(dec_b, dec_s, D_MODEL),
            p_ak, p_av, p_bk, p_bv, s_ak, s_av, s_bk, s_bv)
```

```python
import functools
import math

import jax
import jax.numpy as jnp
from jax import lax
from jax.experimental import pallas as pl
from jax.experimental.pallas import tpu as pltpu

F32 = jnp.float32
BF16 = jnp.bfloat16

D_MODEL = 2048
CHUNK = 64
HEAD_DIM = 64
A_HEADS = 16
A_KV_HEADS = 4
A_PAST_CHUNKS = 2
B_HEADS = 16
B_PAST_CHUNKS = 8
B_REL_CLIP = 256
T5_BUCKETS = 32
T5_MAX_DIST = (A_PAST_CHUNKS + 1) * CHUNK
N_EXPERTS = 32
TOP_K = 4
D_FF = D_MODEL
SWIGLU_ALPHA = 1.702
SWIGLU_LIMIT = 7.0
NORM_EPS = 1e-6
NEG_INF = -1e30
LOG2E = math.log2(math.e)
PAST_LEN = 1024

A_QW = A_HEADS * HEAD_DIM
A_KVW = A_KV_HEADS * HEAD_DIM
B_W = B_HEADS * HEAD_DIM
PA = A_PAST_CHUNKS * CHUNK
PB = B_PAST_CHUNKS * CHUNK
LA = PA + CHUNK
LB = PB + CHUNK

LANES = 128
MXU_COLS = 256

A_KVD = 2 * A_KVW
PROJ_W = A_QW + 3 * B_W + 2 * D_MODEL + 2 * A_KVD
PROJ_TN = 1024
KV32_W = 2 * B_W + 2 * A_KVD
_NORM_TILES = (0, 1, 2)
_SIGMOID_LO, _SIGMOID_HI = 4, 8
_KV_TAIL_TILE = 8

MOE_TM = 512
MOE_TF = 512
MOE_NF = D_FF // MOE_TF
MOE_SHARE = MOE_TM // MOE_NF
COMB_TT = 256
ROUTER_TM = 256
MERGE_TM = 256
ATTN_AHEAD = 3


def _cparams(sem, vmem_mb):
    return pltpu.CompilerParams(dimension_semantics=sem, vmem_limit_bytes=vmem_mb * 1024 * 1024)


def _proj_body(x_ref, g_ref, w_ref, cg_ref, gm_ref, h_ref, kv_ref, xn_s):
    j = pl.program_id(1)

    @pl.when(j == 0)
    def _():
        x = x_ref[...]
        ms = jnp.mean(x * x, axis=-1, keepdims=True)
        xn_s[...] = (x * lax.rsqrt(ms + NORM_EPS) * g_ref[...]).astype(BF16)

    is_norm = functools.reduce(jnp.logical_or, [j == t for t in _NORM_TILES])
    is_sig = jnp.logical_and(j >= _SIGMOID_LO, j < _SIGMOID_HI)

    n_strips = PROJ_TN // MXU_COLS

    def strips(epilogues, write_kv, dots_first=False):
        col = [slice(c * MXU_COLS, (c + 1) * MXU_COLS) for c in range(n_strips)]
        mm = lambda cs: jnp.dot(xn_s[...], w_ref[:, cs], preferred_element_type=F32)
        accs = [mm(cs) for cs in col] if dots_first else None
        for c, cs in enumerate(col):
            y = epilogues[c](accs[c] if dots_first else mm(cs), cs)
            h_ref[:, cs] = y.astype(BF16)
            if write_kv:
                kv_ref[:, cs] = y

    def norm(a, cs):
        ss = jnp.dot((a * a).astype(BF16), gm_ref[...], preferred_element_type=F32)
        return a * lax.rsqrt(ss * (1.0 / HEAD_DIM) + NORM_EPS) * cg_ref[:, cs]

    as_is = lambda a, cs: a
    sigmoid = lambda a, cs: 0.5 * jnp.tanh(0.5 * a) + 0.5

    @pl.when(is_norm)
    def _():
        strips([norm] * n_strips, True, dots_first=True)

    @pl.when(is_sig)
    def _():
        strips([sigmoid] * n_strips, False)

    @pl.when(j == _KV_TAIL_TILE)
    def _():
        strips([norm] * (n_strips // 2) + [as_is] * (n_strips // 2), True, dots_first=True)

    @pl.when(j == _SIGMOID_LO - 1)
    def _():
        strips([as_is] * n_strips, True)


def _kv_tile(j):
    return jnp.clip(j - 2, 0, 1) + (j >= _KV_TAIL_TILE).astype(jnp.int32)


def _proj(x2d, g, w_perm, cg, gm, tm):
    n = x2d.shape[0]
    grid = (n // tm, PROJ_W // PROJ_TN)
    return pl.pallas_call(
        _proj_body,
        grid=grid,
        in_specs=[
            pl.BlockSpec((tm, D_MODEL), lambda i, j: (i, 0)),
            pl.BlockSpec((1, D_MODEL), lambda i, j: (0, 0)),
            pl.BlockSpec((D_MODEL, PROJ_TN), lambda i, j: (0, j)),
            pl.BlockSpec((1, PROJ_TN), lambda i, j: (0, j)),
            pl.BlockSpec((MXU_COLS, MXU_COLS), lambda i, j: (0, 0)),
        ],
        out_specs=[
            pl.BlockSpec((tm, PROJ_TN), lambda i, j: (i, j)),
            pl.BlockSpec((tm, PROJ_TN), lambda i, j: (i, _kv_tile(j))),
        ],
        out_shape=[
            jax.ShapeDtypeStruct((n, PROJ_W), BF16),
            jax.ShapeDtypeStruct((n, KV32_W), F32),
        ],
        scratch_shapes=[pltpu.VMEM((tm, D_MODEL), BF16)],
        compiler_params=_cparams(("parallel", "arbitrary"), 56),
        name="proj",
    )(x2d, g, w_perm, cg, gm)


def _attn_body(sink_ref, qa_ref, qb_ref, kbp_ref, kbc_ref, vbp_ref, vbc_ref,
               kap_ref, kac_ref, vap_ref, vac_ref, ba_ref, bb_ref,
               oa_ref, ob_ref, kb_s, vb_s, ka_s, va_s, *, cpb, first_pos, hi_a, hi_b):
    i = pl.program_id(1)
    qb_rows = cpb * CHUNK
    kb_s[0:PB, :] = kbp_ref[0]
    kb_s[PB:PB + qb_rows, :] = kbc_ref[0]
    vb_s[0:PB, :] = vbp_ref[0]
    vb_s[PB:PB + qb_rows, :] = vbc_ref[0]
    ka_s[0:PA, :] = kap_ref[0]
    ka_s[PA:PA + qb_rows, :] = kac_ref[0]
    va_s[0:PA, :] = vap_ref[0]
    va_s[PA:PA + qb_rows, :] = vac_ref[0]

    lane_a = lax.broadcasted_iota(jnp.int32, (1, LA), 1)
    lane_b = lax.broadcasted_iota(jnp.int32, (1, LB), 1)
    low_half = lax.broadcasted_iota(jnp.int32, (1, LANES), 1) < HEAD_DIM
    nt = (((1,), (1,)), ((), ()))

    tasks = [("a", p) for p in range(A_HEADS // 2)] + [("b", p) for p in range(B_HEADS // 2)]
    upper = lax.broadcasted_iota(jnp.int32, (2 * CHUNK, 1), 0) >= CHUNK

    def scores(task, r0, valid_a, valid_b):
        mixer, p = task
        cols = slice(p * LANES, (p + 1) * LANES)
        if mixer == "a":
            kcols = slice((p // 2) * LANES, (p // 2 + 1) * LANES)
            q_p, k_p, b_ref, valid, keys = qa_ref[0, pl.ds(r0, CHUNK), cols], ka_s[pl.ds(r0, LA), kcols], ba_ref, valid_a, LA
        else:
            q_p, k_p, b_ref, valid, keys = qb_ref[0, pl.ds(r0, CHUNK), cols], kb_s[pl.ds(r0, LB), cols], bb_ref, valid_b, LB
        q2 = jnp.concatenate([q_p, q_p], axis=0)
        qm = jnp.where(jnp.logical_xor(upper, low_half), q2, jnp.zeros_like(q2))
        bias = b_ref[2 * p:2 * p + 2].reshape(2 * CHUNK, keys)
        s = lax.dot_general(qm, k_p, nt, preferred_element_type=F32) + bias
        if valid is not None:
            s = jnp.where(valid, s, NEG_INF)
        return s

    def attend(task, s, r0):
        mixer, p = task
        if mixer == "a":
            kcols = slice((p // 2) * LANES, (p // 2 + 1) * LANES)
            v_p, sink = va_s[pl.ds(r0, LA), kcols], jnp.where(upper, sink_ref[2 * p + 1], sink_ref[2 * p])
        else:
            v_p, sink = vb_s[pl.ds(r0, LB), slice(p * LANES, (p + 1) * LANES)], None
        m = jnp.max(s, axis=-1, keepdims=True)
        if sink is not None:
            m = jnp.maximum(m, sink)
        e = jnp.exp2(s - m)
        l = jnp.sum(e, axis=-1, keepdims=True)
        if sink is not None:
            l = l + jnp.exp2(sink - m)
        o = jnp.dot(e.astype(BF16), v_p, preferred_element_type=F32) / l
        return jnp.where(low_half, o[:CHUNK], o[CHUNK:])

    def make_chunk(masked):
        def chunk(jc, carry):
            r0 = pl.multiple_of(jc * CHUNK, CHUNK)
            valid_a = valid_b = None
            if masked:
                start = first_pos + (i * cpb + jc) * CHUNK
                valid_a = jnp.logical_and(lane_a >= jnp.maximum(PA - start, 0), lane_a < hi_a)
                valid_b = jnp.logical_and(lane_b >= jnp.maximum(PB - start, 0), lane_b < hi_b)
            pending = {}
            for t in range(len(tasks) + ATTN_AHEAD):
                if t < len(tasks):
                    pending[t] = scores(tasks[t], r0, valid_a, valid_b)
                d = t - ATTN_AHEAD
                if d >= 0:
                    mixer, p = tasks[d]
                    o_ref = oa_ref if mixer == "a" else ob_ref
                    o_ref[0, pl.ds(r0, CHUNK), p * LANES:(p + 1) * LANES] = (
                        attend(tasks[d], pending.pop(d), r0).astype(BF16))
            return carry
        return chunk

    n_masked_blocks = pl.cdiv(max(PB - first_pos, 0), cpb * CHUNK)
    if hi_a < LA or hi_b < LB:
        lax.fori_loop(0, cpb, make_chunk(True), 0)
    else:
        @pl.when(i < n_masked_blocks)
        def _():
            lax.fori_loop(0, cpb, make_chunk(True), 0)

        @pl.when(i >= n_masked_blocks)
        def _():
            lax.fori_loop(0, cpb, make_chunk(False), 0)


def _attn(sinks, hq, kbp, kbc, vbp, vbc, kap, kac, vap, vac, bias_a, bias_b, *,
          cpb, first_pos, hi_a, hi_b, cols):
    nb, s, _ = hq.shape
    qb_rows = cpb * CHUNK
    nblk = s // qb_rows
    rb = qb_rows // PB if qb_rows >= PB else None
    ra = qb_rows // PA if qb_rows >= PA else None

    def prev_map(ratio, col):
        if ratio is None:
            return lambda b, i, *_: (b, 0, col)
        return lambda b, i, *_: (b, jnp.maximum(i * ratio - 1, 0), col)

    def cur_map(col):
        return lambda b, i, *_: (b, i, col)

    in_specs = [
        pl.BlockSpec((1, qb_rows, A_QW), cur_map(cols["qa"])),
        pl.BlockSpec((1, qb_rows, B_W), cur_map(cols["qb"])),
        pl.BlockSpec((1, PB, B_W), prev_map(rb, cols["kbp"])),
        pl.BlockSpec((1, qb_rows, B_W), cur_map(cols["kbc"])),
        pl.BlockSpec((1, PB, B_W), prev_map(rb, cols["vbp"])),
        pl.BlockSpec((1, qb_rows, B_W), cur_map(cols["vbc"])),
        pl.BlockSpec((1, PA, A_KVD), prev_map(ra, cols["kap"])),
        pl.BlockSpec((1, qb_rows, A_KVD), cur_map(cols["kac"])),
        pl.BlockSpec((1, PA, A_KVD), prev_map(ra, cols["vap"])),
        pl.BlockSpec((1, qb_rows, A_KVD), cur_map(cols["vac"])),
        pl.BlockSpec((A_HEADS, CHUNK, LA), lambda b, i, *_: (0, 0, 0)),
        pl.BlockSpec((B_HEADS, CHUNK, LB), lambda b, i, *_: (0, 0, 0)),
    ]
    out_specs = [
        pl.BlockSpec((1, qb_rows, A_QW), lambda b, i, *_: (b, i, 0)),
        pl.BlockSpec((1, qb_rows, B_W), lambda b, i, *_: (b, i, 0)),
    ]
    body = functools.partial(_attn_body, cpb=cpb, first_pos=first_pos, hi_a=hi_a, hi_b=hi_b)
    return pl.pallas_call(
        body,
        grid_spec=pltpu.PrefetchScalarGridSpec(
            num_scalar_prefetch=1,
            grid=(nb, nblk),
            in_specs=in_specs,
            out_specs=out_specs,
            scratch_shapes=[
                pltpu.VMEM((PB + qb_rows, B_W), BF16),
                pltpu.VMEM((PB + qb_rows, B_W), BF16),
                pltpu.VMEM((PA + qb_rows, A_KVD), BF16),
                pltpu.VMEM((PA + qb_rows, A_KVD), BF16),
            ],
        ),
        out_shape=[
            jax.ShapeDtypeStruct((nb, s, A_QW), BF16),
            jax.ShapeDtypeStruct((nb, s, B_W), BF16),
        ],
        compiler_params=_cparams(("parallel", "arbitrary"), 48),
        name="attn",
    )(sinks, hq, hq, kbp, kbc, vbp, vbc, kap, kac, vap, vac, bias_a, bias_b)


def _merge_body(oa_p, ob_p, ga_p, gb_p, x_p, oa_s, ob_s, ga_s, gb_s, x_s,
                woa_ref, wob_ref, wout_ref, fg_ref, wr_ref, br_ref, x2_ref, xn_ref, lg_ref, *,
                n_prompt_tiles):
    def tile(oa_ref, ob_ref, ga_ref, gb_ref, x_ref):
        ya = jnp.dot(oa_ref[...], woa_ref[...], preferred_element_type=F32)
        yb = jnp.dot(ob_ref[...], wob_ref[...], preferred_element_type=F32)
        z = ga_ref[...].astype(F32) * ya + gb_ref[...].astype(F32) * yb
        y = jnp.dot(z.astype(BF16), wout_ref[...], preferred_element_type=F32)
        x2 = x_ref[...] + y
        x2_ref[...] = x2
        ms = jnp.mean(x2 * x2, axis=-1, keepdims=True)
        xn = x2 * lax.rsqrt(ms + NORM_EPS) * fg_ref[...]
        xn_ref[...] = xn
        lg_ref[...] = jnp.dot(xn.astype(BF16), wr_ref[...], preferred_element_type=F32) + br_ref[...]

    i = pl.program_id(0)

    @pl.when(i < n_prompt_tiles)
    def _():
        tile(oa_p, ob_p, ga_p, gb_p, x_p)

    @pl.when(i >= n_prompt_tiles)
    def _():
        tile(oa_s, ob_s, ga_s, gb_s, x_s)


def _merge(prompt, sample, w_oa, w_ob, w_out, fg, wr, br):
    tm = MERGE_TM
    n_p, n_s = prompt[3].shape[0], sample[3].shape[0]
    npt, nst = n_p // tm, n_s // tm
    n_out = n_p + n_s
    const = lambda i: (0, 0)
    resident = functools.partial(pl.BlockSpec, index_map=const, pipeline_mode=pl.Buffered(1))

    def token_specs(row):
        return [
            pl.BlockSpec((tm, A_QW), lambda i: (row(i), 0)),
            pl.BlockSpec((tm, B_W), lambda i: (row(i), 0)),
            pl.BlockSpec((tm, D_MODEL), lambda i: (row(i), 2)),
            pl.BlockSpec((tm, D_MODEL), lambda i: (row(i), 3)),
            pl.BlockSpec((tm, D_MODEL), lambda i: (row(i), 0)),
        ]

    in_specs = (token_specs(lambda i: jnp.minimum(i, npt - 1))
                + token_specs(lambda i: jnp.clip(i - npt, 0, nst - 1))
                + [resident((A_QW, D_MODEL)), resident((B_W, D_MODEL)), resident((D_MODEL, D_MODEL)),
                   resident((1, D_MODEL)), resident((D_MODEL, LANES)), resident((1, LANES))])
    oa_p, ob_p, h_p, x_p = prompt
    oa_s, ob_s, h_s, x_s = sample
    return pl.pallas_call(
        functools.partial(_merge_body, n_prompt_tiles=npt),
        grid=(npt + nst,),
        in_specs=in_specs,
        out_specs=[
            pl.BlockSpec((tm, D_MODEL), lambda i: (i, 0)),
            pl.BlockSpec((tm, D_MODEL), lambda i: (i, 0)),
            pl.BlockSpec((tm, LANES), lambda i: (i, 0)),
        ],
        out_shape=[
            jax.ShapeDtypeStruct((n_out, D_MODEL), F32),
            jax.ShapeDtypeStruct((n_out, D_MODEL), F32),
            jax.ShapeDtypeStruct((n_out, LANES), F32),
        ],
        compiler_params=_cparams(("arbitrary",), 56),
        name="merge",
    )(oa_p, ob_p, h_p, h_p, x_p, oa_s, ob_s, h_s, h_s, x_s, w_oa, w_ob, w_out, fg, wr, br)


def _router_body(lg_ref, tri_ref, idx_ref, gate_ref, rank_ref, cnt_ref, carry):
    i = pl.program_id(0)

    @pl.when(i == 0)
    def _():
        carry[...] = jnp.zeros_like(carry)

    lane = lax.broadcasted_iota(jnp.int32, lg_ref.shape, 1)
    x = jnp.where(lane < N_EXPERTS, lg_ref[...], -jnp.inf)
    vals, hots = [], []
    idx_out = jnp.zeros(lg_ref.shape, jnp.int32)
    for k in range(TOP_K):
        m = jnp.max(x, axis=-1, keepdims=True)
        am = jnp.min(jnp.where(x == m, lane, LANES), axis=-1, keepdims=True)
        hot = lane == am
        vals.append(m)
        hots.append(hot)
        idx_out = jnp.where(lane == k, am, idx_out)
        x = jnp.where(hot, -jnp.inf, x)
    es = [jnp.exp(v - vals[0]) for v in vals]
    denom = functools.reduce(lambda a, b: a + b, es)
    sel = functools.reduce(jnp.logical_or, hots)
    sel_f = jnp.where(sel, 1.0, 0.0)
    before = jnp.dot(tri_ref[...], sel_f.astype(BF16), preferred_element_type=F32) + carry[...]
    gate_out = jnp.zeros(lg_ref.shape, F32)
    rank_out = jnp.zeros(lg_ref.shape, F32)
    for k in range(TOP_K):
        gate_out = jnp.where(lane == k, es[k] / denom, gate_out)
        rk = jnp.sum(jnp.where(hots[k], before, 0.0), axis=-1, keepdims=True)
        rank_out = jnp.where(lane == k, rk, rank_out)
    idx_ref[...] = idx_out
    gate_ref[...] = gate_out
    rank_ref[...] = rank_out.astype(jnp.int32)
    total = carry[...] + jnp.sum(sel_f, axis=0, keepdims=True)
    carry[...] = total
    cnt_ref[...] = total.astype(jnp.int32)


def _router(logits):
    n = logits.shape[0]
    tm = ROUTER_TM
    r = jnp.arange(tm)
    tri = (r[None, :] < r[:, None]).astype(BF16)
    tile = pl.BlockSpec((tm, LANES), lambda i: (i, 0))
    return pl.pallas_call(
        _router_body,
        grid=(n // tm,),
        in_specs=[tile, pl.BlockSpec((tm, tm), lambda i: (0, 0))],
        out_specs=[tile, tile, tile, pl.BlockSpec((1, LANES), lambda i: (0, 0))],
        out_shape=[
            jax.ShapeDtypeStruct((n, LANES), jnp.int32),
            jax.ShapeDtypeStruct((n, LANES), F32),
            jax.ShapeDtypeStruct((n, LANES), jnp.int32),
            jax.ShapeDtypeStruct((1, LANES), jnp.int32),
        ],
        scratch_shapes=[pltpu.VMEM((1, LANES), F32)],
        compiler_params=_cparams(("arbitrary",), 16),
        name="router",
    )(logits, tri)


def _row_gather(idx_ref, n, src_hbm, dst, sem):
    def body(r, c):
        t = idx_ref[0, 0, r]
        pltpu.make_async_copy(src_hbm.at[pl.ds(t, 1), :], dst.at[pl.ds(r, 1), :], sem).start()
        return c
    lax.fori_loop(0, n, body, 0)


def _expert_body(wblk_ref, wexp_ref, wlo_ref, whi_ref, wfirst_ref, wkind_ref, nv_ref, tokc_ref, tokn_ref,
                 xn_hbm, w1_hbm, b1_ref, w2_ref, b2_ref,
                 out_ref, xg, xb, act_s, w1buf, sem_x, sem_w, *, n_items):
    i = pl.program_id(0)
    nf = MOE_NF
    nv = nv_ref[0]
    slot = i % 2
    e_cur = wexp_ref[i]
    e_next = wexp_ref[jnp.minimum(i + 1, n_items - 1)]

    def rows_landed(s):
        return pltpu.make_async_copy(xn_hbm.at[pl.ds(0, MOE_TM), :], xg.at[s], sem_x.at[s])

    def w1_tile(e, f, s):
        return [pltpu.make_async_copy(w1_hbm.at[e, :, pl.ds(half * D_FF + f * MOE_TF, MOE_TF)],
                                      w1buf.at[s, half], sem_w.at[s]) for half in range(2)]

    @pl.when(i == nv)
    def _():
        rows_landed(slot).wait()
        for c in w1_tile(e_cur, 0, 0):
            c.wait()

    @pl.when(i == 0)
    def _():
        _row_gather(tokc_ref, MOE_TM, xn_hbm, xg.at[0], sem_x.at[0])
        for c in w1_tile(e_cur, 0, 0):
            c.start()

    half = MOE_TM // 2
    kinds = (slice(0, MOE_TM), slice(0, half), slice(half, MOE_TM))
    live = i < nv
    kind = wkind_ref[i]

    def run(rows):
        rows_landed(slot).wait()
        xb[rows] = xg[slot, rows].astype(BF16)
        x = xb[rows]
        for f in range(nf):
            s = f % 2
            for c in w1_tile(e_cur, f, s):
                c.wait()
            nxt = w1_tile(e_cur, f + 1, 1 - s) if f + 1 < nf else w1_tile(e_next, 0, 1 - s)
            for c in nxt:
                c.start()
            for r in range(f * MOE_SHARE, (f + 1) * MOE_SHARE):
                t = tokn_ref[0, 0, r]
                pltpu.make_async_copy(xn_hbm.at[pl.ds(t, 1), :], xg.at[1 - slot, pl.ds(r, 1), :],
                                      sem_x.at[1 - slot]).start()
            for c in range(MOE_TF // MXU_COLS):
                cs = slice(c * MXU_COLS, (c + 1) * MXU_COLS)
                gs = slice(f * MOE_TF + c * MXU_COLS, f * MOE_TF + (c + 1) * MXU_COLS)
                us = slice(D_FF + gs.start, D_FF + gs.stop)
                hg = jnp.dot(x, w1buf[s, 0, :, cs], preferred_element_type=F32) + b1_ref[0, :, gs]
                hu = jnp.dot(x, w1buf[s, 1, :, cs], preferred_element_type=F32) + b1_ref[0, :, us]
                hg = jnp.minimum(hg, SWIGLU_LIMIT)
                hu = jnp.clip(hu, -SWIGLU_LIMIT, SWIGLU_LIMIT)
                sig = 0.5 * jnp.tanh((0.5 * SWIGLU_ALPHA) * hg) + 0.5
                act_s[rows, gs] = (hg * sig * (hu + 1.0)).astype(BF16)

    def ffn_out(rows):
        return jnp.dot(act_s[rows], w2_ref[0], preferred_element_type=F32) + b2_ref[0]

    def first_store(k, rows):
        out_ref[rows] = ffn_out(rows)
        if k == 1:
            out_ref[half:] = jnp.zeros((MOE_TM - half, D_MODEL), F32)

    def merge_store(rows):
        row = lax.broadcasted_iota(jnp.int32, (rows.stop - rows.start, 1), 0) + rows.start
        mine = jnp.logical_and(row >= wlo_ref[i], row < whi_ref[i])
        out_ref[rows] = jnp.where(mine, ffn_out(rows), out_ref[rows])

    for k, rows in enumerate(kinds):
        mine_kind = jnp.logical_and(live, kind == k)
        pl.when(mine_kind)(functools.partial(run, rows))
        if k != 2:
            pl.when(jnp.logical_and(mine_kind, wfirst_ref[i] == 1))(functools.partial(first_store, k, rows))
        pl.when(jnp.logical_and(mine_kind, wfirst_ref[i] == 0))(functools.partial(merge_store, rows))


def _experts(work, sorted_tok3, xn, w1, b1, w2, b2):
    w_blk, w_exp, w_lo, w_hi, w_first, w_kind, n_work = work
    n_items = w_blk.shape[0]
    nblk = sorted_tok3.shape[0]
    assert MOE_NF % 2 == 0

    def tok_map(shift):
        return lambda i, wb, *_: (wb[jnp.minimum(i + shift, n_items - 1)], 0, 0)

    by_expert = lambda i, wb, we, *_: (we[i], 0, 0)
    in_specs = [
        pl.BlockSpec((1, 1, MOE_TM), tok_map(0), memory_space=pltpu.SMEM),
        pl.BlockSpec((1, 1, MOE_TM), tok_map(1), memory_space=pltpu.SMEM),
        pl.BlockSpec(memory_space=pl.ANY),
        pl.BlockSpec(memory_space=pl.ANY),
        pl.BlockSpec((1, 1, 2 * D_FF), by_expert),
        pl.BlockSpec((1, D_FF, D_MODEL), by_expert),
        pl.BlockSpec((1, 1, D_MODEL), by_expert),
    ]
    return pl.pallas_call(
        functools.partial(_expert_body, n_items=n_items),
        grid_spec=pltpu.PrefetchScalarGridSpec(
            num_scalar_prefetch=7,
            grid=(n_items,),
            in_specs=in_specs,
            out_specs=pl.BlockSpec((MOE_TM, D_MODEL), lambda i, wb, *_: (wb[i], 0)),
            scratch_shapes=[
                pltpu.VMEM((2, MOE_TM, D_MODEL), F32),
                pltpu.VMEM((MOE_TM, D_MODEL), BF16),
                pltpu.VMEM((MOE_TM, D_FF), BF16),
                pltpu.VMEM((2, 2, D_MODEL, MOE_TF), BF16),
                pltpu.SemaphoreType.DMA((2,)),
                pltpu.SemaphoreType.DMA((2,)),
            ],
        ),
        out_shape=jax.ShapeDtypeStruct((nblk * MOE_TM, D_MODEL), F32),
        compiler_params=_cparams(("arbitrary",), 56),
        name="experts",
    )(w_blk, w_exp, w_lo, w_hi, w_first, w_kind, n_work, sorted_tok3, sorted_tok3, xn, w1, b1, w2, b2)


def _combine_body(posc_ref, posn_ref, x2_ref, gate_ref, ys_hbm, outp_ref, outs_ref, buf, sem, *,
                  n_prompt_tiles):
    i = pl.program_id(0)
    n = pl.num_programs(0)
    slot = i % 2
    nrow = TOP_K * COMB_TT

    def issue(pref, s):
        for r in range(COMB_TT):
            for k in range(TOP_K):
                p = pref[0, 0, TOP_K * r + k]
                pltpu.make_async_copy(ys_hbm.at[pl.ds(p, 1), :],
                                      buf.at[s, pl.ds(k * COMB_TT + r, 1), :], sem.at[s]).start()

    @pl.when(i == 0)
    def _():
        issue(posc_ref, 0)

    pltpu.make_async_copy(ys_hbm.at[pl.ds(0, nrow), :], buf.at[slot], sem.at[slot]).wait()

    @pl.when(i + 1 < n)
    def _():
        issue(posn_ref, 1 - slot)

    y = x2_ref[...]
    g = gate_ref[...]
    for k in range(TOP_K):
        y = y + g[:, k:k + 1] * buf[slot, k * COMB_TT:(k + 1) * COMB_TT, :]

    @pl.when(i < n_prompt_tiles)
    def _():
        outp_ref[...] = y

    @pl.when(i >= n_prompt_tiles)
    def _():
        outs_ref[...] = y


def _combine(pos3, x2, gates, ys, n_prompt):
    n_tok = x2.shape[0]
    nt = n_tok // COMB_TT
    npt = n_prompt // COMB_TT
    assert n_tok - n_prompt == COMB_TT
    body = functools.partial(_combine_body, n_prompt_tiles=npt)
    return pl.pallas_call(
        body,
        grid=(nt,),
        in_specs=[
            pl.BlockSpec((1, 1, TOP_K * COMB_TT), lambda i: (i, 0, 0), memory_space=pltpu.SMEM),
            pl.BlockSpec((1, 1, TOP_K * COMB_TT), lambda i: (jnp.minimum(i + 1, nt - 1), 0, 0),
                         memory_space=pltpu.SMEM),
            pl.BlockSpec((COMB_TT, D_MODEL), lambda i: (i, 0)),
            pl.BlockSpec((COMB_TT, LANES), lambda i: (i, 0)),
            pl.BlockSpec(memory_space=pl.ANY),
        ],
        out_specs=[
            pl.BlockSpec((COMB_TT, D_MODEL), lambda i: (jnp.minimum(i, npt - 1), 0)),
            pl.BlockSpec((COMB_TT, D_MODEL), lambda i: (0, 0)),
        ],
        out_shape=[
            jax.ShapeDtypeStruct((n_prompt, D_MODEL), F32),
            jax.ShapeDtypeStruct((COMB_TT, D_MODEL), F32),
        ],
        scratch_shapes=[
            pltpu.VMEM((2, TOP_K * COMB_TT, D_MODEL), F32),
            pltpu.SemaphoreType.DMA((2,)),
        ],
        compiler_params=_cparams(("arbitrary",), 40),
        name="combine",
    )(pos3, pos3, x2, gates, ys)


def _t5_bucket(rel):
    nb = T5_BUCKETS // 2
    max_exact = nb // 2
    ret = (rel > 0).astype(jnp.int32) * nb
    n = jnp.abs(rel)
    large = max_exact + (jnp.log(jnp.maximum(n, 1).astype(F32) / max_exact)
                         / math.log(T5_MAX_DIST / max_exact) * (nb - max_exact)).astype(jnp.int32)
    large = jnp.minimum(large, nb - 1)
    return ret + jnp.where(n < max_exact, n, large)


def _dup_heads(t):
    lead = t.shape[:-1]
    t = t.reshape(*lead, A_KV_HEADS, 1, HEAD_DIM)
    return jnp.broadcast_to(t, (*lead, A_KV_HEADS, 2, HEAD_DIM)).reshape(*lead, A_KVD)


def _work_items(top_i, rank, counts, n_tok):
    nk = n_tok * TOP_K
    assert nk % MOE_TM == 0
    nblk = nk // MOE_TM
    n_items = nblk + N_EXPERTS
    end = jnp.cumsum(counts)
    start = end - counts
    first_blk = start // MOE_TM
    n_it = jnp.where(counts > 0, (end - 1) // MOE_TM - first_blk + 1, 0)
    it_end = jnp.cumsum(n_it)
    it_start = it_end - n_it
    n_work = it_end[-1]
    experts = jnp.arange(N_EXPERTS, dtype=jnp.int32)
    table = lambda tab, idx: jnp.sum(jnp.where(idx[..., None] == experts, tab, 0), axis=-1)
    w = jnp.arange(n_items, dtype=jnp.int32)
    wc = jnp.minimum(w, n_work - 1)
    w_exp = jnp.minimum(jnp.sum((it_end[None, :] <= wc[:, None]).astype(jnp.int32), axis=1), N_EXPERTS - 1)
    w_blk = table(first_blk, w_exp) + wc - table(it_start, w_exp)
    w_lo = jnp.clip(table(start, w_exp) - w_blk * MOE_TM, 0, MOE_TM)
    w_hi = jnp.clip(table(end, w_exp) - w_blk * MOE_TM, 0, MOE_TM)
    w_first = jnp.concatenate([jnp.ones((1,), jnp.int32), (w_blk[1:] != w_blk[:-1]).astype(jnp.int32)])
    pos = table(start, top_i) + rank
    flat_tok = jnp.arange(nk, dtype=jnp.int32) // TOP_K
    _, sorted_tok = lax.sort_key_val(top_i.reshape(-1), flat_tok, is_stable=True)
    i32 = lambda t: t.astype(jnp.int32)
    w_kind = jnp.where(w_hi <= MOE_TM // 2, 1, jnp.where(w_lo >= MOE_TM // 2, 2, 0))
    work = (i32(w_blk), i32(w_exp), i32(w_lo), i32(w_hi), w_first, i32(w_kind), i32(n_work).reshape(1))
    return sorted_tok.reshape(nblk, 1, MOE_TM), work, pos.reshape(n_tok // COMB_TT, 1, TOP_K * COMB_TT)


def _band_bias(vals, keys):
    vt = vals.T
    return jnp.stack([vt[:, CHUNK - 1 - i:CHUNK - 1 - i + keys] for i in range(CHUNK)], axis=1).astype(F32)


def kernel(x_prompt, x_sample, cache_a_k, cache_a_v, cache_b_k, cache_b_v, attn_norm, w_in,
           a_q_norm, a_k_norm, b_q_norm, b_k_norm, a_sinks, t5_table, b_rel_table, w_oa, w_ob,
           w_out, ffn_norm, router_w, router_b, w1, b1, w2, b2):
    batch, seq, _ = x_prompt.shape
    dec_b, dec_s, _ = x_sample.shape
    assert attn_norm.shape[0] == 1, "single layer"
    assert dec_s <= CHUNK and PAST_LEN % CHUNK == 0 and PAST_LEN >= PB
    n_prompt = batch * seq
    n_sample = dec_b * dec_s

    wi = w_in[0]
    c = [0]
    for wdt in (A_QW, A_KVW, A_KVW, B_W, B_W, B_W, D_MODEL, D_MODEL):
        c.append(c[-1] + wdt)
    w_qa, w_ka, w_va, w_qb, w_kb, w_vb, w_ga, w_gb = [wi[:, c[k]:c[k + 1]] for k in range(8)]
    w_perm = jnp.concatenate([w_qa, w_qb, w_kb, w_vb, w_ga, w_gb, _dup_heads(w_ka), _dup_heads(w_va)],
                             axis=1).astype(BF16)
    scale = HEAD_DIM ** -0.5 * LOG2E
    ones = lambda n: jnp.ones((n,), F32)
    cg = jnp.concatenate([
        jnp.tile(a_q_norm[0] * scale, A_HEADS), jnp.tile(b_q_norm[0] * scale, B_HEADS),
        jnp.tile(b_k_norm[0], B_HEADS), ones(B_W), ones(2 * D_MODEL),
        jnp.tile(a_k_norm[0], 2 * A_KV_HEADS), ones(A_KVD)]).reshape(1, PROJ_W).astype(F32)
    hd = jnp.arange(MXU_COLS) // HEAD_DIM
    gm = (hd[:, None] == hd[None, :]).astype(BF16)
    g_attn = attn_norm[0].reshape(1, D_MODEL)
    rel_a = jnp.arange(LA + CHUNK - 1) - PA - (CHUNK - 1)
    rel_b = jnp.arange(LB + CHUNK - 1) - PB - (CHUNK - 1)
    bias_a = _band_bias(t5_table[_t5_bucket(rel_a)] * LOG2E, LA)
    bias_b = _band_bias(b_rel_table[0][jnp.clip(rel_b, -B_REL_CLIP, CHUNK - 1) + B_REL_CLIP] * LOG2E, LB)
    sinks = a_sinks[0].astype(F32) * LOG2E
    woa, wob, wout = w_oa[0].astype(BF16), w_ob[0].astype(BF16), w_out[0].astype(BF16)
    fg = ffn_norm[0].reshape(1, D_MODEL)
    wr = jnp.pad(router_w[0], ((0, 0), (0, LANES - N_EXPERTS))).astype(BF16)
    br = jnp.pad(router_b[0], (0, LANES - N_EXPERTS)).reshape(1, LANES)
    w1b, w2b = w1[0].astype(BF16), w2[0].astype(BF16)
    b1r = b1[0].reshape(N_EXPERTS, 1, 2 * D_FF)
    b2r = b2[0].reshape(N_EXPERTS, 1, D_MODEL)

    xp2 = x_prompt.reshape(n_prompt, D_MODEL)
    h_p, kv_p = _proj(xp2, g_attn, w_perm, cg, gm, tm=1024)
    h_p3 = h_p.reshape(batch, seq, PROJ_W)
    cols_p = dict(qa=0, qb=1, kbp=2, kbc=2, vbp=3, vbc=3, kap=16, kac=16, vap=17, vac=17)
    oa_p, ob_p = _attn(sinks, h_p3, h_p3, h_p3, h_p3, h_p3, h_p3, h_p3, h_p3, h_p3, bias_a, bias_b,
                       cpb=8, first_pos=0, hi_a=LA, hi_b=LB, cols=cols_p)
    n_tok = n_prompt + n_sample

    xs_pad = jnp.pad(x_sample, ((0, 0), (0, CHUNK - dec_s), (0, 0))).reshape(dec_b * CHUNK, D_MODEL)
    h_s, kv_s = _proj(xs_pad, g_attn, w_perm, cg, gm, tm=dec_b * CHUNK)
    h_s3 = h_s.reshape(dec_b, CHUNK, PROJ_W)
    ckb = cache_b_k[0].reshape(dec_b, PB, B_W).astype(BF16)
    cvb = cache_b_v[0].reshape(dec_b, PB, B_W).astype(BF16)
    cka = _dup_heads(cache_a_k[0].reshape(dec_b, PA, A_KVW)).astype(BF16)
    cva = _dup_heads(cache_a_v[0].reshape(dec_b, PA, A_KVW)).astype(BF16)
    cols_s = dict(qa=0, qb=1, kbp=0, kbc=2, vbp=0, vbc=3, kap=0, kac=16, vap=0, vac=17)
    oa_s, ob_s = _attn(sinks, h_s3, ckb, h_s3, cvb, h_s3, cka, h_s3, cva, h_s3, bias_a, bias_b,
                       cpb=1, first_pos=PAST_LEN, hi_a=PA + dec_s, hi_b=PB + dec_s, cols=cols_s)
    keep_rows = lambda t: t[:, :dec_s].reshape(n_sample, t.shape[-1])
    x2, xn, logits = _merge(
        (oa_p.reshape(n_prompt, A_QW), ob_p.reshape(n_prompt, B_W), h_p, xp2),
        (keep_rows(oa_s), keep_rows(ob_s), keep_rows(h_s3), x_sample.reshape(n_sample, D_MODEL)),
        woa, wob, wout, fg, wr, br)

    top_i, gates, rank, counts = _router(logits)
    sorted_tok3, work, pos3 = _work_items(top_i[:, :TOP_K], rank[:, :TOP_K], counts[0, :N_EXPERTS], n_tok)
    ys = _experts(work, sorted_tok3, xn, w1b, b1r, w2b, b2r)
    y_p, y_s = _combine(pos3, x2, gates, ys, n_prompt)

    kv_p3 = kv_p.reshape(batch, seq, KV32_W)
    kv_s3 = kv_s.reshape(dec_b, CHUNK, KV32_W)
    undup = lambda t: t.reshape(*t.shape[:-1], A_KV_HEADS, 2, HEAD_DIM)[..., 0, :]
    heads_b = lambda t: t.reshape(*t.shape[:-1], B_HEADS, HEAD_DIM)
    o_kb, o_vb, o_ka, o_va = 0, B_W, 2 * B_W, 2 * B_W + A_KVD
    p_bk = heads_b(kv_p3[:, seq - PB:, o_kb:o_kb + B_W])[None]
    p_bv = heads_b(kv_p3[:, seq - PB:, o_vb:o_vb + B_W])[None]
    p_ak = undup(kv_p3[:, seq - PA:, o_ka:o_ka + A_KVD])[None]
    p_av = undup(kv_p3[:, seq - PA:, o_va:o_va + A_KVD])[None]
    s_bk = heads_b(kv_s3[:, :dec_s, o_kb:o_kb + B_W])[None]
    s_bv = heads_b(kv_s3[:, :dec_s, o_vb:o_vb + B_W])[None]
    s_ak = undup(kv_s3[:, :dec_s, o_ka:o_ka + A_KVD])[None]
    s_av = undup(kv_s3[:, :dec_s, o_va:o_va + A_KVD])[None]
    return (y_p.reshape(batch, seq, D_MODEL), y_s.reshape(dec_b, dec_s, D_MODEL),
            p_ak, p_av, p_bk, p_bv, s_ak, s_av, s_bk, s_bv)
```

```python
import functools
import math

import jax
import jax.numpy as jnp
from jax import lax
from jax.experimental import pallas as pl
from jax.experimental.pallas import tpu as pltpu

F32 = jnp.float32
BF16 = jnp.bfloat16

D_MODEL = 2048
CHUNK = 64
HEAD_DIM = 64
A_HEADS = 16
A_KV_HEADS = 4
A_PAST_CHUNKS = 2
B_HEADS = 16
B_PAST_CHUNKS = 8
B_REL_CLIP = 256
T5_BUCKETS = 32
T5_MAX_DIST = (A_PAST_CHUNKS + 1) * CHUNK
N_EXPERTS = 32
TOP_K = 4
D_FF = D_MODEL
SWIGLU_ALPHA = 1.702
SWIGLU_LIMIT = 7.0
NORM_EPS = 1e-6
NEG_INF = -1e30
LOG2E = math.log2(math.e)
PAST_LEN = 1024

A_QW = A_HEADS * HEAD_DIM
A_KVW = A_KV_HEADS * HEAD_DIM
B_W = B_HEADS * HEAD_DIM
PA = A_PAST_CHUNKS * CHUNK
PB = B_PAST_CHUNKS * CHUNK
LA = PA + CHUNK
LB = PB + CHUNK

LANES = 128
MXU_COLS = 256

A_KVD = 2 * A_KVW
PROJ_W = A_QW + 3 * B_W + 2 * D_MODEL + 2 * A_KVD
PROJ_TN = 1024
KV32_W = 2 * B_W + 2 * A_KVD
_NORM_TILES = (0, 1, 2)
_SIGMOID_LO, _SIGMOID_HI = 4, 8
_KV_TAIL_TILE = 8

MOE_TM = 512
MOE_TF = 512
MOE_NF = D_FF // MOE_TF
MOE_SHARE = MOE_TM // MOE_NF
COMB_TT = 256
ROUTER_TM = 768
MERGE_TM = 256
ATTN_AHEAD = 4


def _cparams(sem, vmem_mb):
    return pltpu.CompilerParams(dimension_semantics=sem, vmem_limit_bytes=vmem_mb * 1024 * 1024)


def _proj_body(x_ref, g_ref, w_ref, cg_ref, gm_ref, h_ref, kv_ref, xn_s):
    j = pl.program_id(1)

    @pl.when(j == 0)
    def _():
        x = x_ref[...]
        ms = jnp.mean(x * x, axis=-1, keepdims=True)
        xn_s[...] = (x * lax.rsqrt(ms + NORM_EPS) * g_ref[...]).astype(BF16)

    is_norm = functools.reduce(jnp.logical_or, [j == t for t in _NORM_TILES])
    is_sig = jnp.logical_and(j >= _SIGMOID_LO, j < _SIGMOID_HI)

    n_strips = PROJ_TN // MXU_COLS

    def strips(epilogues, write_kv, dots_first=False):
        col = [slice(c * MXU_COLS, (c + 1) * MXU_COLS) for c in range(n_strips)]
        mm = lambda cs: jnp.dot(xn_s[...], w_ref[:, cs], preferred_element_type=F32)
        accs = [mm(cs) for cs in col] if dots_first else None
        for c, cs in enumerate(col):
            y = epilogues[c](accs[c] if dots_first else mm(cs), cs)
            h_ref[:, cs] = y.astype(BF16)
            if write_kv:
                kv_ref[:, cs] = y

    def norm(a, cs):
        ss = jnp.dot((a * a).astype(BF16), gm_ref[...], preferred_element_type=F32)
        return a * lax.rsqrt(ss * (1.0 / HEAD_DIM) + NORM_EPS) * cg_ref[:, cs]

    as_is = lambda a, cs: a
    sigmoid = lambda a, cs: 0.5 * jnp.tanh(0.5 * a) + 0.5

    @pl.when(is_norm)
    def _():
        strips([norm] * n_strips, True, dots_first=True)

    @pl.when(is_sig)
    def _():
        strips([sigmoid] * n_strips, False)

    @pl.when(j == _KV_TAIL_TILE)
    def _():
        strips([norm] * (n_strips // 2) + [as_is] * (n_strips // 2), True, dots_first=True)

    @pl.when(j == _SIGMOID_LO - 1)
    def _():
        strips([as_is] * n_strips, True)


def _kv_tile(j):
    return jnp.clip(j - 2, 0, 1) + (j >= _KV_TAIL_TILE).astype(jnp.int32)


def _proj(x2d, g, w_perm, cg, gm, tm):
    n = x2d.shape[0]
    grid = (n // tm, PROJ_W // PROJ_TN)
    return pl.pallas_call(
        _proj_body,
        grid=grid,
        in_specs=[
            pl.BlockSpec((tm, D_MODEL), lambda i, j: (i, 0)),
            pl.BlockSpec((1, D_MODEL), lambda i, j: (0, 0)),
            pl.BlockSpec((D_MODEL, PROJ_TN), lambda i, j: (0, j)),
            pl.BlockSpec((1, PROJ_TN), lambda i, j: (0, j)),
            pl.BlockSpec((MXU_COLS, MXU_COLS), lambda i, j: (0, 0)),
        ],
        out_specs=[
            pl.BlockSpec((tm, PROJ_TN), lambda i, j: (i, j)),
            pl.BlockSpec((tm, PROJ_TN), lambda i, j: (i, _kv_tile(j))),
        ],
        out_shape=[
            jax.ShapeDtypeStruct((n, PROJ_W), BF16),
            jax.ShapeDtypeStruct((n, KV32_W), F32),
        ],
        scratch_shapes=[pltpu.VMEM((tm, D_MODEL), BF16)],
        compiler_params=_cparams(("parallel", "arbitrary"), 56),
        name="proj",
    )(x2d, g, w_perm, cg, gm)


def _attn_body(sink_ref, qa_ref, qb_ref, kbp_ref, kbc_ref, vbp_ref, vbc_ref,
               kap_ref, kac_ref, vap_ref, vac_ref, ba_ref, bb_ref,
               oa_ref, ob_ref, kb_s, vb_s, ka_s, va_s, *, cpb, first_pos, hi_a, hi_b):
    i = pl.program_id(1)
    qb_rows = cpb * CHUNK
    kb_s[0:PB, :] = kbp_ref[0]
    kb_s[PB:PB + qb_rows, :] = kbc_ref[0]
    vb_s[0:PB, :] = vbp_ref[0]
    vb_s[PB:PB + qb_rows, :] = vbc_ref[0]
    ka_s[0:PA, :] = kap_ref[0]
    ka_s[PA:PA + qb_rows, :] = kac_ref[0]
    va_s[0:PA, :] = vap_ref[0]
    va_s[PA:PA + qb_rows, :] = vac_ref[0]

    lane_a = lax.broadcasted_iota(jnp.int32, (1, LA), 1)
    lane_b = lax.broadcasted_iota(jnp.int32, (1, LB), 1)
    low_half = lax.broadcasted_iota(jnp.int32, (1, LANES), 1) < HEAD_DIM
    nt = (((1,), (1,)), ((), ()))

    tasks = [("a", p) for p in range(A_HEADS // 2)] + [("b", p) for p in range(B_HEADS // 2)]
    upper = lax.broadcasted_iota(jnp.int32, (2 * CHUNK, 1), 0) >= CHUNK

    def scores(task, r0, valid_a, valid_b):
        mixer, p = task
        cols = slice(p * LANES, (p + 1) * LANES)
        if mixer == "a":
            kcols = slice((p // 2) * LANES, (p // 2 + 1) * LANES)
            q_p, k_p, b_ref, valid, keys = qa_ref[0, pl.ds(r0, CHUNK), cols], ka_s[pl.ds(r0, LA), kcols], ba_ref, valid_a, LA
        else:
            q_p, k_p, b_ref, valid, keys = qb_ref[0, pl.ds(r0, CHUNK), cols], kb_s[pl.ds(r0, LB), cols], bb_ref, valid_b, LB
        q2 = jnp.concatenate([q_p, q_p], axis=0)
        qm = jnp.where(jnp.logical_xor(upper, low_half), q2, jnp.zeros_like(q2))
        bias = b_ref[2 * p:2 * p + 2].reshape(2 * CHUNK, keys)
        s = lax.dot_general(qm, k_p, nt, preferred_element_type=F32) + bias
        if valid is not None:
            s = jnp.where(valid, s, NEG_INF)
        return s

    def attend(task, s, r0):
        mixer, p = task
        if mixer == "a":
            kcols = slice((p // 2) * LANES, (p // 2 + 1) * LANES)
            v_p, sink = va_s[pl.ds(r0, LA), kcols], jnp.where(upper, sink_ref[2 * p + 1], sink_ref[2 * p])
        else:
            v_p, sink = vb_s[pl.ds(r0, LB), slice(p * LANES, (p + 1) * LANES)], None
        m = jnp.max(s, axis=-1, keepdims=True)
        if sink is not None:
            m = jnp.maximum(m, sink)
        e = jnp.exp2(s - m)
        l = jnp.sum(e, axis=-1, keepdims=True)
        if sink is not None:
            l = l + jnp.exp2(sink - m)
        o = jnp.dot(e.astype(BF16), v_p, preferred_element_type=F32) / l
        return jnp.where(low_half, o[:CHUNK], o[CHUNK:])

    def make_chunk(masked):
        def chunk(jc, carry):
            r0 = pl.multiple_of(jc * CHUNK, CHUNK)
            valid_a = valid_b = None
            if masked:
                start = first_pos + (i * cpb + jc) * CHUNK
                valid_a = jnp.logical_and(lane_a >= jnp.maximum(PA - start, 0), lane_a < hi_a)
                valid_b = jnp.logical_and(lane_b >= jnp.maximum(PB - start, 0), lane_b < hi_b)
            pending = {}
            for t in range(len(tasks) + ATTN_AHEAD):
                if t < len(tasks):
                    pending[t] = scores(tasks[t], r0, valid_a, valid_b)
                d = t - ATTN_AHEAD
                if d >= 0:
                    mixer, p = tasks[d]
                    o_ref = oa_ref if mixer == "a" else ob_ref
                    o_ref[0, pl.ds(r0, CHUNK), p * LANES:(p + 1) * LANES] = (
                        attend(tasks[d], pending.pop(d), r0).astype(BF16))
            return carry
        return chunk

    n_masked_blocks = pl.cdiv(max(PB - first_pos, 0), cpb * CHUNK)
    if hi_a < LA or hi_b < LB:
        lax.fori_loop(0, cpb, make_chunk(True), 0)
    else:
        @pl.when(i < n_masked_blocks)
        def _():
            lax.fori_loop(0, cpb, make_chunk(True), 0)

        @pl.when(i >= n_masked_blocks)
        def _():
            lax.fori_loop(0, cpb, make_chunk(False), 0)


def _attn(sinks, hq, kbp, kbc, vbp, vbc, kap, kac, vap, vac, bias_a, bias_b, *,
          cpb, first_pos, hi_a, hi_b, cols):
    nb, s, _ = hq.shape
    qb_rows = cpb * CHUNK
    nblk = s // qb_rows
    rb = qb_rows // PB if qb_rows >= PB else None
    ra = qb_rows // PA if qb_rows >= PA else None

    def prev_map(ratio, col):
        if ratio is None:
            return lambda b, i, *_: (b, 0, col)
        return lambda b, i, *_: (b, jnp.maximum(i * ratio - 1, 0), col)

    def cur_map(col):
        return lambda b, i, *_: (b, i, col)

    in_specs = [
        pl.BlockSpec((1, qb_rows, A_QW), cur_map(cols["qa"])),
        pl.BlockSpec((1, qb_rows, B_W), cur_map(cols["qb"])),
        pl.BlockSpec((1, PB, B_W), prev_map(rb, cols["kbp"])),
        pl.BlockSpec((1, qb_rows, B_W), cur_map(cols["kbc"])),
        pl.BlockSpec((1, PB, B_W), prev_map(rb, cols["vbp"])),
        pl.BlockSpec((1, qb_rows, B_W), cur_map(cols["vbc"])),
        pl.BlockSpec((1, PA, A_KVD), prev_map(ra, cols["kap"])),
        pl.BlockSpec((1, qb_rows, A_KVD), cur_map(cols["kac"])),
        pl.BlockSpec((1, PA, A_KVD), prev_map(ra, cols["vap"])),
        pl.BlockSpec((1, qb_rows, A_KVD), cur_map(cols["vac"])),
        pl.BlockSpec((A_HEADS, CHUNK, LA), lambda b, i, *_: (0, 0, 0)),
        pl.BlockSpec((B_HEADS, CHUNK, LB), lambda b, i, *_: (0, 0, 0)),
    ]
    out_specs = [
        pl.BlockSpec((1, qb_rows, A_QW), lambda b, i, *_: (b, i, 0)),
        pl.BlockSpec((1, qb_rows, B_W), lambda b, i, *_: (b, i, 0)),
    ]
    body = functools.partial(_attn_body, cpb=cpb, first_pos=first_pos, hi_a=hi_a, hi_b=hi_b)
    return pl.pallas_call(
        body,
        grid_spec=pltpu.PrefetchScalarGridSpec(
            num_scalar_prefetch=1,
            grid=(nb, nblk),
            in_specs=in_specs,
            out_specs=out_specs,
            scratch_shapes=[
                pltpu.VMEM((PB + qb_rows, B_W), BF16),
                pltpu.VMEM((PB + qb_rows, B_W), BF16),
                pltpu.VMEM((PA + qb_rows, A_KVD), BF16),
                pltpu.VMEM((PA + qb_rows, A_KVD), BF16),
            ],
        ),
        out_shape=[
            jax.ShapeDtypeStruct((nb, s, A_QW), BF16),
            jax.ShapeDtypeStruct((nb, s, B_W), BF16),
        ],
        compiler_params=_cparams(("parallel", "arbitrary"), 48),
        name="attn",
    )(sinks, hq, hq, kbp, kbc, vbp, vbc, kap, kac, vap, vac, bias_a, bias_b)


def _merge_body(oa_p, ob_p, ga_p, gb_p, x_p, oa_s, ob_s, ga_s, gb_s, x_s,
                woa_ref, wob_ref, wout_ref, fg_ref, wr_ref, br_ref, x2_ref, xn_ref, lg_ref, *,
                n_prompt_tiles):
    def tile(oa_ref, ob_ref, ga_ref, gb_ref, x_ref):
        ya = jnp.dot(oa_ref[...], woa_ref[...], preferred_element_type=F32)
        yb = jnp.dot(ob_ref[...], wob_ref[...], preferred_element_type=F32)
        z = ga_ref[...].astype(F32) * ya + gb_ref[...].astype(F32) * yb
        y = jnp.dot(z.astype(BF16), wout_ref[...], preferred_element_type=F32)
        x2 = x_ref[...] + y
        x2_ref[...] = x2
        ms = jnp.mean(x2 * x2, axis=-1, keepdims=True)
        xn = x2 * lax.rsqrt(ms + NORM_EPS) * fg_ref[...]
        xn_ref[...] = xn
        lg_ref[...] = jnp.dot(xn.astype(BF16), wr_ref[...], preferred_element_type=F32) + br_ref[...]

    i = pl.program_id(0)

    @pl.when(i < n_prompt_tiles)
    def _():
        tile(oa_p, ob_p, ga_p, gb_p, x_p)

    @pl.when(i >= n_prompt_tiles)
    def _():
        tile(oa_s, ob_s, ga_s, gb_s, x_s)


def _merge(prompt, sample, w_oa, w_ob, w_out, fg, wr, br):
    tm = MERGE_TM
    n_p, n_s = prompt[3].shape[0], sample[3].shape[0]
    npt, nst = n_p // tm, n_s // tm
    n_out = n_p + n_s
    const = lambda i: (0, 0)
    resident = functools.partial(pl.BlockSpec, index_map=const, pipeline_mode=pl.Buffered(1))

    def token_specs(row):
        return [
            pl.BlockSpec((tm, A_QW), lambda i: (row(i), 0)),
            pl.BlockSpec((tm, B_W), lambda i: (row(i), 0)),
            pl.BlockSpec((tm, D_MODEL), lambda i: (row(i), 2)),
            pl.BlockSpec((tm, D_MODEL), lambda i: (row(i), 3)),
            pl.BlockSpec((tm, D_MODEL), lambda i: (row(i), 0)),
        ]

    in_specs = (token_specs(lambda i: jnp.minimum(i, npt - 1))
                + token_specs(lambda i: jnp.clip(i - npt, 0, nst - 1))
                + [resident((A_QW, D_MODEL)), resident((B_W, D_MODEL)), resident((D_MODEL, D_MODEL)),
                   resident((1, D_MODEL)), resident((D_MODEL, LANES)), resident((1, LANES))])
    oa_p, ob_p, h_p, x_p = prompt
    oa_s, ob_s, h_s, x_s = sample
    return pl.pallas_call(
        functools.partial(_merge_body, n_prompt_tiles=npt),
        grid=(npt + nst,),
        in_specs=in_specs,
        out_specs=[
            pl.BlockSpec((tm, D_MODEL), lambda i: (i, 0)),
            pl.BlockSpec((tm, D_MODEL), lambda i: (i, 0)),
            pl.BlockSpec((tm, LANES), lambda i: (i, 0)),
        ],
        out_shape=[
            jax.ShapeDtypeStruct((n_out, D_MODEL), F32),
            jax.ShapeDtypeStruct((n_out, D_MODEL), F32),
            jax.ShapeDtypeStruct((n_out, LANES), F32),
        ],
        compiler_params=_cparams(("arbitrary",), 56),
        name="merge",
    )(oa_p, ob_p, h_p, h_p, x_p, oa_s, ob_s, h_s, h_s, x_s, w_oa, w_ob, w_out, fg, wr, br)


def _router_body(lg_ref, tri_ref, idx_ref, gate_ref, rank_ref, cnt_ref, carry):
    i = pl.program_id(0)

    @pl.when(i == 0)
    def _():
        carry[...] = jnp.zeros_like(carry)

    lane = lax.broadcasted_iota(jnp.int32, lg_ref.shape, 1)
    x = jnp.where(lane < N_EXPERTS, lg_ref[...], -jnp.inf)
    vals, hots = [], []
    idx_out = jnp.zeros(lg_ref.shape, jnp.int32)
    for k in range(TOP_K):
        m = jnp.max(x, axis=-1, keepdims=True)
        am = jnp.min(jnp.where(x == m, lane, LANES), axis=-1, keepdims=True)
        hot = lane == am
        vals.append(m)
        hots.append(hot)
        idx_out = jnp.where(lane == k, am, idx_out)
        x = jnp.where(hot, -jnp.inf, x)
    es = [jnp.exp(v - vals[0]) for v in vals]
    denom = functools.reduce(lambda a, b: a + b, es)
    sel = functools.reduce(jnp.logical_or, hots)
    sel_f = jnp.where(sel, 1.0, 0.0)
    before = jnp.dot(tri_ref[...], sel_f.astype(BF16), preferred_element_type=F32) + carry[...]
    gate_out = jnp.zeros(lg_ref.shape, F32)
    rank_out = jnp.zeros(lg_ref.shape, F32)
    for k in range(TOP_K):
        gate_out = jnp.where(lane == k, es[k] / denom, gate_out)
        rk = jnp.sum(jnp.where(hots[k], before, 0.0), axis=-1, keepdims=True)
        rank_out = jnp.where(lane == k, rk, rank_out)
    idx_ref[...] = idx_out
    gate_ref[...] = gate_out
    rank_ref[...] = rank_out.astype(jnp.int32)
    total = carry[...] + jnp.sum(sel_f, axis=0, keepdims=True)
    carry[...] = total
    cnt_ref[...] = total.astype(jnp.int32)


def _router(logits):
    n = logits.shape[0]
    tm = ROUTER_TM
    assert n % tm == 0
    r = jnp.arange(tm)
    tri = (r[None, :] < r[:, None]).astype(BF16)
    tile = pl.BlockSpec((tm, LANES), lambda i: (i, 0))
    return pl.pallas_call(
        _router_body,
        grid=(n // tm,),
        in_specs=[tile, pl.BlockSpec((tm, tm), lambda i: (0, 0))],
        out_specs=[tile, tile, tile, pl.BlockSpec((1, LANES), lambda i: (0, 0))],
        out_shape=[
            jax.ShapeDtypeStruct((n, LANES), jnp.int32),
            jax.ShapeDtypeStruct((n, LANES), F32),
            jax.ShapeDtypeStruct((n, LANES), jnp.int32),
            jax.ShapeDtypeStruct((1, LANES), jnp.int32),
        ],
        scratch_shapes=[pltpu.VMEM((1, LANES), F32)],
        compiler_params=_cparams(("arbitrary",), 16),
        name="router",
    )(logits, tri)


def _row_gather(idx_ref, n, src_hbm, dst, sem):
    def body(r, c):
        t = idx_ref[0, 0, r]
        pltpu.make_async_copy(src_hbm.at[pl.ds(t, 1), :], dst.at[pl.ds(r, 1), :], sem).start()
        return c
    lax.fori_loop(0, n, body, 0)


def _expert_body(wblk_ref, wexp_ref, wlo_ref, whi_ref, wfirst_ref, wkind_ref, nv_ref, tokc_ref, tokn_ref,
                 xn_hbm, w1_hbm, b1_ref, w2_ref, b2_ref,
                 out_ref, xg, xb, act_s, w1buf, sem_x, sem_w, *, n_items):
    i = pl.program_id(0)
    nf = MOE_NF
    nv = nv_ref[0]
    slot = i % 2
    e_cur = wexp_ref[i]
    e_next = wexp_ref[jnp.minimum(i + 1, n_items - 1)]

    def rows_landed(s):
        return pltpu.make_async_copy(xn_hbm.at[pl.ds(0, MOE_TM), :], xg.at[s], sem_x.at[s])

    def w1_tile(e, f, s):
        return [pltpu.make_async_copy(w1_hbm.at[e, :, pl.ds(half * D_FF + f * MOE_TF, MOE_TF)],
                                      w1buf.at[s, half], sem_w.at[s]) for half in range(2)]

    @pl.when(i == nv)
    def _():
        rows_landed(slot).wait()
        for c in w1_tile(e_cur, 0, 0):
            c.wait()

    @pl.when(i == 0)
    def _():
        _row_gather(tokc_ref, MOE_TM, xn_hbm, xg.at[0], sem_x.at[0])
        for c in w1_tile(e_cur, 0, 0):
            c.start()

    half = MOE_TM // 2
    kinds = (slice(0, MOE_TM), slice(0, half), slice(half, MOE_TM))
    live = i < nv
    kind = wkind_ref[i]

    def run(rows):
        rows_landed(slot).wait()
        xb[rows] = xg[slot, rows].astype(BF16)
        x = xb[rows]
        for f in range(nf):
            s = f % 2
            for c in w1_tile(e_cur, f, s):
                c.wait()
            nxt = w1_tile(e_cur, f + 1, 1 - s) if f + 1 < nf else w1_tile(e_next, 0, 1 - s)
            for c in nxt:
                c.start()
            for r in range(f * MOE_SHARE, (f + 1) * MOE_SHARE):
                t = tokn_ref[0, 0, r]
                pltpu.make_async_copy(xn_hbm.at[pl.ds(t, 1), :], xg.at[1 - slot, pl.ds(r, 1), :],
                                      sem_x.at[1 - slot]).start()
            for c in range(MOE_TF // MXU_COLS):
                cs = slice(c * MXU_COLS, (c + 1) * MXU_COLS)
                gs = slice(f * MOE_TF + c * MXU_COLS, f * MOE_TF + (c + 1) * MXU_COLS)
                us = slice(D_FF + gs.start, D_FF + gs.stop)
                hg = jnp.dot(x, w1buf[s, 0, :, cs], preferred_element_type=F32) + b1_ref[0, :, gs]
                hu = jnp.dot(x, w1buf[s, 1, :, cs], preferred_element_type=F32) + b1_ref[0, :, us]
                hg = jnp.minimum(hg, SWIGLU_LIMIT)
                hu = jnp.clip(hu, -SWIGLU_LIMIT, SWIGLU_LIMIT)
                sig = 0.5 * jnp.tanh((0.5 * SWIGLU_ALPHA) * hg) + 0.5
                act_s[rows, gs] = (hg * sig * (hu + 1.0)).astype(BF16)

    def ffn_out(rows):
        return jnp.dot(act_s[rows], w2_ref[0], preferred_element_type=F32) + b2_ref[0]

    def first_store(k, rows):
        out_ref[rows] = ffn_out(rows)
        if k == 1:
            out_ref[half:] = jnp.zeros((MOE_TM - half, D_MODEL), F32)

    def merge_store(rows):
        row = lax.broadcasted_iota(jnp.int32, (rows.stop - rows.start, 1), 0) + rows.start
        mine = jnp.logical_and(row >= wlo_ref[i], row < whi_ref[i])
        out_ref[rows] = jnp.where(mine, ffn_out(rows), out_ref[rows])

    for k, rows in enumerate(kinds):
        mine_kind = jnp.logical_and(live, kind == k)
        pl.when(mine_kind)(functools.partial(run, rows))
        if k != 2:
            pl.when(jnp.logical_and(mine_kind, wfirst_ref[i] == 1))(functools.partial(first_store, k, rows))
        pl.when(jnp.logical_and(mine_kind, wfirst_ref[i] == 0))(functools.partial(merge_store, rows))


def _experts(work, sorted_tok3, xn, w1, b1, w2, b2):
    w_blk, w_exp, w_lo, w_hi, w_first, w_kind, n_work = work
    n_items = w_blk.shape[0]
    nblk = sorted_tok3.shape[0]
    assert MOE_NF % 2 == 0

    def tok_map(shift):
        return lambda i, wb, *_: (wb[jnp.minimum(i + shift, n_items - 1)], 0, 0)

    by_expert = lambda i, wb, we, *_: (we[i], 0, 0)
    in_specs = [
        pl.BlockSpec((1, 1, MOE_TM), tok_map(0), memory_space=pltpu.SMEM),
        pl.BlockSpec((1, 1, MOE_TM), tok_map(1), memory_space=pltpu.SMEM),
        pl.BlockSpec(memory_space=pl.ANY),
        pl.BlockSpec(memory_space=pl.ANY),
        pl.BlockSpec((1, 1, 2 * D_FF), by_expert),
        pl.BlockSpec((1, D_FF, D_MODEL), by_expert),
        pl.BlockSpec((1, 1, D_MODEL), by_expert),
    ]
    return pl.pallas_call(
        functools.partial(_expert_body, n_items=n_items),
        grid_spec=pltpu.PrefetchScalarGridSpec(
            num_scalar_prefetch=7,
            grid=(n_items,),
            in_specs=in_specs,
            out_specs=pl.BlockSpec((MOE_TM, D_MODEL), lambda i, wb, *_: (wb[i], 0)),
            scratch_shapes=[
                pltpu.VMEM((2, MOE_TM, D_MODEL), F32),
                pltpu.VMEM((MOE_TM, D_MODEL), BF16),
                pltpu.VMEM((MOE_TM, D_FF), BF16),
                pltpu.VMEM((2, 2, D_MODEL, MOE_TF), BF16),
                pltpu.SemaphoreType.DMA((2,)),
                pltpu.SemaphoreType.DMA((2,)),
            ],
        ),
        out_shape=jax.ShapeDtypeStruct((nblk * MOE_TM, D_MODEL), F32),
        compiler_params=_cparams(("arbitrary",), 56),
        name="experts",
    )(w_blk, w_exp, w_lo, w_hi, w_first, w_kind, n_work, sorted_tok3, sorted_tok3, xn, w1, b1, w2, b2)


def _combine_body(posc_ref, posn_ref, x2_ref, gate_ref, ys_hbm, outp_ref, outs_ref, buf, sem, *,
                  n_prompt_tiles):
    i = pl.program_id(0)
    n = pl.num_programs(0)
    slot = i % 2
    nrow = TOP_K * COMB_TT

    def issue(pref, s):
        for r in range(COMB_TT):
            for k in range(TOP_K):
                p = pref[0, 0, TOP_K * r + k]
                pltpu.make_async_copy(ys_hbm.at[pl.ds(p, 1), :],
                                      buf.at[s, pl.ds(k * COMB_TT + r, 1), :], sem.at[s]).start()

    @pl.when(i == 0)
    def _():
        issue(posc_ref, 0)

    pltpu.make_async_copy(ys_hbm.at[pl.ds(0, nrow), :], buf.at[slot], sem.at[slot]).wait()

    @pl.when(i + 1 < n)
    def _():
        issue(posn_ref, 1 - slot)

    y = x2_ref[...]
    g = gate_ref[...]
    for k in range(TOP_K):
        y = y + g[:, k:k + 1] * buf[slot, k * COMB_TT:(k + 1) * COMB_TT, :]

    @pl.when(i < n_prompt_tiles)
    def _():
        outp_ref[...] = y

    @pl.when(i >= n_prompt_tiles)
    def _():
        outs_ref[...] = y


def _combine(pos3, x2, gates, ys, n_prompt):
    n_tok = x2.shape[0]
    nt = n_tok // COMB_TT
    npt = n_prompt // COMB_TT
    assert n_tok - n_prompt == COMB_TT
    body = functools.partial(_combine_body, n_prompt_tiles=npt)
    return pl.pallas_call(
        body,
        grid=(nt,),
        in_specs=[
            pl.BlockSpec((1, 1, TOP_K * COMB_TT), lambda i: (i, 0, 0), memory_space=pltpu.SMEM),
            pl.BlockSpec((1, 1, TOP_K * COMB_TT), lambda i: (jnp.minimum(i + 1, nt - 1), 0, 0),
                         memory_space=pltpu.SMEM),
            pl.BlockSpec((COMB_TT, D_MODEL), lambda i: (i, 0)),
            pl.BlockSpec((COMB_TT, LANES), lambda i: (i, 0)),
            pl.BlockSpec(memory_space=pl.ANY),
        ],
        out_specs=[
            pl.BlockSpec((COMB_TT, D_MODEL), lambda i: (jnp.minimum(i, npt - 1), 0)),
            pl.BlockSpec((COMB_TT, D_MODEL), lambda i: (0, 0)),
        ],
        out_shape=[
            jax.ShapeDtypeStruct((n_prompt, D_MODEL), F32),
            jax.ShapeDtypeStruct((COMB_TT, D_MODEL), F32),
        ],
        scratch_shapes=[
            pltpu.VMEM((2, TOP_K * COMB_TT, D_MODEL), F32),
            pltpu.SemaphoreType.DMA((2,)),
        ],
        compiler_params=_cparams(("arbitrary",), 40),
        name="combine",
    )(pos3, pos3, x2, gates, ys)


def _t5_bucket(rel):
    nb = T5_BUCKETS // 2
    max_exact = nb // 2
    ret = (rel > 0).astype(jnp.int32) * nb
    n = jnp.abs(rel)
    large = max_exact + (jnp.log(jnp.maximum(n, 1).astype(F32) / max_exact)
                         / math.log(T5_MAX_DIST / max_exact) * (nb - max_exact)).astype(jnp.int32)
    large = jnp.minimum(large, nb - 1)
    return ret + jnp.where(n < max_exact, n, large)


def _dup_heads(t):
    lead = t.shape[:-1]
    t = t.reshape(*lead, A_KV_HEADS, 1, HEAD_DIM)
    return jnp.broadcast_to(t, (*lead, A_KV_HEADS, 2, HEAD_DIM)).reshape(*lead, A_KVD)


def _work_items(top_i, rank, counts, n_tok):
    nk = n_tok * TOP_K
    assert nk % MOE_TM == 0
    nblk = nk // MOE_TM
    n_items = nblk + N_EXPERTS
    end = jnp.cumsum(counts)
    start = end - counts
    first_blk = start // MOE_TM
    n_it = jnp.where(counts > 0, (end - 1) // MOE_TM - first_blk + 1, 0)
    it_end = jnp.cumsum(n_it)
    it_start = it_end - n_it
    n_work = it_end[-1]
    experts = jnp.arange(N_EXPERTS, dtype=jnp.int32)
    table = lambda tab, idx: jnp.sum(jnp.where(idx[..., None] == experts, tab, 0), axis=-1)
    w = jnp.arange(n_items, dtype=jnp.int32)
    wc = jnp.minimum(w, n_work - 1)
    w_exp = jnp.minimum(jnp.sum((it_end[None, :] <= wc[:, None]).astype(jnp.int32), axis=1), N_EXPERTS - 1)
    w_blk = table(first_blk, w_exp) + wc - table(it_start, w_exp)
    w_lo = jnp.clip(table(start, w_exp) - w_blk * MOE_TM, 0, MOE_TM)
    w_hi = jnp.clip(table(end, w_exp) - w_blk * MOE_TM, 0, MOE_TM)
    w_first = jnp.concatenate([jnp.ones((1,), jnp.int32), (w_blk[1:] != w_blk[:-1]).astype(jnp.int32)])
    pos = table(start, top_i) + rank
    flat_tok = jnp.arange(nk, dtype=jnp.int32) // TOP_K
    _, sorted_tok = lax.sort_key_val(top_i.reshape(-1), flat_tok, is_stable=True)
    i32 = lambda t: t.astype(jnp.int32)
    w_kind = jnp.where(w_hi <= MOE_TM // 2, 1, jnp.where(w_lo >= MOE_TM // 2, 2, 0))
    work = (i32(w_blk), i32(w_exp), i32(w_lo), i32(w_hi), w_first, i32(w_kind), i32(n_work).reshape(1))
    return sorted_tok.reshape(nblk, 1, MOE_TM), work, pos.reshape(n_tok // COMB_TT, 1, TOP_K * COMB_TT)


def _band_bias(vals, keys):
    vt = vals.T
    return jnp.stack([vt[:, CHUNK - 1 - i:CHUNK - 1 - i + keys] for i in range(CHUNK)], axis=1).astype(F32)


def kernel(x_prompt, x_sample, cache_a_k, cache_a_v, cache_b_k, cache_b_v, attn_norm, w_in,
           a_q_norm, a_k_norm, b_q_norm, b_k_norm, a_sinks, t5_table, b_rel_table, w_oa, w_ob,
           w_out, ffn_norm, router_w, router_b, w1, b1, w2, b2):
    batch, seq, _ = x_prompt.shape
    dec_b, dec_s, _ = x_sample.shape
    assert attn_norm.shape[0] == 1, "single layer"
    assert dec_s <= CHUNK and PAST_LEN % CHUNK == 0 and PAST_LEN >= PB
    n_prompt = batch * seq
    n_sample = dec_b * dec_s

    wi = w_in[0]
    c = [0]
    for wdt in (A_QW, A_KVW, A_KVW, B_W, B_W, B_W, D_MODEL, D_MODEL):
        c.append(c[-1] + wdt)
    w_qa, w_ka, w_va, w_qb, w_kb, w_vb, w_ga, w_gb = [wi[:, c[k]:c[k + 1]] for k in range(8)]
    w_perm = jnp.concatenate([w_qa, w_qb, w_kb, w_vb, w_ga, w_gb, _dup_heads(w_ka), _dup_heads(w_va)],
                             axis=1).astype(BF16)
    scale = HEAD_DIM ** -0.5 * LOG2E
    ones = lambda n: jnp.ones((n,), F32)
    cg = jnp.concatenate([
        jnp.tile(a_q_norm[0] * scale, A_HEADS), jnp.tile(b_q_norm[0] * scale, B_HEADS),
        jnp.tile(b_k_norm[0], B_HEADS), ones(B_W), ones(2 * D_MODEL),
        jnp.tile(a_k_norm[0], 2 * A_KV_HEADS), ones(A_KVD)]).reshape(1, PROJ_W).astype(F32)
    hd = jnp.arange(MXU_COLS) // HEAD_DIM
    gm = (hd[:, None] == hd[None, :]).astype(BF16)
    g_attn = attn_norm[0].reshape(1, D_MODEL)
    rel_a = jnp.arange(LA + CHUNK - 1) - PA - (CHUNK - 1)
    rel_b = jnp.arange(LB + CHUNK - 1) - PB - (CHUNK - 1)
    bias_a = _band_bias(t5_table[_t5_bucket(rel_a)] * LOG2E, LA)
    bias_b = _band_bias(b_rel_table[0][jnp.clip(rel_b, -B_REL_CLIP, CHUNK - 1) + B_REL_CLIP] * LOG2E, LB)
    sinks = a_sinks[0].astype(F32) * LOG2E
    woa, wob, wout = w_oa[0].astype(BF16), w_ob[0].astype(BF16), w_out[0].astype(BF16)
    fg = ffn_norm[0].reshape(1, D_MODEL)
    wr = jnp.pad(router_w[0], ((0, 0), (0, LANES - N_EXPERTS))).astype(BF16)
    br = jnp.pad(router_b[0], (0, LANES - N_EXPERTS)).reshape(1, LANES)
    w1b, w2b = w1[0].astype(BF16), w2[0].astype(BF16)
    b1r = b1[0].reshape(N_EXPERTS, 1, 2 * D_FF)
    b2r = b2[0].reshape(N_EXPERTS, 1, D_MODEL)

    xp2 = x_prompt.reshape(n_prompt, D_MODEL)
    h_p, kv_p = _proj(xp2, g_attn, w_perm, cg, gm, tm=1024)
    h_p3 = h_p.reshape(batch, seq, PROJ_W)
    cols_p = dict(qa=0, qb=1, kbp=2, kbc=2, vbp=3, vbc=3, kap=16, kac=16, vap=17, vac=17)
    oa_p, ob_p = _attn(sinks, h_p3, h_p3, h_p3, h_p3, h_p3, h_p3, h_p3, h_p3, h_p3, bias_a, bias_b,
                       cpb=8, first_pos=0, hi_a=LA, hi_b=LB, cols=cols_p)
    n_tok = n_prompt + n_sample

    xs_pad = jnp.pad(x_sample, ((0, 0), (0, CHUNK - dec_s), (0, 0))).reshape(dec_b * CHUNK, D_MODEL)
    h_s, kv_s = _proj(xs_pad, g_attn, w_perm, cg, gm, tm=dec_b * CHUNK)
    h_s3 = h_s.reshape(dec_b, CHUNK, PROJ_W)
    ckb = cache_b_k[0].reshape(dec_b, PB, B_W).astype(BF16)
    cvb = cache_b_v[0].reshape(dec_b, PB, B_W).astype(BF16)
    cka = _dup_heads(cache_a_k[0].reshape(dec_b, PA, A_KVW)).astype(BF16)
    cva = _dup_heads(cache_a_v[0].reshape(dec_b, PA, A_KVW)).astype(BF16)
    cols_s = dict(qa=0, qb=1, kbp=0, kbc=2, vbp=0, vbc=3, kap=0, kac=16, vap=0, vac=17)
    oa_s, ob_s = _attn(sinks, h_s3, ckb, h_s3, cvb, h_s3, cka, h_s3, cva, h_s3, bias_a, bias_b,
                       cpb=1, first_pos=PAST_LEN, hi_a=PA + dec_s, hi_b=PB + dec_s, cols=cols_s)
    keep_rows = lambda t: t[:, :dec_s].reshape(n_sample, t.shape[-1])
    x2, xn, logits = _merge(
        (oa_p.reshape(n_prompt, A_QW), ob_p.reshape(n_prompt, B_W), h_p, xp2),
        (keep_rows(oa_s), keep_rows(ob_s), keep_rows(h_s3), x_sample.reshape(n_sample, D_MODEL)),
        woa, wob, wout, fg, wr, br)

    top_i, gates, rank, counts = _router(logits)
    sorted_tok3, work, pos3 = _work_items(top_i[:, :TOP_K], rank[:, :TOP_K], counts[0, :N_EXPERTS], n_tok)
    ys = _experts(work, sorted_tok3, xn, w1b, b1r, w2b, b2r)
    y_p, y_s = _combine(pos3, x2, gates, ys, n_prompt)

    kv_p3 = kv_p.reshape(batch, seq, KV32_W)
    kv_s3 = kv_s.reshape(dec_b, CHUNK, KV32_W)
    undup = lambda t: t.reshape(*t.shape[:-1], A_KV_HEADS, 2, HEAD_DIM)[..., 0, :]
    heads_b = lambda t: t.reshape(*t.shape[:-1], B_HEADS, HEAD_DIM)
    o_kb, o_vb, o_ka, o_va = 0, B_W, 2 * B_W, 2 * B_W + A_KVD
    p_bk = heads_b(kv_p3[:, seq - PB:, o_kb:o_kb + B_W])[None]
    p_bv = heads_b(kv_p3[:, seq - PB:, o_vb:o_vb + B_W])[None]
    p_ak = undup(kv_p3[:, seq - PA:, o_ka:o_ka + A_KVD])[None]
    p_av = undup(kv_p3[:, seq - PA:, o_va:o_va + A_KVD])[None]
    s_bk = heads_b(kv_s3[:, :dec_s, o_kb:o_kb + B_W])[None]
    s_bv = heads_b(kv_s3[:, :dec_s, o_vb:o_vb + B_W])[None]
    s_ak = undup(kv_s3[:, :dec_s, o_ka:o_ka + A_KVD])[None]
    s_av = undup(kv_s3[:, :dec_s, o_va:o_va + A_KVD])[None]
    return (y_p.reshape(batch, seq, D_MODEL), y_s.reshape(dec_b, dec_s, D_MODEL),
            p_ak, p_av, p_bk, p_bv, s_ak, s_av, s_bk, s_bv)
```

```python
import functools
import math

import jax
import jax.numpy as jnp
from jax import lax
from jax.experimental import pallas as pl
from jax.experimental.pallas import tpu as pltpu

F32 = jnp.float32
BF16 = jnp.bfloat16

D_MODEL = 2048
CHUNK = 64
HEAD_DIM = 64
A_HEADS = 16
A_KV_HEADS = 4
A_PAST_CHUNKS = 2
B_HEADS = 16
B_PAST_CHUNKS = 8
B_REL_CLIP = 256
T5_BUCKETS = 32
T5_MAX_DIST = (A_PAST_CHUNKS + 1) * CHUNK
N_EXPERTS = 32
TOP_K = 4
D_FF = D_MODEL
SWIGLU_ALPHA = 1.702
SWIGLU_LIMIT = 7.0
NORM_EPS = 1e-6
NEG_INF = -1e30
LOG2E = math.log2(math.e)
PAST_LEN = 1024

A_QW = A_HEADS * HEAD_DIM
A_KVW = A_KV_HEADS * HEAD_DIM
B_W = B_HEADS * HEAD_DIM
PA = A_PAST_CHUNKS * CHUNK
PB = B_PAST_CHUNKS * CHUNK
LA = PA + CHUNK
LB = PB + CHUNK

LANES = 128
MXU_COLS = 256

A_KVD = 2 * A_KVW
PROJ_W = A_QW + 3 * B_W + 2 * D_MODEL + 2 * A_KVD
PROJ_TN = 1024
KV32_W = 2 * B_W + 2 * A_KVD
_NORM_TILES = (0, 1, 2)
_SIGMOID_LO, _SIGMOID_HI = 4, 8
_KV_TAIL_TILE = 8

MOE_TM = 512
MOE_TF = 512
MOE_NF = D_FF // MOE_TF
MOE_SHARE = MOE_TM // MOE_NF
COMB_TT = 256
ROUTER_TM = 768
MERGE_TM = 256
ATTN_AHEAD = 4


_VMEM_MIB = dict(proj=56, attn=48, merge=56, router=16, experts=56, combine=40)


def _cparams(sem, call):
    return pltpu.CompilerParams(dimension_semantics=sem, vmem_limit_bytes=_VMEM_MIB[call] * 1024 * 1024)


def _proj_body(x_ref, g_ref, w_ref, cg_ref, gm_ref, h_ref, kv_ref, xn_s):
    j = pl.program_id(1)

    @pl.when(j == 0)
    def _():
        x = x_ref[...]
        ms = jnp.mean(x * x, axis=-1, keepdims=True)
        xn_s[...] = (x * lax.rsqrt(ms + NORM_EPS) * g_ref[...]).astype(BF16)

    is_norm = functools.reduce(jnp.logical_or, [j == t for t in _NORM_TILES])
    is_sig = jnp.logical_and(j >= _SIGMOID_LO, j < _SIGMOID_HI)

    n_strips = PROJ_TN // MXU_COLS

    def strips(epilogues, write_kv, dots_first=False):
        col = [slice(c * MXU_COLS, (c + 1) * MXU_COLS) for c in range(n_strips)]
        mm = lambda cs: jnp.dot(xn_s[...], w_ref[:, cs], preferred_element_type=F32)
        accs = [mm(cs) for cs in col] if dots_first else None
        for c, cs in enumerate(col):
            y = epilogues[c](accs[c] if dots_first else mm(cs), cs)
            h_ref[:, cs] = y.astype(BF16)
            if write_kv:
                kv_ref[:, cs] = y

    def norm(a, cs):
        ss = jnp.dot((a * a).astype(BF16), gm_ref[...], preferred_element_type=F32)
        return a * lax.rsqrt(ss * (1.0 / HEAD_DIM) + NORM_EPS) * cg_ref[:, cs]

    as_is = lambda a, cs: a
    sigmoid = lambda a, cs: 0.5 * jnp.tanh(0.5 * a) + 0.5

    @pl.when(is_norm)
    def _():
        strips([norm] * n_strips, True, dots_first=True)

    @pl.when(is_sig)
    def _():
        strips([sigmoid] * n_strips, False)

    @pl.when(j == _KV_TAIL_TILE)
    def _():
        strips([norm] * (n_strips // 2) + [as_is] * (n_strips // 2), True, dots_first=True)

    @pl.when(j == _SIGMOID_LO - 1)
    def _():
        strips([as_is] * n_strips, True)


def _kv_tile(j):
    return jnp.clip(j - 2, 0, 1) + (j >= _KV_TAIL_TILE).astype(jnp.int32)


def _proj(x2d, g, w_perm, cg, gm, tm):
    n = x2d.shape[0]
    grid = (n // tm, PROJ_W // PROJ_TN)
    return pl.pallas_call(
        _proj_body,
        grid=grid,
        in_specs=[
            pl.BlockSpec((tm, D_MODEL), lambda i, j: (i, 0)),
            pl.BlockSpec((1, D_MODEL), lambda i, j: (0, 0)),
            pl.BlockSpec((D_MODEL, PROJ_TN), lambda i, j: (0, j)),
            pl.BlockSpec((1, PROJ_TN), lambda i, j: (0, j)),
            pl.BlockSpec((MXU_COLS, MXU_COLS), lambda i, j: (0, 0)),
        ],
        out_specs=[
            pl.BlockSpec((tm, PROJ_TN), lambda i, j: (i, j)),
            pl.BlockSpec((tm, PROJ_TN), lambda i, j: (i, _kv_tile(j))),
        ],
        out_shape=[
            jax.ShapeDtypeStruct((n, PROJ_W), BF16),
            jax.ShapeDtypeStruct((n, KV32_W), F32),
        ],
        scratch_shapes=[pltpu.VMEM((tm, D_MODEL), BF16)],
        compiler_params=_cparams(("parallel", "arbitrary"), "proj"),
        name="proj",
    )(x2d, g, w_perm, cg, gm)


def _attn_body(sink_ref, qa_ref, qb_ref, kbp_ref, kbc_ref, vbp_ref, vbc_ref,
               kap_ref, kac_ref, vap_ref, vac_ref, ba_ref, bb_ref,
               oa_ref, ob_ref, kb_s, vb_s, ka_s, va_s, *, cpb, first_pos, hi_a, hi_b):
    i = pl.program_id(1)
    qb_rows = cpb * CHUNK
    kb_s[0:PB, :] = kbp_ref[0]
    kb_s[PB:PB + qb_rows, :] = kbc_ref[0]
    vb_s[0:PB, :] = vbp_ref[0]
    vb_s[PB:PB + qb_rows, :] = vbc_ref[0]
    ka_s[0:PA, :] = kap_ref[0]
    ka_s[PA:PA + qb_rows, :] = kac_ref[0]
    va_s[0:PA, :] = vap_ref[0]
    va_s[PA:PA + qb_rows, :] = vac_ref[0]

    lane_a = lax.broadcasted_iota(jnp.int32, (1, LA), 1)
    lane_b = lax.broadcasted_iota(jnp.int32, (1, LB), 1)
    low_half = lax.broadcasted_iota(jnp.int32, (1, LANES), 1) < HEAD_DIM
    nt = (((1,), (1,)), ((), ()))

    tasks = [("a", p) for p in range(A_HEADS // 2)] + [("b", p) for p in range(B_HEADS // 2)]
    upper = lax.broadcasted_iota(jnp.int32, (2 * CHUNK, 1), 0) >= CHUNK

    def scores(task, r0, valid_a, valid_b):
        mixer, p = task
        cols = slice(p * LANES, (p + 1) * LANES)
        if mixer == "a":
            kcols = slice((p // 2) * LANES, (p // 2 + 1) * LANES)
            q_p, k_p, b_ref, valid, keys = qa_ref[0, pl.ds(r0, CHUNK), cols], ka_s[pl.ds(r0, LA), kcols], ba_ref, valid_a, LA
        else:
            q_p, k_p, b_ref, valid, keys = qb_ref[0, pl.ds(r0, CHUNK), cols], kb_s[pl.ds(r0, LB), cols], bb_ref, valid_b, LB
        q2 = jnp.concatenate([q_p, q_p], axis=0)
        qm = jnp.where(jnp.logical_xor(upper, low_half), q2, jnp.zeros_like(q2))
        bias = b_ref[2 * p:2 * p + 2].reshape(2 * CHUNK, keys)
        s = lax.dot_general(qm, k_p, nt, preferred_element_type=F32) + bias
        if valid is not None:
            s = jnp.where(valid, s, NEG_INF)
        return s

    def attend(task, s, r0):
        mixer, p = task
        if mixer == "a":
            kcols = slice((p // 2) * LANES, (p // 2 + 1) * LANES)
            v_p, sink = va_s[pl.ds(r0, LA), kcols], jnp.where(upper, sink_ref[2 * p + 1], sink_ref[2 * p])
        else:
            v_p, sink = vb_s[pl.ds(r0, LB), slice(p * LANES, (p + 1) * LANES)], None
        m = jnp.max(s, axis=-1, keepdims=True)
        if sink is not None:
            m = jnp.maximum(m, sink)
        e = jnp.exp2(s - m)
        l = jnp.sum(e, axis=-1, keepdims=True)
        if sink is not None:
            l = l + jnp.exp2(sink - m)
        o = jnp.dot(e.astype(BF16), v_p, preferred_element_type=F32) / l
        return jnp.where(low_half, o[:CHUNK], o[CHUNK:])

    def make_chunk(masked):
        def chunk(jc, carry):
            r0 = pl.multiple_of(jc * CHUNK, CHUNK)
            valid_a = valid_b = None
            if masked:
                start = first_pos + (i * cpb + jc) * CHUNK
                valid_a = jnp.logical_and(lane_a >= jnp.maximum(PA - start, 0), lane_a < hi_a)
                valid_b = jnp.logical_and(lane_b >= jnp.maximum(PB - start, 0), lane_b < hi_b)
            pending = {}
            for t in range(len(tasks) + ATTN_AHEAD):
                if t < len(tasks):
                    pending[t] = scores(tasks[t], r0, valid_a, valid_b)
                d = t - ATTN_AHEAD
                if d >= 0:
                    mixer, p = tasks[d]
                    o_ref = oa_ref if mixer == "a" else ob_ref
                    o_ref[0, pl.ds(r0, CHUNK), p * LANES:(p + 1) * LANES] = (
                        attend(tasks[d], pending.pop(d), r0).astype(BF16))
            return carry
        return chunk

    n_masked_blocks = pl.cdiv(max(PB - first_pos, 0), cpb * CHUNK)
    if hi_a < LA or hi_b < LB:
        lax.fori_loop(0, cpb, make_chunk(True), 0)
    else:
        @pl.when(i < n_masked_blocks)
        def _():
            lax.fori_loop(0, cpb, make_chunk(True), 0)

        @pl.when(i >= n_masked_blocks)
        def _():
            lax.fori_loop(0, cpb, make_chunk(False), 0)


def _attn(sinks, hq, kbp, kbc, vbp, vbc, kap, kac, vap, vac, bias_a, bias_b, *,
          cpb, first_pos, hi_a, hi_b, cols):
    nb, s, _ = hq.shape
    qb_rows = cpb * CHUNK
    nblk = s // qb_rows
    rb = qb_rows // PB if qb_rows >= PB else None
    ra = qb_rows // PA if qb_rows >= PA else None

    def prev_map(ratio, col):
        if ratio is None:
            return lambda b, i, *_: (b, 0, col)
        return lambda b, i, *_: (b, jnp.maximum(i * ratio - 1, 0), col)

    def cur_map(col):
        return lambda b, i, *_: (b, i, col)

    in_specs = [
        pl.BlockSpec((1, qb_rows, A_QW), cur_map(cols["qa"])),
        pl.BlockSpec((1, qb_rows, B_W), cur_map(cols["qb"])),
        pl.BlockSpec((1, PB, B_W), prev_map(rb, cols["kbp"])),
        pl.BlockSpec((1, qb_rows, B_W), cur_map(cols["kbc"])),
        pl.BlockSpec((1, PB, B_W), prev_map(rb, cols["vbp"])),
        pl.BlockSpec((1, qb_rows, B_W), cur_map(cols["vbc"])),
        pl.BlockSpec((1, PA, A_KVD), prev_map(ra, cols["kap"])),
        pl.BlockSpec((1, qb_rows, A_KVD), cur_map(cols["kac"])),
        pl.BlockSpec((1, PA, A_KVD), prev_map(ra, cols["vap"])),
        pl.BlockSpec((1, qb_rows, A_KVD), cur_map(cols["vac"])),
        pl.BlockSpec((A_HEADS, CHUNK, LA), lambda b, i, *_: (0, 0, 0)),
        pl.BlockSpec((B_HEADS, CHUNK, LB), lambda b, i, *_: (0, 0, 0)),
    ]
    out_specs = [
        pl.BlockSpec((1, qb_rows, A_QW), lambda b, i, *_: (b, i, 0)),
        pl.BlockSpec((1, qb_rows, B_W), lambda b, i, *_: (b, i, 0)),
    ]
    body = functools.partial(_attn_body, cpb=cpb, first_pos=first_pos, hi_a=hi_a, hi_b=hi_b)
    return pl.pallas_call(
        body,
        grid_spec=pltpu.PrefetchScalarGridSpec(
            num_scalar_prefetch=1,
            grid=(nb, nblk),
            in_specs=in_specs,
            out_specs=out_specs,
            scratch_shapes=[
                pltpu.VMEM((PB + qb_rows, B_W), BF16),
                pltpu.VMEM((PB + qb_rows, B_W), BF16),
                pltpu.VMEM((PA + qb_rows, A_KVD), BF16),
                pltpu.VMEM((PA + qb_rows, A_KVD), BF16),
            ],
        ),
        out_shape=[
            jax.ShapeDtypeStruct((nb, s, A_QW), BF16),
            jax.ShapeDtypeStruct((nb, s, B_W), BF16),
        ],
        compiler_params=_cparams(("parallel", "arbitrary"), "attn"),
        name="attn",
    )(sinks, hq, hq, kbp, kbc, vbp, vbc, kap, kac, vap, vac, bias_a, bias_b)


def _merge_body(oa_p, ob_p, ga_p, gb_p, x_p, oa_s, ob_s, ga_s, gb_s, x_s,
                woa_ref, wob_ref, wout_ref, fg_ref, wr_ref, br_ref, x2_ref, xn_ref, lg_ref, *,
                n_prompt_tiles):
    def tile(oa_ref, ob_ref, ga_ref, gb_ref, x_ref):
        ya = jnp.dot(oa_ref[...], woa_ref[...], preferred_element_type=F32)
        yb = jnp.dot(ob_ref[...], wob_ref[...], preferred_element_type=F32)
        z = ga_ref[...].astype(F32) * ya + gb_ref[...].astype(F32) * yb
        y = jnp.dot(z.astype(BF16), wout_ref[...], preferred_element_type=F32)
        x2 = x_ref[...] + y
        x2_ref[...] = x2
        ms = jnp.mean(x2 * x2, axis=-1, keepdims=True)
        xn = x2 * lax.rsqrt(ms + NORM_EPS) * fg_ref[...]
        xn_ref[...] = xn
        lg_ref[...] = jnp.dot(xn.astype(BF16), wr_ref[...], preferred_element_type=F32) + br_ref[...]

    i = pl.program_id(0)

    @pl.when(i < n_prompt_tiles)
    def _():
        tile(oa_p, ob_p, ga_p, gb_p, x_p)

    @pl.when(i >= n_prompt_tiles)
    def _():
        tile(oa_s, ob_s, ga_s, gb_s, x_s)


def _merge(prompt, sample, w_oa, w_ob, w_out, fg, wr, br):
    tm = MERGE_TM
    n_p, n_s = prompt[3].shape[0], sample[3].shape[0]
    npt, nst = n_p // tm, n_s // tm
    n_out = n_p + n_s
    const = lambda i: (0, 0)
    resident = functools.partial(pl.BlockSpec, index_map=const, pipeline_mode=pl.Buffered(1))

    def token_specs(row):
        return [
            pl.BlockSpec((tm, A_QW), lambda i: (row(i), 0)),
            pl.BlockSpec((tm, B_W), lambda i: (row(i), 0)),
            pl.BlockSpec((tm, D_MODEL), lambda i: (row(i), 2)),
            pl.BlockSpec((tm, D_MODEL), lambda i: (row(i), 3)),
            pl.BlockSpec((tm, D_MODEL), lambda i: (row(i), 0)),
        ]

    in_specs = (token_specs(lambda i: jnp.minimum(i, npt - 1))
                + token_specs(lambda i: jnp.clip(i - npt, 0, nst - 1))
                + [resident((A_QW, D_MODEL)), resident((B_W, D_MODEL)), resident((D_MODEL, D_MODEL)),
                   resident((1, D_MODEL)), resident((D_MODEL, LANES)), resident((1, LANES))])
    oa_p, ob_p, h_p, x_p = prompt
    oa_s, ob_s, h_s, x_s = sample
    return pl.pallas_call(
        functools.partial(_merge_body, n_prompt_tiles=npt),
        grid=(npt + nst,),
        in_specs=in_specs,
        out_specs=[
            pl.BlockSpec((tm, D_MODEL), lambda i: (i, 0)),
            pl.BlockSpec((tm, D_MODEL), lambda i: (i, 0)),
            pl.BlockSpec((tm, LANES), lambda i: (i, 0)),
        ],
        out_shape=[
            jax.ShapeDtypeStruct((n_out, D_MODEL), F32),
            jax.ShapeDtypeStruct((n_out, D_MODEL), F32),
            jax.ShapeDtypeStruct((n_out, LANES), F32),
        ],
        compiler_params=_cparams(("arbitrary",), "merge"),
        name="merge",
    )(oa_p, ob_p, h_p, h_p, x_p, oa_s, ob_s, h_s, h_s, x_s, w_oa, w_ob, w_out, fg, wr, br)


def _router_body(lg_ref, tri_ref, idx_ref, gate_ref, rank_ref, cnt_ref, carry):
    i = pl.program_id(0)

    @pl.when(i == 0)
    def _():
        carry[...] = jnp.zeros_like(carry)

    lane = lax.broadcasted_iota(jnp.int32, lg_ref.shape, 1)
    x = jnp.where(lane < N_EXPERTS, lg_ref[...], -jnp.inf)
    vals, hots = [], []
    idx_out = jnp.zeros(lg_ref.shape, jnp.int32)
    for k in range(TOP_K):
        m = jnp.max(x, axis=-1, keepdims=True)
        am = jnp.min(jnp.where(x == m, lane, LANES), axis=-1, keepdims=True)
        hot = lane == am
        vals.append(m)
        hots.append(hot)
        idx_out = jnp.where(lane == k, am, idx_out)
        x = jnp.where(hot, -jnp.inf, x)
    es = [jnp.exp(v - vals[0]) for v in vals]
    denom = functools.reduce(lambda a, b: a + b, es)
    sel = functools.reduce(jnp.logical_or, hots)
    sel_f = jnp.where(sel, 1.0, 0.0)
    before = jnp.dot(tri_ref[...], sel_f.astype(BF16), preferred_element_type=F32) + carry[...]
    gate_out = jnp.zeros(lg_ref.shape, F32)
    rank_out = jnp.zeros(lg_ref.shape, F32)
    for k in range(TOP_K):
        gate_out = jnp.where(lane == k, es[k] / denom, gate_out)
        rk = jnp.sum(jnp.where(hots[k], before, 0.0), axis=-1, keepdims=True)
        rank_out = jnp.where(lane == k, rk, rank_out)
    idx_ref[...] = idx_out
    gate_ref[...] = gate_out
    rank_ref[...] = rank_out.astype(jnp.int32)
    total = carry[...] + jnp.sum(sel_f, axis=0, keepdims=True)
    carry[...] = total
    cnt_ref[...] = total.astype(jnp.int32)


def _router(logits):
    n = logits.shape[0]
    tm = ROUTER_TM
    assert n % tm == 0
    r = jnp.arange(tm)
    tri = (r[None, :] < r[:, None]).astype(BF16)
    tile = pl.BlockSpec((tm, LANES), lambda i: (i, 0))
    return pl.pallas_call(
        _router_body,
        grid=(n // tm,),
        in_specs=[tile, pl.BlockSpec((tm, tm), lambda i: (0, 0))],
        out_specs=[tile, tile, tile, pl.BlockSpec((1, LANES), lambda i: (0, 0))],
        out_shape=[
            jax.ShapeDtypeStruct((n, LANES), jnp.int32),
            jax.ShapeDtypeStruct((n, LANES), F32),
            jax.ShapeDtypeStruct((n, LANES), jnp.int32),
            jax.ShapeDtypeStruct((1, LANES), jnp.int32),
        ],
        scratch_shapes=[pltpu.VMEM((1, LANES), F32)],
        compiler_params=_cparams(("arbitrary",), "router"),
        name="router",
    )(logits, tri)


def _row_gather(idx_ref, n, src_hbm, dst, sem):
    def body(r, c):
        t = idx_ref[0, 0, r]
        pltpu.make_async_copy(src_hbm.at[pl.ds(t, 1), :], dst.at[pl.ds(r, 1), :], sem).start()
        return c
    lax.fori_loop(0, n, body, 0)


def _expert_body(wblk_ref, wexp_ref, wlo_ref, whi_ref, wfirst_ref, wkind_ref, nv_ref, tokc_ref, tokn_ref,
                 xn_hbm, w1_hbm, b1_ref, w2_ref, b2_ref,
                 out_ref, xg, xb, act_s, w1buf, sem_x, sem_w, *, n_items):
    i = pl.program_id(0)
    nf = MOE_NF
    nv = nv_ref[0]
    slot = i % 2
    e_cur = wexp_ref[i]
    e_next = wexp_ref[jnp.minimum(i + 1, n_items - 1)]

    def rows_landed(s):
        return pltpu.make_async_copy(xn_hbm.at[pl.ds(0, MOE_TM), :], xg.at[s], sem_x.at[s])

    def w1_tile(e, f, s):
        return [pltpu.make_async_copy(w1_hbm.at[e, :, pl.ds(half * D_FF + f * MOE_TF, MOE_TF)],
                                      w1buf.at[s, half], sem_w.at[s]) for half in range(2)]

    @pl.when(i == nv)
    def _():
        rows_landed(slot).wait()
        for c in w1_tile(e_cur, 0, 0):
            c.wait()

    @pl.when(i == 0)
    def _():
        _row_gather(tokc_ref, MOE_TM, xn_hbm, xg.at[0], sem_x.at[0])
        for c in w1_tile(e_cur, 0, 0):
            c.start()

    half = MOE_TM // 2
    kinds = (slice(0, MOE_TM), slice(0, half), slice(half, MOE_TM))
    live = i < nv
    kind = wkind_ref[i]

    def run(rows):
        rows_landed(slot).wait()
        xb[rows] = xg[slot, rows].astype(BF16)
        x = xb[rows]
        for f in range(nf):
            s = f % 2
            for c in w1_tile(e_cur, f, s):
                c.wait()
            nxt = w1_tile(e_cur, f + 1, 1 - s) if f + 1 < nf else w1_tile(e_next, 0, 1 - s)
            for c in nxt:
                c.start(priority=1)
            for r in range(f * MOE_SHARE, (f + 1) * MOE_SHARE):
                t = tokn_ref[0, 0, r]
                pltpu.make_async_copy(xn_hbm.at[pl.ds(t, 1), :], xg.at[1 - slot, pl.ds(r, 1), :],
                                      sem_x.at[1 - slot]).start()
            for c in range(MOE_TF // MXU_COLS):
                cs = slice(c * MXU_COLS, (c + 1) * MXU_COLS)
                gs = slice(f * MOE_TF + c * MXU_COLS, f * MOE_TF + (c + 1) * MXU_COLS)
                us = slice(D_FF + gs.start, D_FF + gs.stop)
                hg = jnp.dot(x, w1buf[s, 0, :, cs], preferred_element_type=F32) + b1_ref[0, :, gs]
                hu = jnp.dot(x, w1buf[s, 1, :, cs], preferred_element_type=F32) + b1_ref[0, :, us]
                hg = jnp.minimum(hg, SWIGLU_LIMIT)
                hu = jnp.clip(hu, -SWIGLU_LIMIT, SWIGLU_LIMIT)
                sig = 0.5 * jnp.tanh((0.5 * SWIGLU_ALPHA) * hg) + 0.5
                act_s[rows, gs] = (hg * sig * (hu + 1.0)).astype(BF16)

    def ffn_out(rows):
        return jnp.dot(act_s[rows], w2_ref[0], preferred_element_type=F32) + b2_ref[0]

    def first_store(k, rows):
        out_ref[rows] = ffn_out(rows)
        if k == 1:
            out_ref[half:] = jnp.zeros((MOE_TM - half, D_MODEL), F32)

    def merge_store(rows):
        row = lax.broadcasted_iota(jnp.int32, (rows.stop - rows.start, 1), 0) + rows.start
        mine = jnp.logical_and(row >= wlo_ref[i], row < whi_ref[i])
        out_ref[rows] = jnp.where(mine, ffn_out(rows), out_ref[rows])

    for k, rows in enumerate(kinds):
        mine_kind = jnp.logical_and(live, kind == k)
        pl.when(mine_kind)(functools.partial(run, rows))
        if k != 2:
            pl.when(jnp.logical_and(mine_kind, wfirst_ref[i] == 1))(functools.partial(first_store, k, rows))
        pl.when(jnp.logical_and(mine_kind, wfirst_ref[i] == 0))(functools.partial(merge_store, rows))


def _experts(work, sorted_tok3, xn, w1, b1, w2, b2):
    w_blk, w_exp, w_lo, w_hi, w_first, w_kind, n_work = work
    n_items = w_blk.shape[0]
    nblk = sorted_tok3.shape[0]
    assert MOE_NF % 2 == 0

    def tok_map(shift):
        return lambda i, wb, *_: (wb[jnp.minimum(i + shift, n_items - 1)], 0, 0)

    by_expert = lambda i, wb, we, *_: (we[i], 0, 0)
    in_specs = [
        pl.BlockSpec((1, 1, MOE_TM), tok_map(0), memory_space=pltpu.SMEM),
        pl.BlockSpec((1, 1, MOE_TM), tok_map(1), memory_space=pltpu.SMEM),
        pl.BlockSpec(memory_space=pl.ANY),
        pl.BlockSpec(memory_space=pl.ANY),
        pl.BlockSpec((1, 1, 2 * D_FF), by_expert),
        pl.BlockSpec((1, D_FF, D_MODEL), by_expert),
        pl.BlockSpec((1, 1, D_MODEL), by_expert),
    ]
    return pl.pallas_call(
        functools.partial(_expert_body, n_items=n_items),
        grid_spec=pltpu.PrefetchScalarGridSpec(
            num_scalar_prefetch=7,
            grid=(n_items,),
            in_specs=in_specs,
            out_specs=pl.BlockSpec((MOE_TM, D_MODEL), lambda i, wb, *_: (wb[i], 0)),
            scratch_shapes=[
                pltpu.VMEM((2, MOE_TM, D_MODEL), F32),
                pltpu.VMEM((MOE_TM, D_MODEL), BF16),
                pltpu.VMEM((MOE_TM, D_FF), BF16),
                pltpu.VMEM((2, 2, D_MODEL, MOE_TF), BF16),
                pltpu.SemaphoreType.DMA((2,)),
                pltpu.SemaphoreType.DMA((2,)),
            ],
        ),
        out_shape=jax.ShapeDtypeStruct((nblk * MOE_TM, D_MODEL), F32),
        compiler_params=_cparams(("arbitrary",), "experts"),
        name="experts",
    )(w_blk, w_exp, w_lo, w_hi, w_first, w_kind, n_work, sorted_tok3, sorted_tok3, xn, w1, b1, w2, b2)


def _combine_body(posc_ref, posn_ref, x2_ref, gate_ref, ys_hbm, outp_ref, outs_ref, buf, sem, *,
                  n_prompt_tiles):
    i = pl.program_id(0)
    n = pl.num_programs(0)
    slot = i % 2
    nrow = TOP_K * COMB_TT

    def issue(pref, s):
        for r in range(COMB_TT):
            for k in range(TOP_K):
                p = pref[0, 0, TOP_K * r + k]
                pltpu.make_async_copy(ys_hbm.at[pl.ds(p, 1), :],
                                      buf.at[s, pl.ds(k * COMB_TT + r, 1), :], sem.at[s]).start(priority=k % 2)

    @pl.when(i == 0)
    def _():
        issue(posc_ref, 0)

    pltpu.make_async_copy(ys_hbm.at[pl.ds(0, nrow), :], buf.at[slot], sem.at[slot]).wait()

    @pl.when(i + 1 < n)
    def _():
        issue(posn_ref, 1 - slot)

    y = x2_ref[...]
    g = gate_ref[...]
    for k in range(TOP_K):
        y = y + g[:, k:k + 1] * buf[slot, k * COMB_TT:(k + 1) * COMB_TT, :]

    @pl.when(i < n_prompt_tiles)
    def _():
        outp_ref[...] = y

    @pl.when(i >= n_prompt_tiles)
    def _():
        outs_ref[...] = y


def _combine(pos3, x2, gates, ys, n_prompt):
    n_tok = x2.shape[0]
    nt = n_tok // COMB_TT
    npt = n_prompt // COMB_TT
    assert n_tok - n_prompt == COMB_TT
    body = functools.partial(_combine_body, n_prompt_tiles=npt)
    return pl.pallas_call(
        body,
        grid=(nt,),
        in_specs=[
            pl.BlockSpec((1, 1, TOP_K * COMB_TT), lambda i: (i, 0, 0), memory_space=pltpu.SMEM),
            pl.BlockSpec((1, 1, TOP_K * COMB_TT), lambda i: (jnp.minimum(i + 1, nt - 1), 0, 0),
                         memory_space=pltpu.SMEM),
            pl.BlockSpec((COMB_TT, D_MODEL), lambda i: (i, 0)),
            pl.BlockSpec((COMB_TT, LANES), lambda i: (i, 0)),
            pl.BlockSpec(memory_space=pl.ANY),
        ],
        out_specs=[
            pl.BlockSpec((COMB_TT, D_MODEL), lambda i: (jnp.minimum(i, npt - 1), 0)),
            pl.BlockSpec((COMB_TT, D_MODEL), lambda i: (0, 0)),
        ],
        out_shape=[
            jax.ShapeDtypeStruct((n_prompt, D_MODEL), F32),
            jax.ShapeDtypeStruct((COMB_TT, D_MODEL), F32),
        ],
        scratch_shapes=[
            pltpu.VMEM((2, TOP_K * COMB_TT, D_MODEL), F32),
            pltpu.SemaphoreType.DMA((2,)),
        ],
        compiler_params=_cparams(("arbitrary",), "combine"),
        name="combine",
    )(pos3, pos3, x2, gates, ys)


def _t5_bucket(rel):
    nb = T5_BUCKETS // 2
    max_exact = nb // 2
    ret = (rel > 0).astype(jnp.int32) * nb
    n = jnp.abs(rel)
    large = max_exact + (jnp.log(jnp.maximum(n, 1).astype(F32) / max_exact)
                         / math.log(T5_MAX_DIST / max_exact) * (nb - max_exact)).astype(jnp.int32)
    large = jnp.minimum(large, nb - 1)
    return ret + jnp.where(n < max_exact, n, large)


def _dup_heads(t):
    lead = t.shape[:-1]
    t = t.reshape(*lead, A_KV_HEADS, 1, HEAD_DIM)
    return jnp.broadcast_to(t, (*lead, A_KV_HEADS, 2, HEAD_DIM)).reshape(*lead, A_KVD)


def _work_items(top_i, rank, counts, n_tok):
    nk = n_tok * TOP_K
    assert nk % MOE_TM == 0
    nblk = nk // MOE_TM
    n_items = nblk + N_EXPERTS
    end = jnp.cumsum(counts)
    start = end - counts
    first_blk = start // MOE_TM
    n_it = jnp.where(counts > 0, (end - 1) // MOE_TM - first_blk + 1, 0)
    it_end = jnp.cumsum(n_it)
    it_start = it_end - n_it
    n_work = it_end[-1]
    experts = jnp.arange(N_EXPERTS, dtype=jnp.int32)
    table = lambda tab, idx: jnp.sum(jnp.where(idx[..., None] == experts, tab, 0), axis=-1)
    w = jnp.arange(n_items, dtype=jnp.int32)
    wc = jnp.minimum(w, n_work - 1)
    w_exp = jnp.minimum(jnp.sum((it_end[None, :] <= wc[:, None]).astype(jnp.int32), axis=1), N_EXPERTS - 1)
    w_blk = table(first_blk, w_exp) + wc - table(it_start, w_exp)
    w_lo = jnp.clip(table(start, w_exp) - w_blk * MOE_TM, 0, MOE_TM)
    w_hi = jnp.clip(table(end, w_exp) - w_blk * MOE_TM, 0, MOE_TM)
    w_first = jnp.concatenate([jnp.ones((1,), jnp.int32), (w_blk[1:] != w_blk[:-1]).astype(jnp.int32)])
    pos = table(start, top_i) + rank
    bits = (nk - 1).bit_length()
    assert bits + (N_EXPERTS - 1).bit_length() < 32
    packed = jnp.sort(top_i.reshape(-1) * (1 << bits) + jnp.arange(nk, dtype=jnp.int32))
    sorted_tok = (packed & ((1 << bits) - 1)) // TOP_K
    i32 = lambda t: t.astype(jnp.int32)
    w_kind = jnp.where(w_hi <= MOE_TM // 2, 1, jnp.where(w_lo >= MOE_TM // 2, 2, 0))
    work = (i32(w_blk), i32(w_exp), i32(w_lo), i32(w_hi), w_first, i32(w_kind), i32(n_work).reshape(1))
    return sorted_tok.reshape(nblk, 1, MOE_TM), work, pos.reshape(n_tok // COMB_TT, 1, TOP_K * COMB_TT)


def _band_bias(vals, keys):
    vt = vals.T
    return jnp.stack([vt[:, CHUNK - 1 - i:CHUNK - 1 - i + keys] for i in range(CHUNK)], axis=1).astype(F32)


def kernel(x_prompt, x_sample, cache_a_k, cache_a_v, cache_b_k, cache_b_v, attn_norm, w_in,
           a_q_norm, a_k_norm, b_q_norm, b_k_norm, a_sinks, t5_table, b_rel_table, w_oa, w_ob,
           w_out, ffn_norm, router_w, router_b, w1, b1, w2, b2):
    batch, seq, _ = x_prompt.shape
    dec_b, dec_s, _ = x_sample.shape
    assert attn_norm.shape[0] == 1, "single layer"
    assert dec_s <= CHUNK and PAST_LEN % CHUNK == 0 and PAST_LEN >= PB
    n_prompt = batch * seq
    n_sample = dec_b * dec_s

    wi = w_in[0]
    c = [0]
    for wdt in (A_QW, A_KVW, A_KVW, B_W, B_W, B_W, D_MODEL, D_MODEL):
        c.append(c[-1] + wdt)
    w_qa, w_ka, w_va, w_qb, w_kb, w_vb, w_ga, w_gb = [wi[:, c[k]:c[k + 1]] for k in range(8)]
    w_perm = jnp.concatenate([w_qa, w_qb, w_kb, w_vb, w_ga, w_gb, _dup_heads(w_ka), _dup_heads(w_va)],
                             axis=1).astype(BF16)
    scale = HEAD_DIM ** -0.5 * LOG2E
    ones = lambda n: jnp.ones((n,), F32)
    cg = jnp.concatenate([
        jnp.tile(a_q_norm[0] * scale, A_HEADS), jnp.tile(b_q_norm[0] * scale, B_HEADS),
        jnp.tile(b_k_norm[0], B_HEADS), ones(B_W), ones(2 * D_MODEL),
        jnp.tile(a_k_norm[0], 2 * A_KV_HEADS), ones(A_KVD)]).reshape(1, PROJ_W).astype(F32)
    hd = jnp.arange(MXU_COLS) // HEAD_DIM
    gm = (hd[:, None] == hd[None, :]).astype(BF16)
    g_attn = attn_norm[0].reshape(1, D_MODEL)
    rel_a = jnp.arange(LA + CHUNK - 1) - PA - (CHUNK - 1)
    rel_b = jnp.arange(LB + CHUNK - 1) - PB - (CHUNK - 1)
    bias_a = _band_bias(t5_table[_t5_bucket(rel_a)] * LOG2E, LA)
    bias_b = _band_bias(b_rel_table[0][jnp.clip(rel_b, -B_REL_CLIP, CHUNK - 1) + B_REL_CLIP] * LOG2E, LB)
    sinks = a_sinks[0].astype(F32) * LOG2E
    woa, wob, wout = w_oa[0].astype(BF16), w_ob[0].astype(BF16), w_out[0].astype(BF16)
    fg = ffn_norm[0].reshape(1, D_MODEL)
    wr = jnp.pad(router_w[0], ((0, 0), (0, LANES - N_EXPERTS))).astype(BF16)
    br = jnp.pad(router_b[0], (0, LANES - N_EXPERTS)).reshape(1, LANES)
    w1b, w2b = w1[0].astype(BF16), w2[0].astype(BF16)
    b1r = b1[0].reshape(N_EXPERTS, 1, 2 * D_FF)
    b2r = b2[0].reshape(N_EXPERTS, 1, D_MODEL)

    xp2 = x_prompt.reshape(n_prompt, D_MODEL)
    h_p, kv_p = _proj(xp2, g_attn, w_perm, cg, gm, tm=1024)
    h_p3 = h_p.reshape(batch, seq, PROJ_W)
    cols_p = dict(qa=0, qb=1, kbp=2, kbc=2, vbp=3, vbc=3, kap=16, kac=16, vap=17, vac=17)
    oa_p, ob_p = _attn(sinks, h_p3, h_p3, h_p3, h_p3, h_p3, h_p3, h_p3, h_p3, h_p3, bias_a, bias_b,
                       cpb=8, first_pos=0, hi_a=LA, hi_b=LB, cols=cols_p)
    n_tok = n_prompt + n_sample

    xs_pad = jnp.pad(x_sample, ((0, 0), (0, CHUNK - dec_s), (0, 0))).reshape(dec_b * CHUNK, D_MODEL)
    h_s, kv_s = _proj(xs_pad, g_attn, w_perm, cg, gm, tm=dec_b * CHUNK)
    h_s3 = h_s.reshape(dec_b, CHUNK, PROJ_W)
    ckb = cache_b_k[0].reshape(dec_b, PB, B_W).astype(BF16)
    cvb = cache_b_v[0].reshape(dec_b, PB, B_W).astype(BF16)
    cka = _dup_heads(cache_a_k[0].reshape(dec_b, PA, A_KVW)).astype(BF16)
    cva = _dup_heads(cache_a_v[0].reshape(dec_b, PA, A_KVW)).astype(BF16)
    cols_s = dict(qa=0, qb=1, kbp=0, kbc=2, vbp=0, vbc=3, kap=0, kac=16, vap=0, vac=17)
    oa_s, ob_s = _attn(sinks, h_s3, ckb, h_s3, cvb, h_s3, cka, h_s3, cva, h_s3, bias_a, bias_b,
                       cpb=1, first_pos=PAST_LEN, hi_a=PA + dec_s, hi_b=PB + dec_s, cols=cols_s)
    keep_rows = lambda t: t[:, :dec_s].reshape(n_sample, t.shape[-1])
    x2, xn, logits = _merge(
        (oa_p.reshape(n_prompt, A_QW), ob_p.reshape(n_prompt, B_W), h_p, xp2),
        (keep_rows(oa_s), keep_rows(ob_s), keep_rows(h_s3), x_sample.reshape(n_sample, D_MODEL)),
        woa, wob, wout, fg, wr, br)

    top_i, gates, rank, counts = _router(logits)
    sorted_tok3, work, pos3 = _work_items(top_i[:, :TOP_K], rank[:, :TOP_K], counts[0, :N_EXPERTS], n_tok)
    ys = _experts(work, sorted_tok3, xn, w1b, b1r, w2b, b2r)
    y_p, y_s = _combine(pos3, x2, gates, ys, n_prompt)

    kv_p3 = kv_p.reshape(batch, seq, KV32_W)
    kv_s3 = kv_s.reshape(dec_b, CHUNK, KV32_W)
    undup = lambda t: t.reshape(*t.shape[:-1], A_KV_HEADS, 2, HEAD_DIM)[..., 0, :]
    heads_b = lambda t: t.reshape(*t.shape[:-1], B_HEADS, HEAD_DIM)
    o_kb, o_vb, o_ka, o_va = 0, B_W, 2 * B_W, 2 * B_W + A_KVD
    p_bk = heads_b(kv_p3[:, seq - PB:, o_kb:o_kb + B_W])[None]
    p_bv = heads_b(kv_p3[:, seq - PB:, o_vb:o_vb + B_W])[None]
    p_ak = undup(kv_p3[:, seq - PA:, o_ka:o_ka + A_KVD])[None]
    p_av = undup(kv_p3[:, seq - PA:, o_va:o_va + A_KVD])[None]
    s_bk = heads_b(kv_s3[:, :dec_s, o_kb:o_kb + B_W])[None]
    s_bv = heads_b(kv_s3[:, :dec_s, o_vb:o_vb + B_W])[None]
    s_ak = undup(kv_s3[:, :dec_s, o_ka:o_ka + A_KVD])[None]
    s_av = undup(kv_s3[:, :dec_s, o_va:o_va + A_KVD])[None]
    return (y_p.reshape(batch, seq, D_MODEL), y_s.reshape(dec_b, dec_s, D_MODEL),
            p_ak, p_av, p_bk, p_bv, s_ak, s_av, s_bk, s_bv)
```
